```python
import jax, jax.numpy as jnp
from jax import lax
import numpy as np

D_MODEL = 1024
BATCH = 16
SEQ = 2048
DEPTH = 1

CHUNK = 64
N_MEM = 256
BRANCH_WIDTH = 512
N_BRANCH = 3
HEAD_DIM = 64
FOX_HEADS = BRANCH_WIDTH // HEAD_DIM
CONV_K = 3
CONV_GROUPS = 8
XA_HEADS = 4
XA_HEAD_DIM = BRANCH_WIDTH // XA_HEADS
Q_BLOCK = 128
N_EXPERTS = 32
TOP_K = 4
D_FF = D_MODEL
SWIGLU_ALPHA = 1.702
SWIGLU_LIMIT = 7.0
NORM_EPS = 1e-5
QK_EPS = 1e-6
PROJ_SPLITS = (BRANCH_WIDTH, BRANCH_WIDTH, BRANCH_WIDTH, FOX_HEADS,
               BRANCH_WIDTH, BRANCH_WIDTH, BRANCH_WIDTH, BRANCH_WIDTH, N_BRANCH * D_MODEL)
PROJ_COLS = 7 * BRANCH_WIDTH + FOX_HEADS + N_BRANCH * D_MODEL

kernel_name = 'hybrid_fox_shortconv_memxattn_moe'


def rms_norm(x, g, eps):
    xf = x.astype(jnp.float32)
    y = xf * lax.rsqrt(jnp.mean(xf * xf, axis=-1, keepdims=True) + eps)
    return (y * g.astype(jnp.float32)).astype(x.dtype)


def forgetting_attention(q, k, v, log_f):
    S = q.shape[1]
    Dh = q.shape[-1]
    scale = Dh ** -0.5
    cum = jnp.cumsum(log_f, axis=1).transpose(0, 2, 1)
    outs = []
    for i in range(S // Q_BLOCK):
        q0 = i * Q_BLOCK
        kend = q0 + Q_BLOCK
        qb = q[:, q0:kend]
        kb = k[:, :kend]
        vb = v[:, :kend]
        s = jnp.einsum('bqhd,bkhd->bhqk', qb, kb, preferred_element_type=jnp.float32) * scale
        bias = cum[:, :, q0:kend, None] - cum[:, :, None, :kend]
        t_idx = q0 + jnp.arange(Q_BLOCK)[:, None]
        s_idx = jnp.arange(kend)[None, :]
        s = jnp.where(s_idx <= t_idx, s + bias, -jnp.inf)
        p = jax.nn.softmax(s, axis=-1).astype(vb.dtype)
        outs.append(jnp.einsum('bhqk,bkhd->bqhd', p, vb))
    return jnp.concatenate(outs, axis=1)


def causal_short_conv(u, w):
    S = u.shape[1]
    up = jnp.pad(u, ((0, 0), (CONV_K - 1, 0), (0, 0)))
    y = up[:, 0:S] * w[0]
    for j in range(1, CONV_K):
        y = y + up[:, j:j + S] * w[j]
    return y


def memory_cross_attention(q, mk, mv):
    scale = q.shape[-1] ** -0.5
    s = jnp.einsum('bshd,bmhd->bhsm', q, mk, preferred_element_type=jnp.float32) * scale
    p = jax.nn.softmax(s, axis=-1).astype(mv.dtype)
    return jnp.einsum('bhsm,bmhd->bshd', p, mv)


def clamped_swiglu(h):
    glu, lin = h[..., :D_FF], h[..., D_FF:]
    glu = jnp.minimum(glu, SWIGLU_LIMIT)
    lin = jnp.clip(lin, -SWIGLU_LIMIT, SWIGLU_LIMIT)
    return glu * jax.nn.sigmoid(SWIGLU_ALPHA * glu) * (lin + 1.0)


def moe(x2, w_router, b_router, w_up, b_up, w_down, b_down):
    logits = (x2 @ w_router + b_router).astype(jnp.float32)
    top_vals, top_idx = lax.top_k(logits, TOP_K)
    top_w = jax.nn.softmax(top_vals, axis=-1)
    combine = jnp.einsum('tk,tke->te', top_w,
                         jax.nn.one_hot(top_idx, N_EXPERTS, dtype=jnp.float32)).astype(x2.dtype)
    out = jnp.zeros_like(x2)
    for e in range(N_EXPERTS):
        a = clamped_swiglu(x2 @ w_up[e] + b_up[e])
        out = out + combine[:, e:e + 1] * (a @ w_down[e] + b_down[e])
    return out


def hybrid_layer(x, mem, norm1_g, w_in, b_forget, b_gate, fox_q_g, fox_k_g, conv_w,
                 mem_norm_g, w_mem_kv, xa_q_g, xa_k_g, w_branch, w_out, norm2_g,
                 w_router, b_router, w_up, b_up, w_down, b_down):
    B, S, D = x.shape
    M = mem.shape[1]
    h = rms_norm(x, norm1_g, NORM_EPS)
    p = h @ w_in
    idx = [int(v) for v in np.cumsum(PROJ_SPLITS)[:-1]]
    fq, fk, fv, f_logit, cu, cc, cb, xq, g_logit = jnp.split(p, idx, axis=-1)

    q = rms_norm(fq.reshape(B, S, FOX_HEADS, HEAD_DIM), fox_q_g, QK_EPS)
    k = rms_norm(fk.reshape(B, S, FOX_HEADS, HEAD_DIM), fox_k_g, QK_EPS)
    v = fv.reshape(B, S, FOX_HEADS, HEAD_DIM)
    log_f = jax.nn.log_sigmoid(f_logit.astype(jnp.float32) + b_forget.astype(jnp.float32))
    y_fox = forgetting_attention(q, k, v, log_f).reshape(B, S, BRANCH_WIDTH)

    y_conv = cb * causal_short_conv(cc * cu, conv_w)

    mn = rms_norm(mem, mem_norm_g, NORM_EPS)
    mk, mv = jnp.split(mn @ w_mem_kv, 2, axis=-1)
    mk = rms_norm(mk.reshape(B, M, XA_HEADS, XA_HEAD_DIM), xa_k_g, QK_EPS)
    mv = mv.reshape(B, M, XA_HEADS, XA_HEAD_DIM)
    xq = rms_norm(xq.reshape(B, S, XA_HEADS, XA_HEAD_DIM), xa_q_g, QK_EPS)
    y_mem = memory_cross_attention(xq, mk, mv).reshape(B, S, BRANCH_WIDTH)

    branches = jnp.stack([y_fox, y_conv, y_mem], axis=2)
    proj = jnp.einsum('bsnc,ncd->bsnd', branches, w_branch)
    gates = jax.nn.sigmoid(g_logit.reshape(B, S, N_BRANCH, D) + b_gate)
    merged = jnp.sum(gates * proj, axis=2)
    x1 = x + merged @ w_out

    h2 = rms_norm(x1, norm2_g, NORM_EPS).reshape(B * S, D)
    y = moe(h2, w_router, b_router, w_up, b_up, w_down, b_down).reshape(B, S, D)
    return x1 + y


def setup_inputs(seed: int = 0) -> dict:
    key = jax.random.key(seed)
    ks = jax.random.split(key, 24)
    L = DEPTH
    f32 = jnp.float32

    def nrm(k, shape, scale):
        return jax.random.normal(k, shape, f32) * scale

    return {
        'x': nrm(ks[0], (BATCH, SEQ, D_MODEL), 1.0),
        'mem': nrm(ks[1], (BATCH, N_MEM, D_MODEL), 1.0),
        'norm1_g': 1.0 + nrm(ks[2], (L, D_MODEL), 0.02),
        'w_in': nrm(ks[3], (L, D_MODEL, PROJ_COLS), D_MODEL ** -0.5),
        'b_forget': 3.0 + nrm(ks[4], (L, FOX_HEADS), 0.5),
        'b_gate': nrm(ks[5], (L, N_BRANCH, D_MODEL), 0.1),
        'fox_q_g': 1.0 + nrm(ks[6], (L, HEAD_DIM), 0.02),
        'fox_k_g': 1.0 + nrm(ks[7], (L, HEAD_DIM), 0.02),
        'conv_w': nrm(ks[8], (L, CONV_K, BRANCH_WIDTH), CONV_K ** -0.5),
        'mem_norm_g': 1.0 + nrm(ks[9], (L, D_MODEL), 0.02),
        'w_mem_kv': nrm(ks[10], (L, D_MODEL, 2 * BRANCH_WIDTH), D_MODEL ** -0.5),
        'xa_q_g': 1.0 + nrm(ks[11], (L, XA_HEAD_DIM), 0.02),
        'xa_k_g': 1.0 + nrm(ks[12], (L, XA_HEAD_DIM), 0.02),
        'w_branch': nrm(ks[13], (L, N_BRANCH, BRANCH_WIDTH, D_MODEL), BRANCH_WIDTH ** -0.5),
        'w_out': nrm(ks[14], (L, D_MODEL, D_MODEL), D_MODEL ** -0.5),
        'norm2_g': 1.0 + nrm(ks[15], (L, D_MODEL), 0.02),
        'w_router': nrm(ks[16], (L, D_MODEL, N_EXPERTS), D_MODEL ** -0.5),
        'b_router': nrm(ks[17], (L, N_EXPERTS), 0.01),
        'w_up': nrm(ks[18], (L, N_EXPERTS, D_MODEL, 2 * D_FF), D_MODEL ** -0.5),
        'b_up': nrm(ks[19], (L, N_EXPERTS, 2 * D_FF), 0.01),
        'w_down': nrm(ks[20], (L, N_EXPERTS, D_FF, D_MODEL), D_FF ** -0.5),
        'b_down': nrm(ks[21], (L, N_EXPERTS, D_MODEL), 0.01),
    }


def reference(x, mem, norm1_g, w_in, b_forget, b_gate, fox_q_g, fox_k_g, conv_w,
              mem_norm_g, w_mem_kv, xa_q_g, xa_k_g, w_branch, w_out, norm2_g,
              w_router, b_router, w_up, b_up, w_down, b_down):
    for l in range(DEPTH):
        x = hybrid_layer(x, mem, norm1_g[l], w_in[l], b_forget[l], b_gate[l], fox_q_g[l],
                         fox_k_g[l], conv_w[l], mem_norm_g[l], w_mem_kv[l], xa_q_g[l],
                         xa_k_g[l], w_branch[l], w_out[l], norm2_g[l], w_router[l],
                         b_router[l], w_up[l], b_up[l], w_down[l], b_down[l])
    return x
```

```python
import functools

import jax
import jax.numpy as jnp
from jax import lax
from jax.experimental import pallas as pl
from jax.experimental.pallas import tpu as pltpu

F32 = jnp.float32
BF16 = jnp.bfloat16

NORM_EPS = 1e-5
QK_EPS = 1e-6
TOP_K = 4
SWIGLU_ALPHA = 1.702
SWIGLU_LIMIT = 7.0
NEG_BIG = -1e30

LANES = 128
SUBLANES = 8
MXU_DIM = 256
VMEM_LIMIT_BYTES = 56 * 1024 * 1024

TM_QKV = 512
TQ_FOX = 256
TM_MERGE = 256
TM_ROWS = 256
TM_EXPERT = 256


def _params(n_axes):
    return pltpu.CompilerParams(dimension_semantics=("arbitrary",) * n_axes,
                                vmem_limit_bytes=VMEM_LIMIT_BYTES)


def _const_spec(shape):
    zeros = (0,) * len(shape)
    return pl.BlockSpec(shape, lambda *_: zeros)


def _rms(x, g, eps):
    return x * lax.rsqrt(jnp.mean(x * x, axis=-1, keepdims=True) + eps) * g


def _split_bf16(x):
    hi = x.astype(BF16)
    lo = (x - hi.astype(F32)).astype(BF16)
    return hi, lo


def _dot(a, b):
    return jnp.dot(a, b, preferred_element_type=F32)


def _dot_nt(a, b):
    return lax.dot_general(a, b, (((1,), (1,)), ((), ())), preferred_element_type=F32)


def _memkv_kernel(mem_ref, g_ref, w_ref, kg_ref, mk_ref, mv_ref, *, n_heads, hd):
    y = _rms(mem_ref[...], g_ref[...], NORM_EPS)
    kv = _dot(y.astype(BF16), w_ref[...])
    width = n_heads * hd
    for h in range(n_heads):
        sl = slice(h * hd, (h + 1) * hd)
        mk_ref[:, sl] = _rms(kv[:, sl], kg_ref[...], QK_EPS).astype(BF16)
    mv_ref[...] = kv[:, width:].astype(BF16)


def _memkv(mem2, g, w_kv, kg, n_heads, hd):
    rows, d = mem2.shape
    width = n_heads * hd
    tm = min(rows, 512)
    return pl.pallas_call(
        functools.partial(_memkv_kernel, n_heads=n_heads, hd=hd),
        grid=(rows // tm,),
        in_specs=[pl.BlockSpec((tm, d), lambda i: (i, 0)),
                  _const_spec((1, d)), _const_spec((d, 2 * width)), _const_spec((1, hd))],
        out_specs=[pl.BlockSpec((tm, width), lambda i: (i, 0)),
                   pl.BlockSpec((tm, width), lambda i: (i, 0))],
        out_shape=[jax.ShapeDtypeStruct((rows, width), BF16)] * 2,
        compiler_params=_params(1),
        name="memkv",
    )(mem2, g, w_kv, kg)


def _qkv_kernel(x_ref, g1_ref, wqkv_ref, wf_ref, bf_ref, gq_ref, gk_ref, bd_ref, tri_ref,
                q_ref, k_ref, v_ref, cum_ref, cumt_ref, carry_ref,
                *, tiles_per_batch, width, hd, fh_pad):
    i = pl.program_id(0)

    @pl.when(i % tiles_per_batch == 0)
    def _():
        carry_ref[...] = jnp.zeros_like(carry_ref)

    hb = _rms(x_ref[...], g1_ref[...], NORM_EPS).astype(BF16)
    qkv = _dot(hb, wqkv_ref[...])

    def head_norm(z, g):
        hi, lo = _split_bf16(z * z)
        parts = []
        for c in range(width // MXU_DIM):
            sl = slice(c * MXU_DIM, (c + 1) * MXU_DIM)
            parts.append(_dot(hi[:, sl], bd_ref[...]) + _dot(lo[:, sl], bd_ref[...]))
        ms = jnp.concatenate(parts, axis=-1) * (1.0 / hd)
        return z * lax.rsqrt(ms + QK_EPS) * g

    q = head_norm(qkv[:, :width], gq_ref[...]) * (hd ** -0.5)
    k = head_norm(qkv[:, width:2 * width], gk_ref[...])
    q_ref[...] = q.astype(BF16)
    k_ref[...] = k.astype(BF16)
    v_ref[...] = qkv[:, 2 * width:].astype(BF16)

    z = _dot(hb, wf_ref[...]) + bf_ref[...]
    log_f = jnp.minimum(z, 0.0) - jnp.log(1.0 + jnp.exp(-jnp.abs(z)))
    hi, lo = _split_bf16(log_f)
    cum = _dot(tri_ref[...], hi) + _dot(tri_ref[...], lo) + carry_ref[0:1, :]
    tm = cum.shape[0]
    carry_ref[...] = jnp.broadcast_to(cum[tm - 1:tm, :], carry_ref.shape)
    cum_ref[...] = cum
    cumt_ref[...] = cum.T[:fh_pad, :]


def _qkv(x2, g1, wqkv, wf, bf, gq_t, gk_t, seq, width, hd, fh_pad):
    t, d = x2.shape
    tm = min(TM_QKV, seq)
    bd = (jnp.arange(MXU_DIM)[:, None] // hd == jnp.arange(MXU_DIM)[None, :] // hd).astype(BF16)
    tri = (jnp.arange(tm)[:, None] >= jnp.arange(tm)[None, :]).astype(BF16)
    row = lambda i: (i, 0)
    return pl.pallas_call(
        functools.partial(_qkv_kernel, tiles_per_batch=seq // tm, width=width, hd=hd, fh_pad=fh_pad),
        grid=(t // tm,),
        in_specs=[pl.BlockSpec((tm, d), row), _const_spec((1, d)), _const_spec((d, 3 * width)),
                  _const_spec((d, LANES)), _const_spec((1, LANES)), _const_spec((1, width)),
                  _const_spec((1, width)), _const_spec((MXU_DIM, MXU_DIM)), _const_spec((tm, tm))],
        out_specs=[pl.BlockSpec((tm, width), row)] * 3
                  + [pl.BlockSpec((tm, LANES), row), pl.BlockSpec((fh_pad, tm), lambda i: (0, i))],
        out_shape=[jax.ShapeDtypeStruct((t, width), BF16)] * 3
                  + [jax.ShapeDtypeStruct((t, LANES), F32), jax.ShapeDtypeStruct((fh_pad, t), F32)],
        scratch_shapes=[pltpu.VMEM((SUBLANES, LANES), F32)],
        compiler_params=_params(1),
        name="qkv",
    )(x2, g1, wqkv, wf, bf, gq_t, gk_t, bd, tri)


def _fox_kernel(q_ref, k_ref, v_ref, cum_ref, cumt_ref, o_ref, *, tq, hd):
    hp = pl.program_id(1)
    i = pl.program_id(2)
    lane = lax.broadcasted_iota(jnp.int32, (1, LANES), 1)
    lane_c = lax.broadcasted_iota(jnp.int32, (tq, LANES), 1)
    rows = lax.broadcasted_iota(jnp.int32, (tq, tq), 0)
    cols = lax.broadcasted_iota(jnp.int32, (tq, tq), 1)
    q = q_ref[...]
    cum_q = cum_ref[...]
    outs = []
    for hh in range(LANES // hd):
        h = hp * (LANES // hd) + hh
        in_head = (lane >= hh * hd) & (lane < (hh + 1) * hd)
        qh = jnp.where(in_head, q, jnp.zeros_like(q))
        col = jnp.sum(jnp.where(lane_c == h, cum_q, 0.0), axis=-1, keepdims=True)

        def step(j, carry, masked):
            m, acc = carry
            start = pl.multiple_of(j * tq, tq)
            kj = k_ref[pl.ds(start, tq), :]
            vj = v_ref[pl.ds(start, tq), :]
            vj = jnp.where(in_head, vj, jnp.ones_like(vj))
            s = _dot_nt(qh, kj) + (col - cumt_ref[pl.ds(h, 1), pl.ds(start, tq)])
            if masked:
                s = jnp.where(cols <= rows, s, NEG_BIG)
            m_new = jnp.maximum(m, jnp.max(s, axis=-1, keepdims=True))
            alpha = jnp.exp(m - m_new)
            p = jnp.exp(s - m_new)
            acc = alpha * acc + _dot(p.astype(BF16), vj)
            return m_new, acc

        init = (jnp.full((tq, 1), NEG_BIG, F32), jnp.zeros((tq, LANES), F32))
        carry = lax.fori_loop(0, i, lambda j, c: step(j, c, False), init)
        _, acc = step(i, carry, True)
        other = ((hh + 1) % (LANES // hd)) * hd
        outs.append(acc / acc[:, other:other + 1])
    o = outs[-1]
    for hh in range(LANES // hd - 2, -1, -1):
        o = jnp.where(lane < (hh + 1) * hd, outs[hh], o)
    o_ref[...] = o.astype(BF16)


def _fox(q, k, v, cum, cumt, batch, seq, width, hd, fh_pad):
    t = q.shape[0]
    tq = min(TQ_FOX, seq)
    nq = seq // tq
    qmap = lambda b, hp, i: (b * nq + i, hp)
    kvmap = lambda b, hp, i: (b, hp)
    return pl.pallas_call(
        functools.partial(_fox_kernel, tq=tq, hd=hd),
        grid=(batch, width // LANES, nq),
        in_specs=[pl.BlockSpec((tq, LANES), qmap), pl.BlockSpec((seq, LANES), kvmap),
                  pl.BlockSpec((seq, LANES), kvmap),
                  pl.BlockSpec((tq, LANES), lambda b, hp, i: (b * nq + i, 0)),
                  pl.BlockSpec((fh_pad, seq), lambda b, hp, i: (0, b))],
        out_specs=pl.BlockSpec((tq, LANES), qmap),
        out_shape=jax.ShapeDtypeStruct((t, width), BF16),
        compiler_params=_params(3),
        name="fox",
    )(q, k, v, cum, cumt)


def _merge_kernel(x_ref, yf_ref, mk_ref, mv_ref, g1_ref, wcat_ref, bg_ref, cw_ref, xqg_ref,
                  wbr_ref, wout_ref, g2_ref, wrh_ref, wrl_ref, br_ref, ltri_ref,
                  x1_ref, h2_ref, idx_ref, tw_ref, rank_ref, cnt_ref,
                  prev_ref, carry_ref, *, tiles_per_batch, width, xa_heads, xa_hd):
    i = pl.program_id(0)
    tm, d = x_ref.shape

    @pl.when(i % tiles_per_batch == 0)
    def _():
        prev_ref[...] = jnp.zeros_like(prev_ref)

    @pl.when(i == 0)
    def _():
        carry_ref[...] = jnp.zeros_like(carry_ref)

    x = x_ref[...]
    hb = _rms(x, g1_ref[...], NORM_EPS).astype(BF16)
    pc = _dot(hb, wcat_ref[...])

    uc = pc[:, :width] * pc[:, width:2 * width]
    row = lax.broadcasted_iota(jnp.int32, (tm, 1), 0)
    p1 = prev_ref[SUBLANES - 1:SUBLANES, :]
    p2 = prev_ref[SUBLANES - 2:SUBLANES - 1, :]
    m1 = jnp.where(row == 0, p1, pltpu.roll(uc, 1, 0))
    m2 = jnp.where(row == 0, p2, jnp.where(row == 1, p1, pltpu.roll(uc, 2, 0)))
    y_conv = pc[:, 2 * width:3 * width] * (cw_ref[0:1, :] * m2 + cw_ref[1:2, :] * m1 + cw_ref[2:3, :] * uc)
    prev_ref[...] = uc[tm - SUBLANES:tm, :]

    ys = []
    for h in range(xa_heads):
        sl = slice(3 * width + h * xa_hd, 3 * width + (h + 1) * xa_hd)
        ml = slice(h * xa_hd, (h + 1) * xa_hd)
        qh = _rms(pc[:, sl], xqg_ref[...], QK_EPS) * (xa_hd ** -0.5)
        s = _dot_nt(qh.astype(BF16), mk_ref[:, ml])
        p = jnp.exp(s - jnp.max(s, axis=-1, keepdims=True))
        ys.append(_dot(p.astype(BF16), mv_ref[:, ml]) / jnp.sum(p, axis=-1, keepdims=True))
    y_mem = jnp.concatenate(ys, axis=-1)

    merged = jnp.zeros((tm, d), F32)
    for n, yb in enumerate((yf_ref[...], y_conv.astype(BF16), y_mem.astype(BF16))):
        gate = jax.nn.sigmoid(pc[:, 4 * width + n * d:4 * width + (n + 1) * d] + bg_ref[n:n + 1, :])
        merged = merged + gate * _dot(yb, wbr_ref[n])
    x1 = x + _dot(merged.astype(BF16), wout_ref[...])
    x1_ref[...] = x1
    h2 = _rms(x1, g2_ref[...], NORM_EPS)
    h2_ref[...] = h2

    hi, lo = _split_bf16(h2)
    logits = _dot(hi, wrh_ref[...]) + _dot(hi, wrl_ref[...]) + _dot(lo, wrh_ref[...]) + br_ref[...]
    lane = lax.broadcasted_iota(jnp.int32, (tm, LANES), 1)
    vals, idxs = [], []
    onehot = jnp.zeros((tm, LANES), F32)
    for _ in range(TOP_K):
        m = jnp.max(logits, axis=-1, keepdims=True)
        idx = jnp.min(jnp.where(logits == m, lane, LANES), axis=-1, keepdims=True)
        sel = lane == idx
        logits = jnp.where(sel, -jnp.inf, logits)
        onehot = onehot + sel.astype(F32)
        vals.append(m)
        idxs.append(idx)
    es = [jnp.exp(v - vals[0]) for v in vals]
    den = es[0]
    for e in es[1:]:
        den = den + e

    excl = _dot(ltri_ref[...], onehot.astype(BF16)) + carry_ref[0:1, :]
    carry_ref[...] = carry_ref[...] + jnp.sum(onehot, axis=0, keepdims=True)
    cnt_ref[...] = carry_ref[...]

    idx_o = jnp.zeros((tm, LANES), jnp.int32)
    tw_o = jnp.zeros((tm, LANES), F32)
    rk_o = jnp.zeros((tm, LANES), jnp.int32)
    for kk in range(TOP_K):
        rank = jnp.sum(jnp.where(lane == idxs[kk], excl, 0.0), axis=-1, keepdims=True)
        idx_o = jnp.where(lane == kk, idxs[kk], idx_o)
        tw_o = jnp.where(lane == kk, es[kk] / den, tw_o)
        rk_o = jnp.where(lane == kk, rank.astype(jnp.int32), rk_o)
    idx_ref[...] = idx_o
    tw_ref[...] = tw_o
    rank_ref[...] = rk_o


def _merge(x2, y_fox, mk, mv, g1, wcat, bg, cw, xqg, wbr, wout, g2, wrh, wrl, br,
           seq, n_mem, width, xa_heads, xa_hd):
    t, d = x2.shape
    tm = min(TM_MERGE, seq)
    tpb = seq // tm
    ltri = (jnp.arange(tm)[:, None] > jnp.arange(tm)[None, :]).astype(BF16)
    row = lambda i: (i, 0)
    mem_map = lambda i: (i // tpb, 0)
    return pl.pallas_call(
        functools.partial(_merge_kernel, tiles_per_batch=tpb, width=width, xa_heads=xa_heads, xa_hd=xa_hd),
        grid=(t // tm,),
        in_specs=[pl.BlockSpec((tm, d), row), pl.BlockSpec((tm, width), row),
                  pl.BlockSpec((n_mem, width), mem_map), pl.BlockSpec((n_mem, width), mem_map),
                  _const_spec((1, d)), _const_spec(wcat.shape), _const_spec(bg.shape), _const_spec(cw.shape),
                  _const_spec((1, xa_hd)), _const_spec(wbr.shape), _const_spec((d, d)), _const_spec((1, d)),
                  _const_spec((d, LANES)), _const_spec((d, LANES)), _const_spec((1, LANES)),
                  _const_spec((tm, tm))],
        out_specs=[pl.BlockSpec((tm, d), row), pl.BlockSpec((tm, d), row),
                   pl.BlockSpec((tm, LANES), row), pl.BlockSpec((tm, LANES), row),
                   pl.BlockSpec((tm, LANES), row), _const_spec((SUBLANES, LANES))],
        out_shape=[jax.ShapeDtypeStruct((t, d), F32), jax.ShapeDtypeStruct((t, d), F32),
                   jax.ShapeDtypeStruct((t, LANES), jnp.int32), jax.ShapeDtypeStruct((t, LANES), F32),
                   jax.ShapeDtypeStruct((t, LANES), jnp.int32), jax.ShapeDtypeStruct((SUBLANES, LANES), F32)],
        scratch_shapes=[pltpu.VMEM((SUBLANES, width), F32), pltpu.VMEM((SUBLANES, LANES), F32)],
        compiler_params=_params(1),
        name="merge",
    )(x2, y_fox, mk, mv, g1, wcat, bg, cw, xqg, wbr, wout, g2, wrh, wrl, br, ltri)


def _row_copy(src_ref, src_row, dst_ref, dst_row, sem):
    return pltpu.make_async_copy(src_ref.at[pl.ds(src_row, 1)], dst_ref.at[pl.ds(dst_row, 1)], sem)


def _dispatch_kernel(pad_start_ref, pad_cnt_ref, pos_ref, h2_ref, xs_ref, zero_ref, sem, *, n_exp):
    i = pl.program_id(0)
    tm = h2_ref.shape[0]

    @pl.when(i == 0)
    def _():
        zero_ref[...] = jnp.zeros_like(zero_ref)

        def per_expert(e, c):
            start = pad_start_ref[e]
            n = pad_cnt_ref[e]

            def issue(r, c2):
                _row_copy(zero_ref, 0, xs_ref, start + r, sem).start()
                return c2

            def drain(r, c2):
                _row_copy(zero_ref, 0, xs_ref, start + r, sem).wait()
                return c2

            lax.fori_loop(0, n, issue, 0)
            lax.fori_loop(0, n, drain, 0)
            return c

        lax.fori_loop(0, n_exp, per_expert, 0)

    def issue(r, c):
        for kk in range(TOP_K):
            _row_copy(h2_ref, r, xs_ref, pos_ref[r * TOP_K + kk], sem).start()
        return c

    def drain(r, c):
        for kk in range(TOP_K):
            _row_copy(h2_ref, r, xs_ref, pos_ref[r * TOP_K + kk], sem).wait()
        return c

    lax.fori_loop(0, tm, issue, 0)
    lax.fori_loop(0, tm, drain, 0)


def _dispatch(pad_start, pad_cnt, pos_flat, h2, n_rows, n_exp):
    t, d = h2.shape
    tm = min(TM_ROWS, t)
    grid_spec = pltpu.PrefetchScalarGridSpec(
        num_scalar_prefetch=2,
        grid=(t // tm,),
        in_specs=[pl.BlockSpec((tm * TOP_K,), lambda i, *_: (i,), memory_space=pltpu.SMEM),
                  pl.BlockSpec((tm, d), lambda i, *_: (i, 0))],
        out_specs=pl.BlockSpec(memory_space=pl.ANY),
        scratch_shapes=[pltpu.VMEM((SUBLANES, d), F32), pltpu.SemaphoreType.DMA(())],
    )
    return pl.pallas_call(
        functools.partial(_dispatch_kernel, n_exp=n_exp),
        grid_spec=grid_spec,
        out_shape=jax.ShapeDtypeStruct((n_rows, d), F32),
        compiler_params=_params(1),
        name="dispatch",
    )(pad_start, pad_cnt, pos_flat, h2)


def _expert_kernel(tile_exp_ref, n_used_ref, xs_ref, wup_ref, bup_ref, wdn_ref, bdn_ref, ys_ref, *, d_ff):
    i = pl.program_id(0)

    @pl.when(i < n_used_ref[0])
    def _():
        h = _dot(xs_ref[...].astype(BF16), wup_ref[0]) + bup_ref[0]
        glu = jnp.minimum(h[:, :d_ff], SWIGLU_LIMIT)
        lin = jnp.clip(h[:, d_ff:], -SWIGLU_LIMIT, SWIGLU_LIMIT)
        a = glu * jax.nn.sigmoid(SWIGLU_ALPHA * glu) * (lin + 1.0)
        ys_ref[...] = _dot(a.astype(BF16), wdn_ref[0]) + bdn_ref[0]


def _experts(tile_exp, n_used, xs, wup, bup, wdn, bdn):
    n_rows, d = xs.shape
    d_ff = wdn.shape[1]
    tm = TM_EXPERT
    row = lambda i, te, nu: (jnp.minimum(i, nu[0] - 1), 0)
    exp3 = lambda i, te, nu: (te[i], 0, 0)
    grid_spec = pltpu.PrefetchScalarGridSpec(
        num_scalar_prefetch=2,
        grid=(n_rows // tm,),
        in_specs=[pl.BlockSpec((tm, d), row),
                  pl.BlockSpec((1, d, 2 * d_ff), exp3), pl.BlockSpec((1, 1, 2 * d_ff), exp3),
                  pl.BlockSpec((1, d_ff, d), exp3), pl.BlockSpec((1, 1, d), exp3)],
        out_specs=pl.BlockSpec((tm, d), row),
    )
    return pl.pallas_call(
        functools.partial(_expert_kernel, d_ff=d_ff),
        grid_spec=grid_spec,
        out_shape=jax.ShapeDtypeStruct((n_rows, d), F32),
        compiler_params=_params(1),
        name="experts",
    )(tile_exp, n_used, xs, wup, bup, wdn, bdn)


def _combine_kernel(pos_ref, x1_ref, tw_ref, ys_ref, o_ref, buf_ref, sem):
    tm = x1_ref.shape[0]

    def issue(r, c):
        for kk in range(TOP_K):
            _row_copy(ys_ref, pos_ref[r * TOP_K + kk], buf_ref.at[kk], r, sem).start()
        return c

    def drain(r, c):
        for kk in range(TOP_K):
            _row_copy(ys_ref, pos_ref[r * TOP_K + kk], buf_ref.at[kk], r, sem).wait()
        return c

    lax.fori_loop(0, tm, issue, 0)
    lax.fori_loop(0, tm, drain, 0)
    tw = tw_ref[...]
    out = x1_ref[...]
    for kk in range(TOP_K):
        out = out + tw[:, kk:kk + 1] * buf_ref[kk]
    o_ref[...] = out


def _combine(pos_flat, x1, tw, ys):
    t, d = x1.shape
    tm = min(TM_ROWS, t)
    return pl.pallas_call(
        _combine_kernel,
        grid=(t // tm,),
        in_specs=[pl.BlockSpec((tm * TOP_K,), lambda i: (i,), memory_space=pltpu.SMEM),
                  pl.BlockSpec((tm, d), lambda i: (i, 0)), pl.BlockSpec((tm, LANES), lambda i: (i, 0)),
                  pl.BlockSpec(memory_space=pl.ANY)],
        out_specs=pl.BlockSpec((tm, d), lambda i: (i, 0)),
        out_shape=jax.ShapeDtypeStruct((t, d), F32),
        scratch_shapes=[pltpu.VMEM((TOP_K, tm, d), F32), pltpu.SemaphoreType.DMA(())],
        compiler_params=_params(1),
        name="combine",
    )(pos_flat, x1, tw, ys)


def _pad_lanes(a, value=0.0):
    return jnp.pad(a, ((0, 0), (0, LANES - a.shape[-1])), constant_values=value)


def _layer(x, mem, norm1_g, w_in, b_forget, b_gate, fox_q_g, fox_k_g, conv_w, mem_norm_g, w_mem_kv,
           xa_q_g, xa_k_g, w_branch, w_out, norm2_g, w_router, b_router, w_up, b_up, w_down, b_down):
    batch, seq, d = x.shape
    n_mem = mem.shape[1]
    n_branch, width, _ = w_branch.shape
    n_fh = b_forget.shape[0]
    hd = fox_q_g.shape[0]
    xa_hd = xa_q_g.shape[0]
    xa_heads = width // xa_hd
    n_exp = w_router.shape[1]
    assert n_fh * hd == width and n_branch == 3 and conv_w.shape[0] == 3
    assert LANES % hd == 0 and width % MXU_DIM == 0 and n_exp <= LANES and n_fh <= LANES
    fh_pad = -(-n_fh // SUBLANES) * SUBLANES
    t = batch * seq
    x2 = x.reshape(t, d)
    row1 = lambda a: a.reshape(1, -1).astype(F32)

    o_f = 3 * width
    o_conv = o_f + n_fh
    wqkv = w_in[:, :o_f].astype(BF16)
    wf = _pad_lanes(w_in[:, o_f:o_conv]).astype(BF16)
    wcat = w_in[:, o_conv:].astype(BF16)

    mk, mv = _memkv(mem.reshape(batch * n_mem, d), row1(mem_norm_g), w_mem_kv.astype(BF16), row1(xa_k_g),
                    xa_heads, xa_hd)
    q, k, v, cum, cumt = _qkv(x2, row1(norm1_g), wqkv, wf, _pad_lanes(row1(b_forget)),
                              row1(jnp.tile(fox_q_g, n_fh)), row1(jnp.tile(fox_k_g, n_fh)),
                              seq, width, hd, fh_pad)
    y_fox = _fox(q, k, v, cum, cumt, batch, seq, width, hd, fh_pad)

    wr = _pad_lanes(w_router.astype(F32))
    wr_hi = wr.astype(BF16)
    wr_lo = (wr - wr_hi.astype(F32)).astype(BF16)
    x1, h2, idx, tw, rank, cnt = _merge(
        x2, y_fox, mk, mv, row1(norm1_g), wcat, b_gate.astype(F32), conv_w.astype(F32), row1(xa_q_g),
        w_branch.astype(BF16), w_out.astype(BF16), row1(norm2_g), wr_hi, wr_lo,
        _pad_lanes(row1(b_router), NEG_BIG), seq, n_mem, width, xa_heads, xa_hd)

    counts = cnt[0, :n_exp].astype(jnp.int32)
    tiles_per = (counts + TM_EXPERT - 1) // TM_EXPERT
    tile_end = jnp.cumsum(tiles_per)
    group_start = (tile_end - tiles_per) * TM_EXPERT
    n_tiles = (t * TOP_K) // TM_EXPERT + n_exp
    n_used = tile_end[-1:]
    tile_ids = jnp.minimum(jnp.arange(n_tiles, dtype=jnp.int32), n_used[0] - 1)
    tile_exp = jnp.minimum(jnp.searchsorted(tile_end, tile_ids, side="right"), n_exp - 1).astype(jnp.int32)
    pos = (group_start[idx[:, :TOP_K]] + rank[:, :TOP_K]).reshape(-1).astype(jnp.int32)
    pad_start = (group_start + counts).astype(jnp.int32)
    pad_cnt = (tiles_per * TM_EXPERT - counts).astype(jnp.int32)

    xs = _dispatch(pad_start, pad_cnt, pos, h2, n_tiles * TM_EXPERT, n_exp)
    ys = _experts(tile_exp, n_used.astype(jnp.int32), xs, w_up.astype(BF16), b_up[:, None, :].astype(F32),
                  w_down.astype(BF16), b_down[:, None, :].astype(F32))
    out = _combine(pos, x1, tw, ys)
    return out.reshape(batch, seq, d)


def kernel(x, mem, norm1_g, w_in, b_forget, b_gate, fox_q_g, fox_k_g, conv_w, mem_norm_g, w_mem_kv, xa_q_g,
           xa_k_g, w_branch, w_out, norm2_g, w_router, b_router, w_up, b_up, w_down, b_down):
    for l in range(norm1_g.shape[0]):
        x = _layer(x, mem, norm1_g[l], w_in[l], b_forget[l], b_gate[l], fox_q_g[l], fox_k_g[l], conv_w[l],
                   mem_norm_g[l], w_mem_kv[l], xa_q_g[l], xa_k_g[l], w_branch[l], w_out[l], norm2_g[l],
                   w_router[l], b_router[l], w_up[l], b_up[l], w_down[l], b_down[l])
    return x
```

```python
import functools

import jax
import jax.numpy as jnp
from jax import lax
from jax.experimental import pallas as pl
from jax.experimental.pallas import tpu as pltpu

F32 = jnp.float32
BF16 = jnp.bfloat16
I32 = jnp.int32

NORM_EPS = 1e-5
QK_EPS = 1e-6
TOP_K = 4
SWIGLU_ALPHA = 1.702
SWIGLU_LIMIT = 7.0
NEG_BIG = -1e30
LOG2_E = 1.4426950408889634

LANES = 128
SUBLANES = 8
MXU_DIM = 256
VMEM_LIMIT_BYTES = 56 * 1024 * 1024

TM_QKV = 512
TQ_FOX = 256
TK_FOX = 512
TM_WINDOW = 256
TM_EXPERT = 512
ZERO_ROWS = 256


def _params(n_axes):
    return pltpu.CompilerParams(dimension_semantics=("arbitrary",) * n_axes,
                                vmem_limit_bytes=VMEM_LIMIT_BYTES)


def _const_spec(shape):
    zeros = (0,) * len(shape)
    return pl.BlockSpec(shape, lambda *_: zeros)


def _rms(x, g, eps):
    return x * lax.rsqrt(jnp.mean(x * x, axis=-1, keepdims=True) + eps) * g


def _split_bf16(x):
    hi = x.astype(BF16)
    lo = (x - hi.astype(F32)).astype(BF16)
    return hi, lo


def _dot(a, b):
    return jnp.dot(a, b, preferred_element_type=F32)


def _dot_nt(a, b):
    return lax.dot_general(a, b, (((1,), (1,)), ((), ())), preferred_element_type=F32)


def _memkv_kernel(mem_ref, g_ref, w_ref, kg_ref, mk_ref, mv_ref, *, n_heads, hd):
    y = _rms(mem_ref[...], g_ref[...], NORM_EPS)
    kv = _dot(y.astype(BF16), w_ref[...])
    width = n_heads * hd
    for h in range(n_heads):
        sl = slice(h * hd, (h + 1) * hd)
        mk_ref[:, sl] = _rms(kv[:, sl], kg_ref[...], QK_EPS).astype(BF16)
    mv_ref[...] = kv[:, width:].astype(BF16)


def _memkv(mem2, g, w_kv, kg, n_heads, hd):
    rows, d = mem2.shape
    width = n_heads * hd
    tm = min(rows, 512)
    return pl.pallas_call(
        functools.partial(_memkv_kernel, n_heads=n_heads, hd=hd),
        grid=(rows // tm,),
        in_specs=[pl.BlockSpec((tm, d), lambda i: (i, 0)),
                  _const_spec((1, d)), _const_spec((d, 2 * width)), _const_spec((1, hd))],
        out_specs=[pl.BlockSpec((tm, width), lambda i: (i, 0)),
                   pl.BlockSpec((tm, width), lambda i: (i, 0))],
        out_shape=[jax.ShapeDtypeStruct((rows, width), BF16)] * 2,
        compiler_params=_params(1),
        name="memkv",
    )(mem2, g, w_kv, kg)


def _qkv_kernel(x_ref, g1_ref, wqkv_ref, wf_ref, bf_ref, gq_ref, gk_ref, bd_ref, tri_ref,
                q_ref, k_ref, v_ref, cumt_ref, carry_ref,
                *, tiles_per_batch, width, hd, fh_pad):
    i = pl.program_id(0)

    @pl.when(i % tiles_per_batch == 0)
    def _():
        carry_ref[...] = jnp.zeros_like(carry_ref)

    hb = _rms(x_ref[...], g1_ref[...], NORM_EPS).astype(BF16)
    qkv = _dot(hb, wqkv_ref[...])

    def head_norm(z, g):
        hi, lo = _split_bf16(z * z)
        parts = []
        for c in range(width // MXU_DIM):
            sl = slice(c * MXU_DIM, (c + 1) * MXU_DIM)
            parts.append(_dot(hi[:, sl], bd_ref[...]) + _dot(lo[:, sl], bd_ref[...]))
        ms = jnp.concatenate(parts, axis=-1) * (1.0 / hd)
        return z * lax.rsqrt(ms + QK_EPS) * g

    q = head_norm(qkv[:, :width], gq_ref[...]) * (hd ** -0.5 * LOG2_E)
    k = head_norm(qkv[:, width:2 * width], gk_ref[...])
    q_ref[...] = q.astype(BF16)
    k_ref[...] = k.astype(BF16)
    v_ref[...] = qkv[:, 2 * width:].astype(BF16)

    z = _dot(hb, wf_ref[...]) + bf_ref[...]
    log_f = jnp.minimum(z, 0.0) - jnp.log(1.0 + jnp.exp(-jnp.abs(z)))
    hi, lo = _split_bf16(log_f)
    cum = _dot(tri_ref[...], hi) + _dot(tri_ref[...], lo) + carry_ref[0:1, :]
    tm = cum.shape[0]
    carry_ref[...] = jnp.broadcast_to(cum[tm - 1:tm, :], carry_ref.shape)
    cumt_ref[...] = (cum * LOG2_E).T[:fh_pad, :]


def _qkv(x2, g1, wqkv, wf, bf, gq_t, gk_t, seq, width, hd, fh_pad):
    t, d = x2.shape
    tm = min(TM_QKV, seq)
    bd = (jnp.arange(MXU_DIM)[:, None] // hd == jnp.arange(MXU_DIM)[None, :] // hd).astype(BF16)
    tri = (jnp.arange(tm)[:, None] >= jnp.arange(tm)[None, :]).astype(BF16)
    row = lambda i: (i, 0)
    return pl.pallas_call(
        functools.partial(_qkv_kernel, tiles_per_batch=seq // tm, width=width, hd=hd, fh_pad=fh_pad),
        grid=(t // tm,),
        in_specs=[pl.BlockSpec((tm, d), row), _const_spec((1, d)), _const_spec((d, 3 * width)),
                  _const_spec((d, LANES)), _const_spec((1, LANES)), _const_spec((1, width)),
                  _const_spec((1, width)), _const_spec((MXU_DIM, MXU_DIM)), _const_spec((tm, tm))],
        out_specs=[pl.BlockSpec((tm, width), row)] * 3 + [pl.BlockSpec((fh_pad, tm), lambda i: (0, i))],
        out_shape=[jax.ShapeDtypeStruct((t, width), BF16)] * 3 + [jax.ShapeDtypeStruct((fh_pad, t), F32)],
        scratch_shapes=[pltpu.VMEM((SUBLANES, LANES), F32)],
        compiler_params=_params(1),
        name="qkv",
    )(x2, g1, wqkv, wf, bf, gq_t, gk_t, bd, tri)


def _fox_kernel(q_ref, k_ref, v_ref, cumt_ref, o_ref, *, tq, tk, hd):
    hp = pl.program_id(1)
    q0 = pl.program_id(2) * tq
    nh = LANES // hd
    lane = lax.broadcasted_iota(I32, (1, LANES), 1)
    rows = lax.broadcasted_iota(I32, (tq, tk), 0) + q0
    cols = lax.broadcasted_iota(I32, (tq, tk), 1)
    q = q_ref[...]
    heads = []
    for hh in range(nh):
        in_head = (lane >= hh * hd) & (lane < (hh + 1) * hd)
        heads.append((hp * nh + hh, in_head, jnp.where(in_head, q, jnp.zeros_like(q))))

    def scores(j):
        start = pl.multiple_of(j * tk, tk)
        kj = k_ref[pl.ds(start, tk), :]
        return tuple(_dot_nt(qh, kj) - cumt_ref[pl.ds(h, 1), pl.ds(start, tk)] for h, _, qh in heads)

    def consume(j, ss, carry, masked):
        start = pl.multiple_of(j * tk, tk)
        vj = v_ref[pl.ds(start, tk), :]
        out = []
        for (_, in_head, _), s, (m, acc) in zip(heads, ss, carry):
            vh = jnp.where(in_head, vj, jnp.ones_like(vj))
            if masked:
                s = jnp.where(cols + start <= rows, s, NEG_BIG)
            m_new = jnp.maximum(m, jnp.max(s, axis=-1, keepdims=True))
            p = jnp.exp2(s - m_new)
            out.append((m_new, jnp.exp2(m - m_new) * acc + _dot(p.astype(BF16), vh)))
        return tuple(out)

    def body(j, c):
        ss, carry = c
        return scores(j + 1), consume(j, ss, carry, False)

    carry = tuple((jnp.full((tq, 1), NEG_BIG, F32), jnp.zeros((tq, LANES), F32)) for _ in heads)
    n_full = q0 // tk
    ss, carry = lax.fori_loop(0, n_full, body, (scores(0), carry))
    carry = consume(n_full, ss, carry, True)
    o = None
    for hh in range(nh - 1, -1, -1):
        acc = carry[hh][1]
        other = ((hh + 1) % nh) * hd
        val = acc / acc[:, other:other + 1]
        o = val if o is None else jnp.where(lane < (hh + 1) * hd, val, o)
    o_ref[...] = o.astype(BF16)


def _fox(q, k, v, cumt, batch, seq, width, hd, fh_pad):
    t = q.shape[0]
    tq = min(TQ_FOX, seq)
    tk = min(TK_FOX, seq)
    assert tk % tq == 0 and seq % tk == 0
    nq = seq // tq
    qmap = lambda b, hp, i: (b * nq + i, hp)
    kvmap = lambda b, hp, i: (b, hp)
    return pl.pallas_call(
        functools.partial(_fox_kernel, tq=tq, tk=tk, hd=hd),
        grid=(batch, width // LANES, nq),
        in_specs=[pl.BlockSpec((tq, LANES), qmap), pl.BlockSpec((seq, LANES), kvmap),
                  pl.BlockSpec((seq, LANES), kvmap),
                  pl.BlockSpec((fh_pad, seq), lambda b, hp, i: (0, b))],
        out_specs=pl.BlockSpec((tq, LANES), qmap),
        out_shape=jax.ShapeDtypeStruct((t, width), BF16),
        compiler_params=_params(3),
        name="fox",
    )(q, k, v, cumt)


def _merge_kernel(x_ref, yf_ref, mk_ref, mv_ref, g1_ref, wcat_ref, bg_ref, cw_ref, xqg_ref,
                  wbr_ref, wout_ref, g2_ref, wrh_ref, wrl_ref, br_ref, ltri_ref, utri_ref,
                  x1_ref, h2_ref, lpos_ref, lpost_ref, tw_ref, tcnt_ref, tbase_ref,
                  prev_ref, carry_ref, *, tiles_per_batch, width, xa_heads, xa_hd):
    i = pl.program_id(0)
    tm, d = x_ref.shape

    @pl.when(i % tiles_per_batch == 0)
    def _():
        prev_ref[...] = jnp.zeros_like(prev_ref)

    @pl.when(i == 0)
    def _():
        carry_ref[...] = jnp.zeros_like(carry_ref)

    x = x_ref[...]
    hb = _rms(x, g1_ref[...], NORM_EPS).astype(BF16)
    pc = _dot(hb, wcat_ref[...])

    uc = pc[:, :width] * pc[:, width:2 * width]
    row = lax.broadcasted_iota(I32, (tm, 1), 0)
    p1 = prev_ref[SUBLANES - 1:SUBLANES, :]
    p2 = prev_ref[SUBLANES - 2:SUBLANES - 1, :]
    m1 = jnp.where(row == 0, p1, pltpu.roll(uc, 1, 0))
    m2 = jnp.where(row == 0, p2, jnp.where(row == 1, p1, pltpu.roll(uc, 2, 0)))
    y_conv = pc[:, 2 * width:3 * width] * (cw_ref[0:1, :] * m2 + cw_ref[1:2, :] * m1 + cw_ref[2:3, :] * uc)
    prev_ref[...] = uc[tm - SUBLANES:tm, :]

    ys = []
    for h in range(xa_heads):
        sl = slice(3 * width + h * xa_hd, 3 * width + (h + 1) * xa_hd)
        ml = slice(h * xa_hd, (h + 1) * xa_hd)
        qh = _rms(pc[:, sl], xqg_ref[...], QK_EPS) * (xa_hd ** -0.5)
        s = _dot_nt(qh.astype(BF16), mk_ref[:, ml])
        p = jnp.exp(s - jnp.max(s, axis=-1, keepdims=True))
        ys.append(_dot(p.astype(BF16), mv_ref[:, ml]) / jnp.sum(p, axis=-1, keepdims=True))
    y_mem = jnp.concatenate(ys, axis=-1)

    merged = jnp.zeros((tm, d), F32)
    for n, yb in enumerate((yf_ref[...], y_conv.astype(BF16), y_mem.astype(BF16))):
        gate = jax.nn.sigmoid(pc[:, 4 * width + n * d:4 * width + (n + 1) * d] + bg_ref[n:n + 1, :])
        merged = merged + gate * _dot(yb, wbr_ref[n])
    x1 = x + _dot(merged.astype(BF16), wout_ref[...])
    x1_ref[...] = x1
    h2 = _rms(x1, g2_ref[...], NORM_EPS)
    hi, lo = _split_bf16(h2)
    h2_ref[...] = hi

    logits = _dot(hi, wrh_ref[...]) + _dot(hi, wrl_ref[...]) + _dot(lo, wrh_ref[...]) + br_ref[...]
    lane = lax.broadcasted_iota(I32, (tm, LANES), 1)
    vals, idxs = [], []
    onehot = jnp.zeros((tm, LANES), F32)
    for _ in range(TOP_K):
        m = jnp.max(logits, axis=-1, keepdims=True)
        idx = jnp.min(jnp.where(logits == m, lane, LANES), axis=-1, keepdims=True)
        sel = lane == idx
        logits = jnp.where(sel, -jnp.inf, logits)
        onehot = onehot + sel.astype(F32)
        vals.append(m)
        idxs.append(idx)
    es = [jnp.exp(v - vals[0]) for v in vals]
    den = es[0]
    for e in es[1:]:
        den = den + e

    cnt = jnp.sum(onehot, axis=0, keepdims=True)
    span = jnp.floor((cnt + (SUBLANES - 1)) * (1.0 / SUBLANES)) * SUBLANES
    s_hi, s_lo = _split_bf16(jnp.broadcast_to(span, (SUBLANES, LANES)))
    seg_start = (_dot(s_hi, utri_ref[...]) + _dot(s_lo, utri_ref[...]))[0:1, :]
    where_to = _dot(ltri_ref[...], onehot.astype(BF16)) + seg_start
    tcnt_ref[0] = jnp.broadcast_to(span, (SUBLANES, LANES))
    tbase_ref[0] = carry_ref[...]
    carry_ref[...] = carry_ref[...] + span

    lpos_o = jnp.zeros((tm, LANES), F32)
    tw_o = jnp.zeros((tm, LANES), F32)
    for kk in range(TOP_K):
        lpos = jnp.sum(jnp.where(lane == idxs[kk], where_to, 0.0), axis=-1, keepdims=True)
        lpos_o = jnp.where(lane == kk, lpos, lpos_o)
        tw_o = jnp.where(lane == kk, es[kk] / den, tw_o)
    lpos_ref[...] = lpos_o.astype(I32)
    lpost_ref[0] = lpos_o.T[:SUBLANES, :].astype(I32)
    tw_ref[...] = tw_o


def _merge(x2, y_fox, mk, mv, g1, wcat, bg, cw, xqg, wbr, wout, g2, wrh, wrl, br,
           seq, n_mem, width, xa_heads, xa_hd):
    t, d = x2.shape
    tm = min(TM_WINDOW, seq)
    nt = t // tm
    tpb = seq // tm
    ltri = (jnp.arange(tm)[:, None] > jnp.arange(tm)[None, :]).astype(BF16)
    utri = (jnp.arange(LANES)[:, None] < jnp.arange(LANES)[None, :]).astype(BF16)
    row = lambda i: (i, 0)
    tile3 = lambda i: (i, 0, 0)
    mem_map = lambda i: (i // tpb, 0)
    return pl.pallas_call(
        functools.partial(_merge_kernel, tiles_per_batch=tpb, width=width, xa_heads=xa_heads, xa_hd=xa_hd),
        grid=(nt,),
        in_specs=[pl.BlockSpec((tm, d), row), pl.BlockSpec((tm, width), row),
                  pl.BlockSpec((n_mem, width), mem_map), pl.BlockSpec((n_mem, width), mem_map),
                  _const_spec((1, d)), _const_spec(wcat.shape), _const_spec(bg.shape), _const_spec(cw.shape),
                  _const_spec((1, xa_hd)), _const_spec(wbr.shape), _const_spec((d, d)), _const_spec((1, d)),
                  _const_spec((d, LANES)), _const_spec((d, LANES)), _const_spec((1, LANES)),
                  _const_spec((tm, tm)), _const_spec((LANES, LANES))],
        out_specs=[pl.BlockSpec((tm, d), row), pl.BlockSpec((tm, d), row),
                   pl.BlockSpec((tm, LANES), row), pl.BlockSpec((1, SUBLANES, tm), tile3),
                   pl.BlockSpec((tm, LANES), row),
                   pl.BlockSpec((1, SUBLANES, LANES), tile3), pl.BlockSpec((1, SUBLANES, LANES), tile3)],
        out_shape=[jax.ShapeDtypeStruct((t, d), F32), jax.ShapeDtypeStruct((t, d), BF16),
                   jax.ShapeDtypeStruct((t, LANES), I32), jax.ShapeDtypeStruct((nt, SUBLANES, tm), I32),
                   jax.ShapeDtypeStruct((t, LANES), F32),
                   jax.ShapeDtypeStruct((nt, SUBLANES, LANES), F32),
                   jax.ShapeDtypeStruct((nt, SUBLANES, LANES), F32)],
        scratch_shapes=[pltpu.VMEM((SUBLANES, width), F32), pltpu.VMEM((SUBLANES, LANES), F32)],
        compiler_params=_params(1),
        name="merge",
    )(x2, y_fox, mk, mv, g1, wcat, bg, cw, xqg, wbr, wout, g2, wrh, wrl, br, ltri, utri)


def _block_rows(tm, n_exp):
    return -(-(TOP_K * tm + (SUBLANES - 1) * n_exp) // MXU_DIM) * MXU_DIM


def _chunked_copies(src_ref, src0, dst_ref, dst0, n, sem, max_rows, wait, fixed_src=False):
    for b in range(max_rows.bit_length() - 1, SUBLANES.bit_length() - 2, -1):
        size = 1 << b
        off = (n >> (b + 1)) << (b + 1)

        @pl.when((n >> b) & 1 == 1)
        def _():
            src = 0 if fixed_src else pl.multiple_of(src0 + off, SUBLANES)
            cp = pltpu.make_async_copy(src_ref.at[pl.ds(src, size)],
                                       dst_ref.at[pl.ds(pl.multiple_of(dst0 + off, SUBLANES), size)], sem)
            if wait:
                cp.wait()
            else:
                cp.start()


def _dispatch_kernel(seg_len_ref, seg_src_ref, seg_dst_ref, pad_start_ref, pad_cnt_ref, n_used_ref,
                     lpost_ref, h2_ref, xs_ref, buf_ref, zero_ref, sems, zsem, *, n_exp, n_tiles):
    w = pl.program_id(0)
    nw = pl.num_programs(0)
    tm = h2_ref.shape[0]
    rows = buf_ref.shape[1]
    slot = w % 2

    def window_copies(win, sl, wait):
        def per_expert(e, c):
            s = win * n_exp + e
            _chunked_copies(buf_ref.at[sl], seg_src_ref[s], xs_ref, seg_dst_ref[s], seg_len_ref[s],
                            sems.at[sl], tm, wait)
            return c
        lax.fori_loop(0, n_exp, per_expert, 0)

    def clear(wait):
        def per_expert(e, c):
            _chunked_copies(zero_ref, 0, xs_ref, pad_start_ref[e], pad_cnt_ref[e], zsem, ZERO_ROWS, wait,
                            fixed_src=True)
            return c

        def per_tile(i, c):
            for part in range(TM_EXPERT // ZERO_ROWS):
                cp = pltpu.make_async_copy(
                    zero_ref, xs_ref.at[pl.ds(i * TM_EXPERT + part * ZERO_ROWS, ZERO_ROWS)], zsem)
                if wait:
                    cp.wait()
                else:
                    cp.start()
            return c

        lax.fori_loop(0, n_exp, per_expert, 0)
        lax.fori_loop(n_used_ref[0], n_tiles, per_tile, 0)

    @pl.when(w == 0)
    def _():
        zero_ref[...] = jnp.zeros_like(zero_ref)
        clear(False)

    r_iota = lax.broadcasted_iota(I32, (rows, tm), 0)
    sel = r_iota == lpost_ref[0, 0:1, :]
    for kk in range(1, TOP_K):
        sel = sel | (r_iota == lpost_ref[0, kk:kk + 1, :])
    perm = jnp.where(sel, 1.0, 0.0).astype(BF16)

    @pl.when(w >= 2)
    def _():
        window_copies(w - 2, slot, True)

    buf_ref[slot] = _dot(perm, h2_ref[...])
    window_copies(w, slot, False)

    @pl.when(w == nw - 1)
    def _():
        @pl.when(nw >= 2)
        def _():
            window_copies(w - 1, 1 - slot, True)
        window_copies(w, slot, True)
        clear(True)


def _dispatch(tables, lpost, h2, n_tiles, n_exp):
    t, d = h2.shape
    tm = lpost.shape[2]
    grid_spec = pltpu.PrefetchScalarGridSpec(
        num_scalar_prefetch=len(tables),
        grid=(t // tm,),
        in_specs=[pl.BlockSpec((1, SUBLANES, tm), lambda i, *_: (i, 0, 0)),
                  pl.BlockSpec((tm, d), lambda i, *_: (i, 0))],
        out_specs=pl.BlockSpec(memory_space=pl.ANY),
        scratch_shapes=[pltpu.VMEM((2, _block_rows(tm, n_exp), d), F32), pltpu.VMEM((ZERO_ROWS, d), F32),
                        pltpu.SemaphoreType.DMA((2,)), pltpu.SemaphoreType.DMA(())],
    )
    return pl.pallas_call(
        functools.partial(_dispatch_kernel, n_exp=n_exp, n_tiles=n_tiles),
        grid_spec=grid_spec,
        out_shape=jax.ShapeDtypeStruct((n_tiles * TM_EXPERT, d), F32),
        compiler_params=_params(1),
        name="dispatch",
    )(*tables, lpost, h2)


def _expert_kernel(tile_exp_ref, n_used_ref, xs_ref, wup_ref, bup_ref, wdn_ref, bdn_ref, ys_ref, *, d_ff):
    i = pl.program_id(0)

    @pl.when(i < n_used_ref[0])
    def _():
        h = _dot(xs_ref[...].astype(BF16), wup_ref[0]) + bup_ref[0]
        glu = jnp.minimum(h[:, :d_ff], SWIGLU_LIMIT)
        lin = jnp.clip(h[:, d_ff:], -SWIGLU_LIMIT, SWIGLU_LIMIT)
        a = glu * jax.nn.sigmoid(SWIGLU_ALPHA * glu) * (lin + 1.0)
        ys_ref[...] = _dot(a.astype(BF16), wdn_ref[0]) + bdn_ref[0]

    @pl.when(i >= n_used_ref[0])
    def _():
        ys_ref[...] = jnp.zeros_like(ys_ref)


def _experts(tile_exp, n_used, xs, wup, bup, wdn, bdn):
    n_rows, d = xs.shape
    d_ff = wdn.shape[1]
    tm = TM_EXPERT
    row = lambda i, te, nu: (i, 0)
    exp3 = lambda i, te, nu: (te[i], 0, 0)
    grid_spec = pltpu.PrefetchScalarGridSpec(
        num_scalar_prefetch=2,
        grid=(n_rows // tm,),
        in_specs=[pl.BlockSpec((tm, d), row),
                  pl.BlockSpec((1, d, 2 * d_ff), exp3), pl.BlockSpec((1, 1, 2 * d_ff), exp3),
                  pl.BlockSpec((1, d_ff, d), exp3), pl.BlockSpec((1, 1, d), exp3)],
        out_specs=pl.BlockSpec((tm, d), row),
    )
    return pl.pallas_call(
        functools.partial(_expert_kernel, d_ff=d_ff),
        grid_spec=grid_spec,
        out_shape=jax.ShapeDtypeStruct((n_rows, d), F32),
        compiler_params=_params(1),
        name="experts",
    )(tile_exp, n_used, xs, wup, bup, wdn, bdn)


def _combine_kernel(seg_len_ref, seg_src_ref, seg_dst_ref, lpos_ref, tw_ref, x1_ref, ys_ref, o_ref,
                    buf_ref, sems, *, n_exp):
    w = pl.program_id(0)
    nw = pl.num_programs(0)
    tm = x1_ref.shape[0]
    rows = buf_ref.shape[1]
    slot = w % 2

    def window_copies(win, sl, wait):
        def per_expert(e, c):
            s = win * n_exp + e
            _chunked_copies(ys_ref, seg_dst_ref[s], buf_ref.at[sl], seg_src_ref[s], seg_len_ref[s],
                            sems.at[sl], tm, wait)
            return c
        lax.fori_loop(0, n_exp, per_expert, 0)

    @pl.when(w == 0)
    def _():
        buf_ref[...] = jnp.zeros_like(buf_ref)
        window_copies(0, 0, False)

    @pl.when(w + 1 < nw)
    def _():
        window_copies(w + 1, 1 - slot, False)

    window_copies(w, slot, True)

    c_iota = lax.broadcasted_iota(I32, (tm, rows), 1)
    lpos = lpos_ref[...]
    tw = tw_ref[...]
    wperm = jnp.zeros((tm, rows), F32)
    for kk in range(TOP_K):
        wperm = jnp.where(c_iota == lpos[:, kk:kk + 1], tw[:, kk:kk + 1], wperm)
    hi, lo = _split_bf16(wperm)
    y = buf_ref[slot].astype(BF16)
    o_ref[...] = x1_ref[...] + _dot(hi, y) + _dot(lo, y)


def _combine(tables, lpos, tw, x1, ys, n_exp):
    t, d = x1.shape
    tm = min(TM_WINDOW, t)
    row = lambda i, *_: (i, 0)
    grid_spec = pltpu.PrefetchScalarGridSpec(
        num_scalar_prefetch=len(tables),
        grid=(t // tm,),
        in_specs=[pl.BlockSpec((tm, LANES), row), pl.BlockSpec((tm, LANES), row), pl.BlockSpec((tm, d), row),
                  pl.BlockSpec(memory_space=pl.ANY)],
        out_specs=pl.BlockSpec((tm, d), row),
        scratch_shapes=[pltpu.VMEM((2, _block_rows(tm, n_exp), d), F32), pltpu.SemaphoreType.DMA((2,))],
    )
    return pl.pallas_call(
        functools.partial(_combine_kernel, n_exp=n_exp),
        grid_spec=grid_spec,
        out_shape=jax.ShapeDtypeStruct((t, d), F32),
        compiler_params=_params(1),
        name="combine",
    )(*tables, lpos, tw, x1, ys)


def _pad_lanes(a, value=0.0):
    return jnp.pad(a, ((0, 0), (0, LANES - a.shape[-1])), constant_values=value)


def _layer(x, mem, norm1_g, w_in, b_forget, b_gate, fox_q_g, fox_k_g, conv_w, mem_norm_g, w_mem_kv,
           xa_q_g, xa_k_g, w_branch, w_out, norm2_g, w_router, b_router, w_up, b_up, w_down, b_down):
    batch, seq, d = x.shape
    n_mem = mem.shape[1]
    n_branch, width, _ = w_branch.shape
    n_fh = b_forget.shape[0]
    hd = fox_q_g.shape[0]
    xa_hd = xa_q_g.shape[0]
    xa_heads = width // xa_hd
    n_exp = w_router.shape[1]
    assert n_fh * hd == width and n_branch == 3 and conv_w.shape[0] == 3
    assert LANES % hd == 0 and width % MXU_DIM == 0 and n_exp <= LANES and n_fh <= LANES
    fh_pad = -(-n_fh // SUBLANES) * SUBLANES
    t = batch * seq
    x2 = x.reshape(t, d)
    row1 = lambda a: a.reshape(1, -1).astype(F32)

    o_f = 3 * width
    o_conv = o_f + n_fh
    wqkv = w_in[:, :o_f].astype(BF16)
    wf = _pad_lanes(w_in[:, o_f:o_conv]).astype(BF16)
    wcat = w_in[:, o_conv:].astype(BF16)

    mk, mv = _memkv(mem.reshape(batch * n_mem, d), row1(mem_norm_g), w_mem_kv.astype(BF16), row1(xa_k_g),
                    xa_heads, xa_hd)
    q, k, v, cumt = _qkv(x2, row1(norm1_g), wqkv, wf, _pad_lanes(row1(b_forget)),
                         row1(jnp.tile(fox_q_g, n_fh)), row1(jnp.tile(fox_k_g, n_fh)),
                         seq, width, hd, fh_pad)
    y_fox = _fox(q, k, v, cumt, batch, seq, width, hd, fh_pad)

    wr = _pad_lanes(w_router.astype(F32))
    wr_hi = wr.astype(BF16)
    wr_lo = (wr - wr_hi.astype(F32)).astype(BF16)
    x1, h2, lpos, lpost, tw, tcnt, tbase = _merge(
        x2, y_fox, mk, mv, row1(norm1_g), wcat, b_gate.astype(F32), conv_w.astype(F32), row1(xa_q_g),
        w_branch.astype(BF16), w_out.astype(BF16), row1(norm2_g), wr_hi, wr_lo,
        _pad_lanes(row1(b_router), NEG_BIG), seq, n_mem, width, xa_heads, xa_hd)

    seg_len = tcnt[:, 0, :n_exp].astype(I32)
    seg_base = tbase[:, 0, :n_exp].astype(I32)
    counts = seg_base[-1] + seg_len[-1]
    tiles_per = (counts + TM_EXPERT - 1) // TM_EXPERT
    tile_end = jnp.cumsum(tiles_per)
    group_start = (tile_end - tiles_per) * TM_EXPERT
    seg_dst = group_start[None, :] + seg_base
    seg_src = jnp.cumsum(seg_len, axis=1) - seg_len
    n_tiles = -(-(t * TOP_K + (SUBLANES - 1) * seg_len.size) // TM_EXPERT) + n_exp
    n_used = tile_end[-1:].astype(I32)
    tile_ids = jnp.arange(n_tiles, dtype=I32)
    tile_exp = jnp.minimum(jnp.sum(tile_ids[:, None] >= tile_end[None, :], axis=1), n_exp - 1).astype(I32)
    tile_exp = jnp.where(tile_ids < n_used[0], tile_exp, tile_exp[jnp.maximum(n_used[0] - 1, 0)])
    pad_start = (group_start + counts).astype(I32)
    pad_cnt = (tiles_per * TM_EXPERT - counts).astype(I32)
    seg_tables = (seg_len.reshape(-1), seg_src.reshape(-1).astype(I32), seg_dst.reshape(-1).astype(I32))

    xs = _dispatch(seg_tables + (pad_start, pad_cnt, n_used), lpost, h2, n_tiles, n_exp)
    ys = _experts(tile_exp, n_used, xs, w_up.astype(BF16), b_up[:, None, :].astype(F32),
                  w_down.astype(BF16), b_down[:, None, :].astype(F32))
    out = _combine(seg_tables, lpos, tw, x1, ys, n_exp)
    return out.reshape(batch, seq, d)


def kernel(x, mem, norm1_g, w_in, b_forget, b_gate, fox_q_g, fox_k_g, conv_w, mem_norm_g, w_mem_kv, xa_q_g,
           xa_k_g, w_branch, w_out, norm2_g, w_router, b_router, w_up, b_up, w_down, b_down):
    for l in range(norm1_g.shape[0]):
        x = _layer(x, mem, norm1_g[l], w_in[l], b_forget[l], b_gate[l], fox_q_g[l], fox_k_g[l], conv_w[l],
                   mem_norm_g[l], w_mem_kv[l], xa_q_g[l], xa_k_g[l], w_branch[l], w_out[l], norm2_g[l],
                   w_router[l], b_router[l], w_up[l], b_up[l], w_down[l], b_down[l])
    return x
```

```python
import functools

import jax
import jax.numpy as jnp
from jax import lax
from jax.experimental import pallas as pl
from jax.experimental.pallas import tpu as pltpu

F32 = jnp.float32
BF16 = jnp.bfloat16
I32 = jnp.int32

NORM_EPS = 1e-5
QK_EPS = 1e-6
TOP_K = 4
SWIGLU_ALPHA = 1.702
SWIGLU_LIMIT = 7.0
NEG_BIG = -1e30
LOG2_E = 1.4426950408889634

LANES = 128
SUBLANES = 8
MXU_DIM = 256
VMEM_LIMIT_BYTES = 56 * 1024 * 1024

TM_QKV = 512
TQ_FOX = 512
TK_FOX = 512
TM_WINDOW = 256
TM_EXPERT = 512
ZERO_ROWS = 256


def _params(n_axes):
    return pltpu.CompilerParams(dimension_semantics=("arbitrary",) * n_axes,
                                vmem_limit_bytes=VMEM_LIMIT_BYTES)


def _const_spec(shape):
    zeros = (0,) * len(shape)
    return pl.BlockSpec(shape, lambda *_: zeros)


def _rms(x, g, eps):
    return x * lax.rsqrt(jnp.mean(x * x, axis=-1, keepdims=True) + eps) * g


def _split_bf16(x):
    hi = x.astype(BF16)
    lo = (x - hi.astype(F32)).astype(BF16)
    return hi, lo


def _dot(a, b):
    return jnp.dot(a, b, preferred_element_type=F32)


def _dot_nt(a, b):
    return lax.dot_general(a, b, (((1,), (1,)), ((), ())), preferred_element_type=F32)


def _memkv_kernel(mem_ref, g_ref, w_ref, kg_ref, mk_ref, mv_ref, *, n_heads, hd):
    y = _rms(mem_ref[...], g_ref[...], NORM_EPS)
    kv = _dot(y.astype(BF16), w_ref[...])
    width = n_heads * hd
    for h in range(n_heads):
        sl = slice(h * hd, (h + 1) * hd)
        mk_ref[:, sl] = _rms(kv[:, sl], kg_ref[...], QK_EPS).astype(BF16)
    mv_ref[...] = kv[:, width:].astype(BF16)


def _memkv(mem2, g, w_kv, kg, n_heads, hd):
    rows, d = mem2.shape
    width = n_heads * hd
    tm = min(rows, 512)
    return pl.pallas_call(
        functools.partial(_memkv_kernel, n_heads=n_heads, hd=hd),
        grid=(rows // tm,),
        in_specs=[pl.BlockSpec((tm, d), lambda i: (i, 0)),
                  _const_spec((1, d)), _const_spec((d, 2 * width)), _const_spec((1, hd))],
        out_specs=[pl.BlockSpec((tm, width), lambda i: (i, 0)),
                   pl.BlockSpec((tm, width), lambda i: (i, 0))],
        out_shape=[jax.ShapeDtypeStruct((rows, width), BF16)] * 2,
        compiler_params=_params(1),
        name="memkv",
    )(mem2, g, w_kv, kg)


def _qkv_kernel(x_ref, g1_ref, wqkv_ref, wf_ref, bf_ref, gq_ref, gk_ref, bd_ref, tri_ref,
                q_ref, k_ref, v_ref, cumt_ref, carry_ref,
                *, tiles_per_batch, width, hd, fh_pad):
    i = pl.program_id(0)

    @pl.when(i % tiles_per_batch == 0)
    def _():
        carry_ref[...] = jnp.zeros_like(carry_ref)

    hb = _rms(x_ref[...], g1_ref[...], NORM_EPS).astype(BF16)
    qkv = _dot(hb, wqkv_ref[...])

    def head_norm(z, g):
        hi, lo = _split_bf16(z * z)
        parts = []
        for c in range(width // MXU_DIM):
            sl = slice(c * MXU_DIM, (c + 1) * MXU_DIM)
            parts.append(_dot(hi[:, sl], bd_ref[...]) + _dot(lo[:, sl], bd_ref[...]))
        ms = jnp.concatenate(parts, axis=-1) * (1.0 / hd)
        return z * lax.rsqrt(ms + QK_EPS) * g

    q = head_norm(qkv[:, :width], gq_ref[...]) * (hd ** -0.5 * LOG2_E)
    k = head_norm(qkv[:, width:2 * width], gk_ref[...])
    q_ref[...] = q.astype(BF16)
    k_ref[...] = k.astype(BF16)
    v_ref[...] = qkv[:, 2 * width:].astype(BF16)

    z = _dot(hb, wf_ref[...]) + bf_ref[...]
    log_f = jnp.minimum(z, 0.0) - jnp.log(1.0 + jnp.exp(-jnp.abs(z)))
    hi, lo = _split_bf16(log_f)
    cum = _dot(tri_ref[...], hi) + _dot(tri_ref[...], lo) + carry_ref[0:1, :]
    tm = cum.shape[0]
    carry_ref[...] = jnp.broadcast_to(cum[tm - 1:tm, :], carry_ref.shape)
    cumt_ref[...] = (cum * LOG2_E).T[:fh_pad, :]


def _qkv(x2, g1, wqkv, wf, bf, gq_t, gk_t, seq, width, hd, fh_pad):
    t, d = x2.shape
    tm = min(TM_QKV, seq)
    bd = (jnp.arange(MXU_DIM)[:, None] // hd == jnp.arange(MXU_DIM)[None, :] // hd).astype(BF16)
    tri = (jnp.arange(tm)[:, None] >= jnp.arange(tm)[None, :]).astype(BF16)
    row = lambda i: (i, 0)
    return pl.pallas_call(
        functools.partial(_qkv_kernel, tiles_per_batch=seq // tm, width=width, hd=hd, fh_pad=fh_pad),
        grid=(t // tm,),
        in_specs=[pl.BlockSpec((tm, d), row), _const_spec((1, d)), _const_spec((d, 3 * width)),
                  _const_spec((d, LANES)), _const_spec((1, LANES)), _const_spec((1, width)),
                  _const_spec((1, width)), _const_spec((MXU_DIM, MXU_DIM)), _const_spec((tm, tm))],
        out_specs=[pl.BlockSpec((tm, width), row)] * 3 + [pl.BlockSpec((fh_pad, tm), lambda i: (0, i))],
        out_shape=[jax.ShapeDtypeStruct((t, width), BF16)] * 3 + [jax.ShapeDtypeStruct((fh_pad, t), F32)],
        scratch_shapes=[pltpu.VMEM((SUBLANES, LANES), F32)],
        compiler_params=_params(1),
        name="qkv",
    )(x2, g1, wqkv, wf, bf, gq_t, gk_t, bd, tri)


def _fox_kernel(q_ref, k_ref, v_ref, cumt_ref, o_ref, s_ref, *, tq, tk, hd):
    hp = pl.program_id(1)
    q0 = pl.program_id(2) * tq
    nh = LANES // hd
    lane = lax.broadcasted_iota(I32, (1, LANES), 1)
    rows = lax.broadcasted_iota(I32, (tq, tk), 0) + q0
    cols = lax.broadcasted_iota(I32, (tq, tk), 1)
    q = q_ref[...]
    heads = []
    for hh in range(nh):
        in_head = (lane >= hh * hd) & (lane < (hh + 1) * hd)
        heads.append((hp * nh + hh, in_head, jnp.where(in_head, q, jnp.zeros_like(q))))

    def scores(j, slot):
        start = pl.multiple_of(j * tk, tk)
        kj = k_ref[pl.ds(start, tk), :]
        for hh, (h, _, qh) in enumerate(heads):
            s_ref[slot, hh] = _dot_nt(qh, kj) - cumt_ref[pl.ds(h, 1), pl.ds(start, tk)]

    def consume(j, slot, carry, masked):
        start = pl.multiple_of(j * tk, tk)
        vj = v_ref[pl.ds(start, tk), :]
        out = []
        for hh, ((_, in_head, _), (m, acc)) in enumerate(zip(heads, carry)):
            vh = jnp.where(in_head, vj, jnp.ones_like(vj))
            s = s_ref[slot, hh]
            if masked:
                s = jnp.where(cols + start <= rows, s, NEG_BIG)
            m_new = jnp.maximum(m, jnp.max(s, axis=-1, keepdims=True))
            p = jnp.exp2(s - m_new)
            out.append((m_new, jnp.exp2(m - m_new) * acc + _dot(p.astype(BF16), vh)))
        return tuple(out)

    def pair(i, carry):
        j = 2 * i
        scores(j + 1, 1)
        carry = consume(j, 0, carry, False)
        scores(j + 2, 0)
        return consume(j + 1, 1, carry, False)

    def tail_odd(carry):
        scores(n_full, 1)
        carry = consume(n_full - 1, 0, carry, False)
        return consume(n_full, 1, carry, True)

    carry = tuple((jnp.full((tq, 1), NEG_BIG, F32), jnp.zeros((tq, LANES), F32)) for _ in heads)
    n_full = q0 // tk
    scores(0, 0)
    carry = lax.fori_loop(0, n_full // 2, pair, carry)
    carry = lax.cond(n_full % 2 == 1, tail_odd, lambda c: consume(n_full, 0, c, True), carry)
    o = None
    for hh in range(nh - 1, -1, -1):
        acc = carry[hh][1]
        other = ((hh + 1) % nh) * hd
        val = acc / acc[:, other:other + 1]
        o = val if o is None else jnp.where(lane < (hh + 1) * hd, val, o)
    o_ref[...] = o.astype(BF16)


def _fox(q, k, v, cumt, batch, seq, width, hd, fh_pad):
    t = q.shape[0]
    tq = min(TQ_FOX, seq)
    tk = min(TK_FOX, seq)
    assert tk % tq == 0 and seq % tk == 0
    nq = seq // tq
    qmap = lambda b, hp, i: (b * nq + i, hp)
    kvmap = lambda b, hp, i: (b, hp)
    return pl.pallas_call(
        functools.partial(_fox_kernel, tq=tq, tk=tk, hd=hd),
        grid=(batch, width // LANES, nq),
        in_specs=[pl.BlockSpec((tq, LANES), qmap), pl.BlockSpec((seq, LANES), kvmap),
                  pl.BlockSpec((seq, LANES), kvmap),
                  pl.BlockSpec((fh_pad, seq), lambda b, hp, i: (0, b))],
        out_specs=pl.BlockSpec((tq, LANES), qmap),
        out_shape=jax.ShapeDtypeStruct((t, width), BF16),
        scratch_shapes=[pltpu.VMEM((2, LANES // hd, tq, tk), F32)],
        compiler_params=_params(3),
        name="fox",
    )(q, k, v, cumt)


def _merge_kernel(x_ref, yf_ref, mk_ref, mv_ref, g1_ref, wcat_ref, bg_ref, cw_ref, xqg_ref,
                  wbr_ref, wout_ref, g2_ref, wrh_ref, wrl_ref, br_ref, ltri_ref, utri_ref,
                  x1_ref, h2_ref, lpos_ref, lpost_ref, tw_ref, tcnt_ref, tbase_ref,
                  prev_ref, halo_ref, carry_ref, hi_ref, lo_ref, *, tiles_per_batch, width, xa_heads, xa_hd):
    i = pl.program_id(0)
    nt = pl.num_programs(0) - 1
    tm, d = x_ref.shape

    @pl.when(i == 0)
    def _():
        prev_ref[...] = jnp.zeros_like(prev_ref)
        halo_ref[...] = jnp.zeros_like(halo_ref)
        carry_ref[...] = jnp.zeros_like(carry_ref)
        hi_ref[...] = jnp.zeros_like(hi_ref)
        lo_ref[...] = jnp.zeros_like(lo_ref)

    routed = _route(hi_ref[...], lo_ref[...], wrh_ref, wrl_ref, br_ref)

    halo = jnp.where(i < nt, jnp.where(i % tiles_per_batch == 0, 0.0, prev_ref[...]), halo_ref[...])
    halo_ref[...] = halo

    x = x_ref[...]
    hb = _rms(x, g1_ref[...], NORM_EPS).astype(BF16)
    pc = _dot(hb, wcat_ref[...])

    uc = pc[:, :width] * pc[:, width:2 * width]
    row = lax.broadcasted_iota(I32, (tm, 1), 0)
    p1 = halo[SUBLANES - 1:SUBLANES, :]
    p2 = halo[SUBLANES - 2:SUBLANES - 1, :]
    m1 = jnp.where(row == 0, p1, pltpu.roll(uc, 1, 0))
    m2 = jnp.where(row == 0, p2, jnp.where(row == 1, p1, pltpu.roll(uc, 2, 0)))
    y_conv = pc[:, 2 * width:3 * width] * (cw_ref[0:1, :] * m2 + cw_ref[1:2, :] * m1 + cw_ref[2:3, :] * uc)
    prev_ref[...] = uc[tm - SUBLANES:tm, :]

    ys = []
    for h in range(xa_heads):
        sl = slice(3 * width + h * xa_hd, 3 * width + (h + 1) * xa_hd)
        ml = slice(h * xa_hd, (h + 1) * xa_hd)
        qh = _rms(pc[:, sl], xqg_ref[...], QK_EPS) * (xa_hd ** -0.5)
        s = _dot_nt(qh.astype(BF16), mk_ref[:, ml])
        p = jnp.exp(s - jnp.max(s, axis=-1, keepdims=True))
        ys.append(_dot(p.astype(BF16), mv_ref[:, ml]) / jnp.sum(p, axis=-1, keepdims=True))
    y_mem = jnp.concatenate(ys, axis=-1)

    _place(i, *routed, ltri_ref, utri_ref, lpos_ref, lpost_ref, tw_ref, tcnt_ref, tbase_ref, carry_ref)

    merged = jnp.zeros((tm, d), F32)
    for n, yb in enumerate((yf_ref[...], y_conv.astype(BF16), y_mem.astype(BF16))):
        gate = jax.nn.sigmoid(pc[:, 4 * width + n * d:4 * width + (n + 1) * d] + bg_ref[n:n + 1, :])
        merged = merged + gate * _dot(yb, wbr_ref[n])
    x1 = x + _dot(merged.astype(BF16), wout_ref[...])
    x1_ref[...] = x1
    hi, lo = _split_bf16(_rms(x1, g2_ref[...], NORM_EPS))
    h2_ref[...] = hi
    hi_ref[...] = hi
    lo_ref[...] = lo


def _route(hi, lo, wrh_ref, wrl_ref, br_ref):
    tm = hi.shape[0]
    logits = _dot(hi, wrh_ref[...]) + _dot(hi, wrl_ref[...]) + _dot(lo, wrh_ref[...]) + br_ref[...]
    lane = lax.broadcasted_iota(I32, (tm, LANES), 1)
    vals, idxs = [], []
    onehot = jnp.zeros((tm, LANES), F32)
    for _ in range(TOP_K):
        m = jnp.max(logits, axis=-1, keepdims=True)
        idx = jnp.min(jnp.where(logits == m, lane, LANES), axis=-1, keepdims=True)
        sel = lane == idx
        logits = jnp.where(sel, -jnp.inf, logits)
        onehot = onehot + sel.astype(F32)
        vals.append(m)
        idxs.append(idx)
    es = [jnp.exp(v - vals[0]) for v in vals]
    den = es[0]
    for e in es[1:]:
        den = den + e
    return onehot, idxs, [e / den for e in es]


def _place(i, onehot, idxs, tws, ltri_ref, utri_ref, lpos_ref, lpost_ref, tw_ref, tcnt_ref, tbase_ref,
           carry_ref):
    tm = onehot.shape[0]
    lane = lax.broadcasted_iota(I32, (tm, LANES), 1)
    cnt = jnp.sum(onehot, axis=0, keepdims=True)
    span = jnp.floor((cnt + (SUBLANES - 1)) * (1.0 / SUBLANES)) * SUBLANES
    s_hi, s_lo = _split_bf16(jnp.broadcast_to(span, (SUBLANES, LANES)))
    seg_start = (_dot(s_hi, utri_ref[...]) + _dot(s_lo, utri_ref[...]))[0:1, :]
    where_to = _dot(ltri_ref[...], onehot.astype(BF16)) + seg_start
    tcnt_ref[0] = jnp.broadcast_to(span, (SUBLANES, LANES))
    tbase_ref[0] = carry_ref[...]
    carry_ref[...] = jnp.where(i > 0, carry_ref[...] + span, 0.0)

    lpos_o = jnp.zeros((tm, LANES), F32)
    tw_o = jnp.zeros((tm, LANES), F32)
    for kk in range(TOP_K):
        lpos = jnp.sum(jnp.where(lane == idxs[kk], where_to, 0.0), axis=-1, keepdims=True)
        lpos_o = jnp.where(lane == kk, lpos, lpos_o)
        tw_o = jnp.where(lane == kk, tws[kk], tw_o)
    lpos_ref[...] = lpos_o.astype(I32)
    lpost_ref[0] = lpos_o.T[:SUBLANES, :].astype(I32)
    tw_ref[...] = tw_o


def _merge(x2, y_fox, mk, mv, g1, wcat, bg, cw, xqg, wbr, wout, g2, wrh, wrl, br,
           seq, n_mem, width, xa_heads, xa_hd):
    t, d = x2.shape
    tm = min(TM_WINDOW, seq)
    nt = t // tm
    tpb = seq // tm
    ltri = (jnp.arange(tm)[:, None] > jnp.arange(tm)[None, :]).astype(BF16)
    utri = (jnp.arange(LANES)[:, None] < jnp.arange(LANES)[None, :]).astype(BF16)
    mix = lambda i: jnp.minimum(i, nt - 1)
    rte = lambda i: jnp.maximum(i - 1, 0)
    row = lambda i: (mix(i), 0)
    rrow = lambda i: (rte(i), 0)
    tile3 = lambda i: (rte(i), 0, 0)
    mem_map = lambda i: (mix(i) // tpb, 0)
    return pl.pallas_call(
        functools.partial(_merge_kernel, tiles_per_batch=tpb, width=width, xa_heads=xa_heads, xa_hd=xa_hd),
        grid=(nt + 1,),
        in_specs=[pl.BlockSpec((tm, d), row), pl.BlockSpec((tm, width), row),
                  pl.BlockSpec((n_mem, width), mem_map), pl.BlockSpec((n_mem, width), mem_map),
                  _const_spec((1, d)), _const_spec(wcat.shape), _const_spec(bg.shape), _const_spec(cw.shape),
                  _const_spec((1, xa_hd)), _const_spec(wbr.shape), _const_spec((d, d)), _const_spec((1, d)),
                  _const_spec((d, LANES)), _const_spec((d, LANES)), _const_spec((1, LANES)),
                  _const_spec((tm, tm)), _const_spec((LANES, LANES))],
        out_specs=[pl.BlockSpec((tm, d), row), pl.BlockSpec((tm, d), row),
                   pl.BlockSpec((tm, LANES), rrow), pl.BlockSpec((1, SUBLANES, tm), tile3),
                   pl.BlockSpec((tm, LANES), rrow),
                   pl.BlockSpec((1, SUBLANES, LANES), tile3), pl.BlockSpec((1, SUBLANES, LANES), tile3)],
        out_shape=[jax.ShapeDtypeStruct((t, d), F32), jax.ShapeDtypeStruct((t, d), BF16),
                   jax.ShapeDtypeStruct((t, LANES), I32), jax.ShapeDtypeStruct((nt, SUBLANES, tm), I32),
                   jax.ShapeDtypeStruct((t, LANES), F32),
                   jax.ShapeDtypeStruct((nt, SUBLANES, LANES), F32),
                   jax.ShapeDtypeStruct((nt, SUBLANES, LANES), F32)],
        scratch_shapes=[pltpu.VMEM((SUBLANES, width), F32), pltpu.VMEM((SUBLANES, width), F32),
                        pltpu.VMEM((SUBLANES, LANES), F32), pltpu.VMEM((tm, d), BF16), pltpu.VMEM((tm, d), BF16)],
        compiler_params=_params(1),
        name="merge",
    )(x2, y_fox, mk, mv, g1, wcat, bg, cw, xqg, wbr, wout, g2, wrh, wrl, br, ltri, utri)


def _block_rows(tm, n_exp):
    return -(-(TOP_K * tm + (SUBLANES - 1) * n_exp) // MXU_DIM) * MXU_DIM


def _chunked_copies(src_ref, src0, dst_ref, dst0, n, sem, max_rows, wait, fixed_src=False):
    for b in range(max_rows.bit_length() - 1, SUBLANES.bit_length() - 2, -1):
        size = 1 << b
        off = (n >> (b + 1)) << (b + 1)

        @pl.when((n >> b) & 1 == 1)
        def _():
            src = 0 if fixed_src else pl.multiple_of(src0 + off, SUBLANES)
            cp = pltpu.make_async_copy(src_ref.at[pl.ds(src, size)],
                                       dst_ref.at[pl.ds(pl.multiple_of(dst0 + off, SUBLANES), size)], sem)
            if wait:
                cp.wait()
            else:
                cp.start()


def _dispatch_kernel(seg_len_ref, seg_src_ref, seg_dst_ref, pad_start_ref, pad_cnt_ref, n_used_ref,
                     lpost_ref, h2_ref, xs_ref, buf_ref, zero_ref, sems, zsem, *, n_exp, n_tiles):
    w = pl.program_id(0)
    nw = pl.num_programs(0)
    tm = h2_ref.shape[0]
    rows = buf_ref.shape[1]
    slot = w % 2

    def window_copies(win, sl, wait):
        def per_expert(e, c):
            s = win * n_exp + e
            _chunked_copies(buf_ref.at[sl], seg_src_ref[s], xs_ref, seg_dst_ref[s], seg_len_ref[s],
                            sems.at[sl], tm, wait)
            return c
        lax.fori_loop(0, n_exp, per_expert, 0)

    def clear(wait):
        def per_expert(e, c):
            _chunked_copies(zero_ref, 0, xs_ref, pad_start_ref[e], pad_cnt_ref[e], zsem, ZERO_ROWS, wait,
                            fixed_src=True)
            return c

        def per_tile(i, c):
            for part in range(TM_EXPERT // ZERO_ROWS):
                cp = pltpu.make_async_copy(
                    zero_ref, xs_ref.at[pl.ds(i * TM_EXPERT + part * ZERO_ROWS, ZERO_ROWS)], zsem)
                if wait:
                    cp.wait()
                else:
                    cp.start()
            return c

        lax.fori_loop(0, n_exp, per_expert, 0)
        lax.fori_loop(n_used_ref[0], n_tiles, per_tile, 0)

    @pl.when(w == 0)
    def _():
        zero_ref[...] = jnp.zeros_like(zero_ref)
        clear(False)

    r_iota = lax.broadcasted_iota(I32, (rows, tm), 0)
    sel = r_iota == lpost_ref[0, 0:1, :]
    for kk in range(1, TOP_K):
        sel = sel | (r_iota == lpost_ref[0, kk:kk + 1, :])
    perm = jnp.where(sel, 1.0, 0.0).astype(BF16)

    @pl.when(w >= 2)
    def _():
        window_copies(w - 2, slot, True)

    buf_ref[slot] = _dot(perm, h2_ref[...])
    window_copies(w, slot, False)

    @pl.when(w == nw - 1)
    def _():
        @pl.when(nw >= 2)
        def _():
            window_copies(w - 1, 1 - slot, True)
        window_copies(w, slot, True)
        clear(True)


def _dispatch(tables, lpost, h2, n_tiles, n_exp):
    t, d = h2.shape
    tm = lpost.shape[2]
    grid_spec = pltpu.PrefetchScalarGridSpec(
        num_scalar_prefetch=len(tables),
        grid=(t // tm,),
        in_specs=[pl.BlockSpec((1, SUBLANES, tm), lambda i, *_: (i, 0, 0)),
                  pl.BlockSpec((tm, d), lambda i, *_: (i, 0))],
        out_specs=pl.BlockSpec(memory_space=pl.ANY),
        scratch_shapes=[pltpu.VMEM((2, _block_rows(tm, n_exp), d), F32), pltpu.VMEM((ZERO_ROWS, d), F32),
                        pltpu.SemaphoreType.DMA((2,)), pltpu.SemaphoreType.DMA(())],
    )
    return pl.pallas_call(
        functools.partial(_dispatch_kernel, n_exp=n_exp, n_tiles=n_tiles),
        grid_spec=grid_spec,
        out_shape=jax.ShapeDtypeStruct((n_tiles * TM_EXPERT, d), F32),
        compiler_params=_params(1),
        name="dispatch",
    )(*tables, lpost, h2)


def _expert_kernel(tile_exp_ref, n_used_ref, xs_ref, wup_ref, bup_ref, wdn_ref, bdn_ref, ys_ref,
                   wup_bf_ref, wdn_bf_ref, *, d_ff):
    i = pl.program_id(0)

    @pl.when((i == 0) | (tile_exp_ref[i] != tile_exp_ref[jnp.maximum(i - 1, 0)]))
    def _():
        wup_bf_ref[...] = wup_ref[0].astype(BF16)
        wdn_bf_ref[...] = wdn_ref[0].astype(BF16)

    @pl.when(i < n_used_ref[0])
    def _():
        h = _dot(xs_ref[...].astype(BF16), wup_bf_ref[...]) + bup_ref[0]
        glu = jnp.minimum(h[:, :d_ff], SWIGLU_LIMIT)
        lin = jnp.clip(h[:, d_ff:], -SWIGLU_LIMIT, SWIGLU_LIMIT)
        a = glu * jax.nn.sigmoid(SWIGLU_ALPHA * glu) * (lin + 1.0)
        ys_ref[...] = _dot(a.astype(BF16), wdn_bf_ref[...]) + bdn_ref[0]

    @pl.when(i >= n_used_ref[0])
    def _():
        ys_ref[...] = jnp.zeros_like(ys_ref)


def _experts(tile_exp, n_used, xs, wup, bup, wdn, bdn):
    n_rows, d = xs.shape
    d_ff = wdn.shape[1]
    tm = TM_EXPERT
    row = lambda i, te, nu: (i, 0)
    exp3 = lambda i, te, nu: (te[i], 0, 0)
    grid_spec = pltpu.PrefetchScalarGridSpec(
        num_scalar_prefetch=2,
        grid=(n_rows // tm,),
        in_specs=[pl.BlockSpec((tm, d), row),
                  pl.BlockSpec((1, d, 2 * d_ff), exp3), pl.BlockSpec((1, 1, 2 * d_ff), exp3),
                  pl.BlockSpec((1, d_ff, d), exp3), pl.BlockSpec((1, 1, d), exp3)],
        out_specs=pl.BlockSpec((tm, d), row),
        scratch_shapes=[pltpu.VMEM((d, 2 * d_ff), BF16), pltpu.VMEM((d_ff, d), BF16)],
    )
    return pl.pallas_call(
        functools.partial(_expert_kernel, d_ff=d_ff),
        grid_spec=grid_spec,
        out_shape=jax.ShapeDtypeStruct((n_rows, d), F32),
        compiler_params=_params(1),
        name="experts",
    )(tile_exp, n_used, xs, wup, bup, wdn, bdn)


def _combine_kernel(seg_len_ref, seg_src_ref, seg_dst_ref, lpos_ref, tw_ref, x1_ref, ys_ref, o_ref,
                    buf_ref, sems, *, n_exp):
    w = pl.program_id(0)
    nw = pl.num_programs(0)
    tm = x1_ref.shape[0]
    rows = buf_ref.shape[1]
    slot = w % 2

    def window_copies(win, sl, wait):
        def per_expert(e, c):
            s = win * n_exp + e
            _chunked_copies(ys_ref, seg_dst_ref[s], buf_ref.at[sl], seg_src_ref[s], seg_len_ref[s],
                            sems.at[sl], tm, wait)
            return c
        lax.fori_loop(0, n_exp, per_expert, 0)

    @pl.when(w == 0)
    def _():
        buf_ref[...] = jnp.zeros_like(buf_ref)
        window_copies(0, 0, False)

    @pl.when(w + 1 < nw)
    def _():
        window_copies(w + 1, 1 - slot, False)

    window_copies(w, slot, True)

    c_iota = lax.broadcasted_iota(I32, (tm, rows), 1)
    lpos = lpos_ref[...]
    tw = tw_ref[...]
    wperm = jnp.zeros((tm, rows), F32)
    for kk in range(TOP_K):
        wperm = jnp.where(c_iota == lpos[:, kk:kk + 1], tw[:, kk:kk + 1], wperm)
    hi, lo = _split_bf16(wperm)
    y = buf_ref[slot].astype(BF16)
    o_ref[...] = x1_ref[...] + _dot(hi, y) + _dot(lo, y)


def _combine(tables, lpos, tw, x1, ys, n_exp):
    t, d = x1.shape
    tm = min(TM_WINDOW, t)
    row = lambda i, *_: (i, 0)
    grid_spec = pltpu.PrefetchScalarGridSpec(
        num_scalar_prefetch=len(tables),
        grid=(t // tm,),
        in_specs=[pl.BlockSpec((tm, LANES), row), pl.BlockSpec((tm, LANES), row), pl.BlockSpec((tm, d), row),
                  pl.BlockSpec(memory_space=pl.ANY)],
        out_specs=pl.BlockSpec((tm, d), row),
        scratch_shapes=[pltpu.VMEM((2, _block_rows(tm, n_exp), d), F32), pltpu.SemaphoreType.DMA((2,))],
    )
    return pl.pallas_call(
        functools.partial(_combine_kernel, n_exp=n_exp),
        grid_spec=grid_spec,
        out_shape=jax.ShapeDtypeStruct((t, d), F32),
        compiler_params=_params(1),
        name="combine",
    )(*tables, lpos, tw, x1, ys)


def _pad_lanes(a, value=0.0):
    return jnp.pad(a, ((0, 0), (0, LANES - a.shape[-1])), constant_values=value)


def _layer(x, mem, norm1_g, w_in, b_forget, b_gate, fox_q_g, fox_k_g, conv_w, mem_norm_g, w_mem_kv,
           xa_q_g, xa_k_g, w_branch, w_out, norm2_g, w_router, b_router, w_up, b_up, w_down, b_down):
    batch, seq, d = x.shape
    n_mem = mem.shape[1]
    n_branch, width, _ = w_branch.shape
    n_fh = b_forget.shape[0]
    hd = fox_q_g.shape[0]
    xa_hd = xa_q_g.shape[0]
    xa_heads = width // xa_hd
    n_exp = w_router.shape[1]
    assert n_fh * hd == width and n_branch == 3 and conv_w.shape[0] == 3
    assert LANES % hd == 0 and width % MXU_DIM == 0 and n_exp <= LANES and n_fh <= LANES
    fh_pad = -(-n_fh // SUBLANES) * SUBLANES
    t = batch * seq
    x2 = x.reshape(t, d)
    row1 = lambda a: a.reshape(1, -1).astype(F32)

    o_f = 3 * width
    o_conv = o_f + n_fh
    wqkv = w_in[:, :o_f].astype(BF16)
    wf = _pad_lanes(w_in[:, o_f:o_conv]).astype(BF16)
    wcat = w_in[:, o_conv:].astype(BF16)

    mk, mv = _memkv(mem.reshape(batch * n_mem, d), row1(mem_norm_g), w_mem_kv.astype(BF16), row1(xa_k_g),
                    xa_heads, xa_hd)
    q, k, v, cumt = _qkv(x2, row1(norm1_g), wqkv, wf, _pad_lanes(row1(b_forget)),
                         row1(jnp.tile(fox_q_g, n_fh)), row1(jnp.tile(fox_k_g, n_fh)),
                         seq, width, hd, fh_pad)
    y_fox = _fox(q, k, v, cumt, batch, seq, width, hd, fh_pad)

    wr = _pad_lanes(w_router.astype(F32))
    wr_hi = wr.astype(BF16)
    wr_lo = (wr - wr_hi.astype(F32)).astype(BF16)
    x1, h2, lpos, lpost, tw, tcnt, tbase = _merge(
        x2, y_fox, mk, mv, row1(norm1_g), wcat, b_gate.astype(F32), conv_w.astype(F32), row1(xa_q_g),
        w_branch.astype(BF16), w_out.astype(BF16), row1(norm2_g), wr_hi, wr_lo,
        _pad_lanes(row1(b_router), NEG_BIG), seq, n_mem, width, xa_heads, xa_hd)

    seg_len = tcnt[:, 0, :n_exp].astype(I32)
    seg_base = tbase[:, 0, :n_exp].astype(I32)
    counts = seg_base[-1] + seg_len[-1]
    tiles_per = (counts + TM_EXPERT - 1) // TM_EXPERT
    tile_end = jnp.cumsum(tiles_per)
    group_start = (tile_end - tiles_per) * TM_EXPERT
    seg_dst = group_start[None, :] + seg_base
    seg_src = jnp.cumsum(seg_len, axis=1) - seg_len
    n_tiles = -(-(t * TOP_K + (SUBLANES - 1) * seg_len.size) // TM_EXPERT) + n_exp
    n_used = tile_end[-1:].astype(I32)
    tile_ids = jnp.arange(n_tiles, dtype=I32)
    tile_exp = jnp.minimum(jnp.sum(tile_ids[:, None] >= tile_end[None, :], axis=1), n_exp - 1).astype(I32)
    tile_exp = jnp.where(tile_ids < n_used[0], tile_exp, tile_exp[jnp.maximum(n_used[0] - 1, 0)])
    pad_start = (group_start + counts).astype(I32)
    pad_cnt = (tiles_per * TM_EXPERT - counts).astype(I32)
    seg_tables = (seg_len.reshape(-1), seg_src.reshape(-1).astype(I32), seg_dst.reshape(-1).astype(I32))

    xs = _dispatch(seg_tables + (pad_start, pad_cnt, n_used), lpost, h2, n_tiles, n_exp)
    ys = _experts(tile_exp, n_used, xs, w_up.astype(F32), b_up[:, None, :].astype(F32),
                  w_down.astype(F32), b_down[:, None, :].astype(F32))
    out = _combine(seg_tables, lpos, tw, x1, ys, n_exp)
    return out.reshape(batch, seq, d)


def kernel(x, mem, norm1_g, w_in, b_forget, b_gate, fox_q_g, fox_k_g, conv_w, mem_norm_g, w_mem_kv, xa_q_g,
           xa_k_g, w_branch, w_out, norm2_g, w_router, b_router, w_up, b_up, w_down, b_down):
    for l in range(norm1_g.shape[0]):
        x = _layer(x, mem, norm1_g[l], w_in[l], b_forget[l], b_gate[l], fox_q_g[l], fox_k_g[l], conv_w[l],
                   mem_norm_g[l], w_mem_kv[l], xa_q_g[l], xa_k_g[l], w_branch[l], w_out[l], norm2_g[l],
                   w_router[l], b_router[l], w_up[l], b_up[l], w_down[l], b_down[l])
    return x
```

```python
import functools

import jax
import jax.numpy as jnp
from jax import lax
from jax.experimental import pallas as pl
from jax.experimental.pallas import tpu as pltpu

F32 = jnp.float32
BF16 = jnp.bfloat16
I32 = jnp.int32

NORM_EPS = 1e-5
QK_EPS = 1e-6
TOP_K = 4
SWIGLU_ALPHA = 1.702
SWIGLU_LIMIT = 7.0
NEG_BIG = -1e30
LOG2_E = 1.4426950408889634

LANES = 128
SUBLANES = 8
MXU_DIM = 256
VMEM_LIMIT_BYTES = 56 * 1024 * 1024

TM_QKV = 512
TQ_FOX = 512
TK_FOX = 512
TM_WINDOW = 256
TM_EXPERT = 512
ZERO_ROWS = 256
CHUNK_TABLE_LEN = 512


def _params(n_axes):
    return pltpu.CompilerParams(dimension_semantics=("arbitrary",) * n_axes,
                                vmem_limit_bytes=VMEM_LIMIT_BYTES)


def _const_spec(shape):
    zeros = (0,) * len(shape)
    return pl.BlockSpec(shape, lambda *_: zeros)


def _rms(x, g, eps):
    return x * lax.rsqrt(jnp.mean(x * x, axis=-1, keepdims=True) + eps) * g


def _split_bf16(x):
    hi = x.astype(BF16)
    lo = (x - hi.astype(F32)).astype(BF16)
    return hi, lo


def _dot(a, b):
    return jnp.dot(a, b, preferred_element_type=F32)


def _dot_nt(a, b):
    return lax.dot_general(a, b, (((1,), (1,)), ((), ())), preferred_element_type=F32)


def _memkv_kernel(mem_ref, g_ref, w_ref, kg_ref, mk_ref, mv_ref, *, n_heads, hd):
    y = _rms(mem_ref[...], g_ref[...], NORM_EPS)
    kv = _dot(y.astype(BF16), w_ref[...])
    width = n_heads * hd
    for h in range(n_heads):
        sl = slice(h * hd, (h + 1) * hd)
        mk_ref[:, sl] = _rms(kv[:, sl], kg_ref[...], QK_EPS).astype(BF16)
    mv_ref[...] = kv[:, width:].astype(BF16)


def _memkv(mem2, g, w_kv, kg, n_heads, hd):
    rows, d = mem2.shape
    width = n_heads * hd
    tm = min(rows, 512)
    return pl.pallas_call(
        functools.partial(_memkv_kernel, n_heads=n_heads, hd=hd),
        grid=(rows // tm,),
        in_specs=[pl.BlockSpec((tm, d), lambda i: (i, 0)),
                  _const_spec((1, d)), _const_spec((d, 2 * width)), _const_spec((1, hd))],
        out_specs=[pl.BlockSpec((tm, width), lambda i: (i, 0)),
                   pl.BlockSpec((tm, width), lambda i: (i, 0))],
        out_shape=[jax.ShapeDtypeStruct((rows, width), BF16)] * 2,
        compiler_params=_params(1),
        name="memkv",
    )(mem2, g, w_kv, kg)


def _qkv_kernel(x_ref, g1_ref, wqkv_ref, wf_ref, bf_ref, gq_ref, gk_ref, bd_ref, tri_ref,
                q_ref, k_ref, v_ref, cumt_ref, carry_ref,
                *, tiles_per_batch, width, hd, fh_pad):
    i = pl.program_id(0)

    @pl.when(i % tiles_per_batch == 0)
    def _():
        carry_ref[...] = jnp.zeros_like(carry_ref)

    hb = _rms(x_ref[...], g1_ref[...], NORM_EPS).astype(BF16)
    qkv = _dot(hb, wqkv_ref[...])

    def head_norm(z, g):
        sq = (z * z).astype(BF16)
        parts = []
        for c in range(width // MXU_DIM):
            sl = slice(c * MXU_DIM, (c + 1) * MXU_DIM)
            parts.append(_dot(sq[:, sl], bd_ref[...]))
        ms = jnp.concatenate(parts, axis=-1) * (1.0 / hd)
        return z * lax.rsqrt(ms + QK_EPS) * g

    q = head_norm(qkv[:, :width], gq_ref[...]) * (hd ** -0.5 * LOG2_E)
    k = head_norm(qkv[:, width:2 * width], gk_ref[...])
    q_ref[...] = q.astype(BF16)
    k_ref[...] = k.astype(BF16)
    v_ref[...] = qkv[:, 2 * width:].astype(BF16)

    z = _dot(hb, wf_ref[...]) + bf_ref[...]
    log_f = jnp.minimum(z, 0.0) - jnp.log(1.0 + jnp.exp(-jnp.abs(z)))
    hi, lo = _split_bf16(log_f)
    cum = _dot(tri_ref[...], hi) + _dot(tri_ref[...], lo) + carry_ref[0:1, :]
    tm = cum.shape[0]
    carry_ref[...] = jnp.broadcast_to(cum[tm - 1:tm, :], carry_ref.shape)
    cumt_ref[...] = (cum * LOG2_E).T[:fh_pad, :]


def _qkv(x2, g1, wqkv, wf, bf, gq_t, gk_t, seq, width, hd, fh_pad):
    t, d = x2.shape
    tm = min(TM_QKV, seq)
    bd = (jnp.arange(MXU_DIM)[:, None] // hd == jnp.arange(MXU_DIM)[None, :] // hd).astype(BF16)
    tri = (jnp.arange(tm)[:, None] >= jnp.arange(tm)[None, :]).astype(BF16)
    row = lambda i: (i, 0)
    return pl.pallas_call(
        functools.partial(_qkv_kernel, tiles_per_batch=seq // tm, width=width, hd=hd, fh_pad=fh_pad),
        grid=(t // tm,),
        in_specs=[pl.BlockSpec((tm, d), row), _const_spec((1, d)), _const_spec((d, 3 * width)),
                  _const_spec((d, LANES)), _const_spec((1, LANES)), _const_spec((1, width)),
                  _const_spec((1, width)), _const_spec((MXU_DIM, MXU_DIM)), _const_spec((tm, tm))],
        out_specs=[pl.BlockSpec((tm, width), row)] * 3 + [pl.BlockSpec((fh_pad, tm), lambda i: (0, i))],
        out_shape=[jax.ShapeDtypeStruct((t, width), BF16)] * 3 + [jax.ShapeDtypeStruct((fh_pad, t), F32)],
        scratch_shapes=[pltpu.VMEM((SUBLANES, LANES), F32)],
        compiler_params=_params(1),
        name="qkv",
    )(x2, g1, wqkv, wf, bf, gq_t, gk_t, bd, tri)


def _fox_kernel(q_ref, k_ref, v_ref, cumt_ref, o_ref, s_ref, *, tq, tk, hd):
    hp = pl.program_id(1)
    q0 = pl.program_id(2) * tq
    nh = LANES // hd
    lane = lax.broadcasted_iota(I32, (1, LANES), 1)
    rows = lax.broadcasted_iota(I32, (tq, tk), 0) + q0
    cols = lax.broadcasted_iota(I32, (tq, tk), 1)
    q = q_ref[...]
    heads = []
    for hh in range(nh):
        in_head = (lane >= hh * hd) & (lane < (hh + 1) * hd)
        heads.append((hp * nh + hh, in_head, jnp.where(in_head, q, jnp.zeros_like(q))))

    def scores(j, slot):
        start = pl.multiple_of(j * tk, tk)
        kj = k_ref[pl.ds(start, tk), :]
        for hh, (h, _, qh) in enumerate(heads):
            s_ref[slot, hh] = _dot_nt(qh, kj) - cumt_ref[pl.ds(h, 1), pl.ds(start, tk)]

    def consume(j, slot, carry, masked):
        start = pl.multiple_of(j * tk, tk)
        vj = v_ref[pl.ds(start, tk), :]
        out = []
        for hh, ((_, in_head, _), (m, acc)) in enumerate(zip(heads, carry)):
            vh = jnp.where(in_head, vj, jnp.ones_like(vj))
            s = s_ref[slot, hh]
            if masked:
                s = jnp.where(cols + start <= rows, s, NEG_BIG)
            m_new = jnp.maximum(m, jnp.max(s, axis=-1, keepdims=True))
            p = jnp.exp2(s - m_new)
            out.append((m_new, jnp.exp2(m - m_new) * acc + _dot(p.astype(BF16), vh)))
        return tuple(out)

    def pair(i, carry):
        j = 2 * i
        scores(j + 1, 1)
        carry = consume(j, 0, carry, False)
        scores(j + 2, 0)
        return consume(j + 1, 1, carry, False)

    def tail_odd(carry):
        scores(n_full, 1)
        carry = consume(n_full - 1, 0, carry, False)
        return consume(n_full, 1, carry, True)

    carry = tuple((jnp.full((tq, 1), NEG_BIG, F32), jnp.zeros((tq, LANES), F32)) for _ in heads)
    n_full = q0 // tk
    scores(0, 0)
    carry = lax.fori_loop(0, n_full // 2, pair, carry)
    carry = lax.cond(n_full % 2 == 1, tail_odd, lambda c: consume(n_full, 0, c, True), carry)
    o = None
    for hh in range(nh - 1, -1, -1):
        acc = carry[hh][1]
        other = ((hh + 1) % nh) * hd
        val = acc / acc[:, other:other + 1]
        o = val if o is None else jnp.where(lane < (hh + 1) * hd, val, o)
    o_ref[...] = o.astype(BF16)


def _fox(q, k, v, cumt, batch, seq, width, hd, fh_pad):
    t = q.shape[0]
    tq = min(TQ_FOX, seq)
    tk = min(TK_FOX, seq)
    assert tk % tq == 0 and seq % tk == 0
    nq = seq // tq
    qmap = lambda b, hp, i: (b * nq + i, hp)
    kvmap = lambda b, hp, i: (b, hp)
    return pl.pallas_call(
        functools.partial(_fox_kernel, tq=tq, tk=tk, hd=hd),
        grid=(batch, width // LANES, nq),
        in_specs=[pl.BlockSpec((tq, LANES), qmap), pl.BlockSpec((seq, LANES), kvmap),
                  pl.BlockSpec((seq, LANES), kvmap),
                  pl.BlockSpec((fh_pad, seq), lambda b, hp, i: (0, b))],
        out_specs=pl.BlockSpec((tq, LANES), qmap),
        out_shape=jax.ShapeDtypeStruct((t, width), BF16),
        scratch_shapes=[pltpu.VMEM((2, LANES // hd, tq, tk), F32)],
        compiler_params=_params(3),
        name="fox",
    )(q, k, v, cumt)


def _merge_kernel(x_ref, yf_ref, mk_ref, mv_ref, g1_ref, wcat_ref, bg_ref, cw_ref, xqg_ref,
                  wbr_ref, wout_ref, g2_ref, wrh_ref, wrl_ref, br_ref, ltri_ref, utri_ref,
                  x1_ref, h2_ref, lpos_ref, lpost_ref, tw_ref, tcnt_ref, tbase_ref,
                  prev_ref, halo_ref, carry_ref, hi_ref, lo_ref, *, tiles_per_batch, width, xa_heads, xa_hd):
    i = pl.program_id(0)
    nt = pl.num_programs(0) - 1
    tm, d = x_ref.shape

    @pl.when(i == 0)
    def _():
        prev_ref[...] = jnp.zeros_like(prev_ref)
        halo_ref[...] = jnp.zeros_like(halo_ref)
        carry_ref[...] = jnp.zeros_like(carry_ref)
        hi_ref[...] = jnp.zeros_like(hi_ref)
        lo_ref[...] = jnp.zeros_like(lo_ref)

    routed = _route(hi_ref[...], lo_ref[...], wrh_ref, wrl_ref, br_ref)

    halo = jnp.where(i < nt, jnp.where(i % tiles_per_batch == 0, 0.0, prev_ref[...]), halo_ref[...])
    halo_ref[...] = halo

    x = x_ref[...]
    hb = _rms(x, g1_ref[...], NORM_EPS).astype(BF16)
    pc = _dot(hb, wcat_ref[...])

    uc = pc[:, :width] * pc[:, width:2 * width]
    row = lax.broadcasted_iota(I32, (tm, 1), 0)
    p1 = halo[SUBLANES - 1:SUBLANES, :]
    p2 = halo[SUBLANES - 2:SUBLANES - 1, :]
    m1 = jnp.where(row == 0, p1, pltpu.roll(uc, 1, 0))
    m2 = jnp.where(row == 0, p2, jnp.where(row == 1, p1, pltpu.roll(uc, 2, 0)))
    y_conv = pc[:, 2 * width:3 * width] * (cw_ref[0:1, :] * m2 + cw_ref[1:2, :] * m1 + cw_ref[2:3, :] * uc)
    prev_ref[...] = uc[tm - SUBLANES:tm, :]

    ys = []
    for h in range(xa_heads):
        sl = slice(3 * width + h * xa_hd, 3 * width + (h + 1) * xa_hd)
        ml = slice(h * xa_hd, (h + 1) * xa_hd)
        qh = _rms(pc[:, sl], xqg_ref[...], QK_EPS) * (xa_hd ** -0.5)
        s = _dot_nt(qh.astype(BF16), mk_ref[:, ml])
        p = jnp.exp(s - jnp.max(s, axis=-1, keepdims=True))
        ys.append(_dot(p.astype(BF16), mv_ref[:, ml]) / jnp.sum(p, axis=-1, keepdims=True))
    y_mem = jnp.concatenate(ys, axis=-1)

    _place(i, *routed, ltri_ref, utri_ref, lpos_ref, lpost_ref, tw_ref, tcnt_ref, tbase_ref, carry_ref)

    merged = jnp.zeros((tm, d), F32)
    for n, yb in enumerate((yf_ref[...], y_conv.astype(BF16), y_mem.astype(BF16))):
        gate = jax.nn.sigmoid(pc[:, 4 * width + n * d:4 * width + (n + 1) * d] + bg_ref[n:n + 1, :])
        merged = merged + gate * _dot(yb, wbr_ref[n])
    x1 = x + _dot(merged.astype(BF16), wout_ref[...])
    x1_ref[...] = x1
    hi, lo = _split_bf16(_rms(x1, g2_ref[...], NORM_EPS))
    h2_ref[...] = hi
    hi_ref[...] = hi
    lo_ref[...] = lo


def _route(hi, lo, wrh_ref, wrl_ref, br_ref):
    tm = hi.shape[0]
    logits = _dot(hi, wrh_ref[...]) + _dot(hi, wrl_ref[...]) + _dot(lo, wrh_ref[...]) + br_ref[...]
    lane = lax.broadcasted_iota(I32, (tm, LANES), 1)
    vals, idxs = [], []
    onehot = jnp.zeros((tm, LANES), F32)
    for _ in range(TOP_K):
        m = jnp.max(logits, axis=-1, keepdims=True)
        idx = jnp.min(jnp.where(logits == m, lane, LANES), axis=-1, keepdims=True)
        sel = lane == idx
        logits = jnp.where(sel, -jnp.inf, logits)
        onehot = onehot + sel.astype(F32)
        vals.append(m)
        idxs.append(idx)
    es = [jnp.exp(v - vals[0]) for v in vals]
    den = es[0]
    for e in es[1:]:
        den = den + e
    return onehot, idxs, [e / den for e in es]


def _place(i, onehot, idxs, tws, ltri_ref, utri_ref, lpos_ref, lpost_ref, tw_ref, tcnt_ref, tbase_ref,
           carry_ref):
    tm = onehot.shape[0]
    lane = lax.broadcasted_iota(I32, (tm, LANES), 1)
    cnt = jnp.sum(onehot, axis=0, keepdims=True)
    span = jnp.floor((cnt + (SUBLANES - 1)) * (1.0 / SUBLANES)) * SUBLANES
    s_hi, s_lo = _split_bf16(jnp.broadcast_to(span, (SUBLANES, LANES)))
    seg_start = (_dot(s_hi, utri_ref[...]) + _dot(s_lo, utri_ref[...]))[0:1, :]
    where_to = _dot(ltri_ref[...], onehot.astype(BF16)) + seg_start
    tcnt_ref[0] = jnp.broadcast_to(span, (SUBLANES, LANES))
    tbase_ref[0] = carry_ref[...]
    carry_ref[...] = jnp.where(i > 0, carry_ref[...] + span, 0.0)

    lpos_o = jnp.zeros((tm, LANES), F32)
    tw_o = jnp.zeros((tm, LANES), F32)
    for kk in range(TOP_K):
        lpos = jnp.sum(jnp.where(lane == idxs[kk], where_to, 0.0), axis=-1, keepdims=True)
        lpos_o = jnp.where(lane == kk, lpos, lpos_o)
        tw_o = jnp.where(lane == kk, tws[kk], tw_o)
    lpos_ref[...] = lpos_o.astype(I32)
    lpost_ref[0] = lpos_o.T[:SUBLANES, :].astype(I32)
    tw_ref[...] = tw_o


def _merge(x2, y_fox, mk, mv, g1, wcat, bg, cw, xqg, wbr, wout, g2, wrh, wrl, br,
           seq, n_mem, width, xa_heads, xa_hd):
    t, d = x2.shape
    tm = min(TM_WINDOW, seq)
    nt = t // tm
    tpb = seq // tm
    ltri = (jnp.arange(tm)[:, None] > jnp.arange(tm)[None, :]).astype(BF16)
    utri = (jnp.arange(LANES)[:, None] < jnp.arange(LANES)[None, :]).astype(BF16)
    mix = lambda i: jnp.minimum(i, nt - 1)
    rte = lambda i: jnp.maximum(i - 1, 0)
    row = lambda i: (mix(i), 0)
    rrow = lambda i: (rte(i), 0)
    tile3 = lambda i: (rte(i), 0, 0)
    mem_map = lambda i: (mix(i) // tpb, 0)
    return pl.pallas_call(
        functools.partial(_merge_kernel, tiles_per_batch=tpb, width=width, xa_heads=xa_heads, xa_hd=xa_hd),
        grid=(nt + 1,),
        in_specs=[pl.BlockSpec((tm, d), row), pl.BlockSpec((tm, width), row),
                  pl.BlockSpec((n_mem, width), mem_map), pl.BlockSpec((n_mem, width), mem_map),
                  _const_spec((1, d)), _const_spec(wcat.shape), _const_spec(bg.shape), _const_spec(cw.shape),
                  _const_spec((1, xa_hd)), _const_spec(wbr.shape), _const_spec((d, d)), _const_spec((1, d)),
                  _const_spec((d, LANES)), _const_spec((d, LANES)), _const_spec((1, LANES)),
                  _const_spec((tm, tm)), _const_spec((LANES, LANES))],
        out_specs=[pl.BlockSpec((tm, d), row), pl.BlockSpec((tm, d), row),
                   pl.BlockSpec((tm, LANES), rrow), pl.BlockSpec((1, SUBLANES, tm), tile3),
                   pl.BlockSpec((tm, LANES), rrow),
                   pl.BlockSpec((1, SUBLANES, LANES), tile3), pl.BlockSpec((1, SUBLANES, LANES), tile3)],
        out_shape=[jax.ShapeDtypeStruct((t, d), F32), jax.ShapeDtypeStruct((t, d), BF16),
                   jax.ShapeDtypeStruct((t, LANES), I32), jax.ShapeDtypeStruct((nt, SUBLANES, tm), I32),
                   jax.ShapeDtypeStruct((t, LANES), F32),
                   jax.ShapeDtypeStruct((nt, SUBLANES, LANES), F32),
                   jax.ShapeDtypeStruct((nt, SUBLANES, LANES), F32)],
        scratch_shapes=[pltpu.VMEM((SUBLANES, width), F32), pltpu.VMEM((SUBLANES, width), F32),
                        pltpu.VMEM((SUBLANES, LANES), F32), pltpu.VMEM((tm, d), BF16), pltpu.VMEM((tm, d), BF16)],
        compiler_params=_params(1),
        name="merge",
    )(x2, y_fox, mk, mv, g1, wcat, bg, cw, xqg, wbr, wout, g2, wrh, wrl, br, ltri, utri)


def _block_rows(tm, n_exp):
    return -(-(TOP_K * tm + (SUBLANES - 1) * n_exp) // MXU_DIM) * MXU_DIM


def _chunked_copies(src_ref, src0, dst_ref, dst0, n, sem, max_rows, wait, fixed_src=False):
    for b in range(max_rows.bit_length() - 1, SUBLANES.bit_length() - 2, -1):
        size = 1 << b
        off = (n >> (b + 1)) << (b + 1)

        @pl.when((n >> b) & 1 == 1)
        def _():
            src = 0 if fixed_src else pl.multiple_of(src0 + off, SUBLANES)
            cp = pltpu.make_async_copy(src_ref.at[pl.ds(src, size)],
                                       dst_ref.at[pl.ds(pl.multiple_of(dst0 + off, SUBLANES), size)], sem)
            if wait:
                cp.wait()
            else:
                cp.start()


def _chunk_classes(tm):
    return [1 << b for b in range(SUBLANES.bit_length() - 1, tm.bit_length())]


def _chunk_table(seg_len, seg_src, seg_dst, tm, n_exp):
    sizes = jnp.asarray(_chunk_classes(tm), I32)
    bit = (seg_len[:, :, None] // sizes) % 2
    off = seg_len[:, :, None] // (2 * sizes) * (2 * sizes)
    slot = jnp.cumsum(bit, axis=1) - 1
    hit = (slot[:, :, None, :] == jnp.arange(n_exp, dtype=I32)[None, None, :, None]) & (bit[:, :, None, :] == 1)
    pack = lambda start: jnp.sum(jnp.where(hit, (start[:, :, None] + off)[:, :, None, :], 0), axis=1)
    nw = seg_len.shape[0]
    cols = [jnp.swapaxes(pack(seg_src), 1, 2).reshape(nw, -1), jnp.swapaxes(pack(seg_dst), 1, 2).reshape(nw, -1),
            jnp.sum(bit, axis=1)]
    tbl = jnp.concatenate(cols, axis=1).astype(I32)
    assert tbl.shape[1] <= CHUNK_TABLE_LEN
    return jnp.pad(tbl, ((0, 0), (0, CHUNK_TABLE_LEN - tbl.shape[1]))).reshape(-1)


def _table_copies(tbl_ref, block_ref, sorted_ref, sem, tm, n_exp, to_sorted, wait):
    sizes = _chunk_classes(tm)
    for c, size in enumerate(sizes):
        def body(i, carry, c=c, size=size):
            blk = block_ref.at[pl.ds(pl.multiple_of(tbl_ref[c * n_exp + i], SUBLANES), size)]
            srt = sorted_ref.at[pl.ds(pl.multiple_of(tbl_ref[(len(sizes) + c) * n_exp + i], SUBLANES), size)]
            cp = pltpu.make_async_copy(blk, srt, sem) if to_sorted else pltpu.make_async_copy(srt, blk, sem)
            if wait:
                cp.wait()
            else:
                cp.start()
            return carry
        lax.fori_loop(0, tbl_ref[2 * len(sizes) * n_exp + c], body, 0)


def _dispatch_kernel(pad_start_ref, pad_cnt_ref, n_used_ref, tbl_ref, tbl_prev_ref, lpost_ref, h2_ref, xs_ref,
                     buf_ref, zero_ref, sems, zsem, *, n_exp, n_tiles):
    w = pl.program_id(0)
    nw = pl.num_programs(0)
    tm = h2_ref.shape[0]
    rows = buf_ref.shape[1]
    slot = w % 2

    def clear(wait):
        def per_expert(e, c):
            _chunked_copies(zero_ref, 0, xs_ref, pad_start_ref[e], pad_cnt_ref[e], zsem, ZERO_ROWS, wait,
                            fixed_src=True)
            return c

        def per_tile(i, c):
            for part in range(TM_EXPERT // ZERO_ROWS):
                cp = pltpu.make_async_copy(
                    zero_ref, xs_ref.at[pl.ds(i * TM_EXPERT + part * ZERO_ROWS, ZERO_ROWS)], zsem)
                if wait:
                    cp.wait()
                else:
                    cp.start()
            return c

        lax.fori_loop(0, n_exp, per_expert, 0)
        lax.fori_loop(n_used_ref[0], n_tiles, per_tile, 0)

    @pl.when(w == 0)
    def _():
        zero_ref[...] = jnp.zeros_like(zero_ref)
        clear(False)

    r_iota = lax.broadcasted_iota(I32, (rows, tm), 0)
    sel = r_iota == lpost_ref[0, 0:1, :]
    for kk in range(1, TOP_K):
        sel = sel | (r_iota == lpost_ref[0, kk:kk + 1, :])
    perm = jnp.where(sel, 1.0, 0.0).astype(BF16)

    buf_ref[slot] = _dot(perm, h2_ref[...])
    _table_copies(tbl_ref, buf_ref.at[slot], xs_ref, sems.at[slot], tm, n_exp, True, False)

    @pl.when(w >= 1)
    def _():
        _table_copies(tbl_prev_ref, buf_ref.at[1 - slot], xs_ref, sems.at[1 - slot], tm, n_exp, True, True)

    @pl.when(w == nw - 1)
    def _():
        _table_copies(tbl_ref, buf_ref.at[slot], xs_ref, sems.at[slot], tm, n_exp, True, True)
        clear(True)


def _dispatch(pad_start, pad_cnt, n_used, chunk_tbl, lpost, h2, n_tiles, n_exp):
    t, d = h2.shape
    tm = lpost.shape[2]
    smem_blk = lambda shift: pl.BlockSpec((CHUNK_TABLE_LEN,), lambda i, *_: (jnp.maximum(i + shift, 0),),
                                          memory_space=pltpu.SMEM)
    grid_spec = pltpu.PrefetchScalarGridSpec(
        num_scalar_prefetch=3,
        grid=(t // tm,),
        in_specs=[smem_blk(0), smem_blk(-1),
                  pl.BlockSpec((1, SUBLANES, tm), lambda i, *_: (i, 0, 0)),
                  pl.BlockSpec((tm, d), lambda i, *_: (i, 0))],
        out_specs=pl.BlockSpec(memory_space=pl.ANY),
        scratch_shapes=[pltpu.VMEM((2, _block_rows(tm, n_exp), d), F32), pltpu.VMEM((ZERO_ROWS, d), F32),
                        pltpu.SemaphoreType.DMA((2,)), pltpu.SemaphoreType.DMA(())],
    )
    return pl.pallas_call(
        functools.partial(_dispatch_kernel, n_exp=n_exp, n_tiles=n_tiles),
        grid_spec=grid_spec,
        out_shape=jax.ShapeDtypeStruct((n_tiles * TM_EXPERT, d), F32),
        compiler_params=_params(1),
        name="dispatch",
    )(pad_start, pad_cnt, n_used, chunk_tbl, chunk_tbl, lpost, h2)


def _expert_kernel(tile_exp_ref, n_used_ref, xs_ref, wup_ref, bup_ref, wdn_ref, bdn_ref, ys_ref,
                   wup_bf_ref, wdn_bf_ref, *, d_ff):
    i = pl.program_id(0)

    @pl.when((i == 0) | (tile_exp_ref[i] != tile_exp_ref[jnp.maximum(i - 1, 0)]))
    def _():
        wup_bf_ref[...] = wup_ref[0].astype(BF16)
        wdn_bf_ref[...] = wdn_ref[0].astype(BF16)

    @pl.when(i < n_used_ref[0])
    def _():
        h = _dot(xs_ref[...].astype(BF16), wup_bf_ref[...]) + bup_ref[0]
        glu = jnp.minimum(h[:, :d_ff], SWIGLU_LIMIT)
        lin = jnp.clip(h[:, d_ff:], -SWIGLU_LIMIT, SWIGLU_LIMIT)
        a = glu * jax.nn.sigmoid(SWIGLU_ALPHA * glu) * (lin + 1.0)
        ys_ref[...] = _dot(a.astype(BF16), wdn_bf_ref[...]) + bdn_ref[0]

    @pl.when(i >= n_used_ref[0])
    def _():
        ys_ref[...] = jnp.zeros_like(ys_ref)


def _experts(tile_exp, n_used, xs, wup, bup, wdn, bdn):
    n_rows, d = xs.shape
    d_ff = wdn.shape[1]
    tm = TM_EXPERT
    row = lambda i, te, nu: (i, 0)
    used_row = lambda i, te, nu: (jnp.minimum(i, nu[0] - 1), 0)
    exp3 = lambda i, te, nu: (te[i], 0, 0)
    grid_spec = pltpu.PrefetchScalarGridSpec(
        num_scalar_prefetch=2,
        grid=(n_rows // tm,),
        in_specs=[pl.BlockSpec((tm, d), used_row),
                  pl.BlockSpec((1, d, 2 * d_ff), exp3), pl.BlockSpec((1, 1, 2 * d_ff), exp3),
                  pl.BlockSpec((1, d_ff, d), exp3), pl.BlockSpec((1, 1, d), exp3)],
        out_specs=pl.BlockSpec((tm, d), row),
        scratch_shapes=[pltpu.VMEM((d, 2 * d_ff), BF16), pltpu.VMEM((d_ff, d), BF16)],
    )
    return pl.pallas_call(
        functools.partial(_expert_kernel, d_ff=d_ff),
        grid_spec=grid_spec,
        out_shape=jax.ShapeDtypeStruct((n_rows, d), F32),
        compiler_params=_params(1),
        name="experts",
    )(tile_exp, n_used, xs, wup, bup, wdn, bdn)


def _combine_kernel(tbl_ref, tbl_next_ref, lpos_ref, tw_ref, x1_ref, ys_ref, o_ref, buf_ref, sems, *, n_exp):
    w = pl.program_id(0)
    nw = pl.num_programs(0)
    tm = x1_ref.shape[0]
    rows = buf_ref.shape[1]
    slot = w % 2

    @pl.when(w == 0)
    def _():
        buf_ref[...] = jnp.zeros_like(buf_ref)
        _table_copies(tbl_ref, buf_ref.at[0], ys_ref, sems.at[0], tm, n_exp, False, False)

    @pl.when(w + 1 < nw)
    def _():
        _table_copies(tbl_next_ref, buf_ref.at[1 - slot], ys_ref, sems.at[1 - slot], tm, n_exp, False, False)

    _table_copies(tbl_ref, buf_ref.at[slot], ys_ref, sems.at[slot], tm, n_exp, False, True)

    c_iota = lax.broadcasted_iota(I32, (tm, rows), 1)
    lpos = lpos_ref[...]
    tw = tw_ref[...]
    wperm = jnp.zeros((tm, rows), F32)
    for kk in range(TOP_K):
        wperm = jnp.where(c_iota == lpos[:, kk:kk + 1], tw[:, kk:kk + 1], wperm)
    o_ref[...] = x1_ref[...] + _dot(wperm.astype(BF16), buf_ref[slot].astype(BF16))


def _combine(chunk_tbl, lpos, tw, x1, ys, n_exp):
    t, d = x1.shape
    tm = min(TM_WINDOW, t)
    nw = t // tm
    row = lambda i: (i, 0)
    smem_blk = lambda shift: pl.BlockSpec((CHUNK_TABLE_LEN,), lambda i: (jnp.minimum(i + shift, nw - 1),),
                                          memory_space=pltpu.SMEM)
    return pl.pallas_call(
        functools.partial(_combine_kernel, n_exp=n_exp),
        grid=(nw,),
        in_specs=[smem_blk(0), smem_blk(1),
                  pl.BlockSpec((tm, LANES), row), pl.BlockSpec((tm, LANES), row), pl.BlockSpec((tm, d), row),
                  pl.BlockSpec(memory_space=pl.ANY)],
        out_specs=pl.BlockSpec((tm, d), row),
        out_shape=jax.ShapeDtypeStruct((t, d), F32),
        scratch_shapes=[pltpu.VMEM((2, _block_rows(tm, n_exp), d), F32), pltpu.SemaphoreType.DMA((2,))],
        compiler_params=_params(1),
        name="combine",
    )(chunk_tbl, chunk_tbl, lpos, tw, x1, ys)


def _pad_lanes(a, value=0.0):
    return jnp.pad(a, ((0, 0), (0, LANES - a.shape[-1])), constant_values=value)


def _layer(x, mem, norm1_g, w_in, b_forget, b_gate, fox_q_g, fox_k_g, conv_w, mem_norm_g, w_mem_kv,
           xa_q_g, xa_k_g, w_branch, w_out, norm2_g, w_router, b_router, w_up, b_up, w_down, b_down):
    batch, seq, d = x.shape
    n_mem = mem.shape[1]
    n_branch, width, _ = w_branch.shape
    n_fh = b_forget.shape[0]
    hd = fox_q_g.shape[0]
    xa_hd = xa_q_g.shape[0]
    xa_heads = width // xa_hd
    n_exp = w_router.shape[1]
    assert n_fh * hd == width and n_branch == 3 and conv_w.shape[0] == 3
    assert LANES % hd == 0 and width % MXU_DIM == 0 and n_exp <= LANES and n_fh <= LANES
    fh_pad = -(-n_fh // SUBLANES) * SUBLANES
    t = batch * seq
    x2 = x.reshape(t, d)
    row1 = lambda a: a.reshape(1, -1).astype(F32)

    o_f = 3 * width
    o_conv = o_f + n_fh
    wqkv = w_in[:, :o_f].astype(BF16)
    wf = _pad_lanes(w_in[:, o_f:o_conv]).astype(BF16)
    wcat = w_in[:, o_conv:].astype(BF16)

    mk, mv = _memkv(mem.reshape(batch * n_mem, d), row1(mem_norm_g), w_mem_kv.astype(BF16), row1(xa_k_g),
                    xa_heads, xa_hd)
    q, k, v, cumt = _qkv(x2, row1(norm1_g), wqkv, wf, _pad_lanes(row1(b_forget)),
                         row1(jnp.tile(fox_q_g, n_fh)), row1(jnp.tile(fox_k_g, n_fh)),
                         seq, width, hd, fh_pad)
    y_fox = _fox(q, k, v, cumt, batch, seq, width, hd, fh_pad)

    wr = _pad_lanes(w_router.astype(F32))
    wr_hi = wr.astype(BF16)
    wr_lo = (wr - wr_hi.astype(F32)).astype(BF16)
    x1, h2, lpos, lpost, tw, tcnt, tbase = _merge(
        x2, y_fox, mk, mv, row1(norm1_g), wcat, b_gate.astype(F32), conv_w.astype(F32), row1(xa_q_g),
        w_branch.astype(BF16), w_out.astype(BF16), row1(norm2_g), wr_hi, wr_lo,
        _pad_lanes(row1(b_router), NEG_BIG), seq, n_mem, width, xa_heads, xa_hd)

    seg_len = tcnt[:, 0, :n_exp].astype(I32)
    seg_base = tbase[:, 0, :n_exp].astype(I32)
    counts = seg_base[-1] + seg_len[-1]
    tiles_per = (counts + TM_EXPERT - 1) // TM_EXPERT
    tile_end = jnp.cumsum(tiles_per)
    group_start = (tile_end - tiles_per) * TM_EXPERT
    seg_dst = group_start[None, :] + seg_base
    seg_src = jnp.cumsum(seg_len, axis=1) - seg_len
    n_tiles = -(-(t * TOP_K + (SUBLANES - 1) * seg_len.size) // TM_EXPERT) + n_exp
    n_used = tile_end[-1:].astype(I32)
    tile_ids = jnp.arange(n_tiles, dtype=I32)
    tile_exp = jnp.minimum(jnp.sum(tile_ids[:, None] >= tile_end[None, :], axis=1), n_exp - 1).astype(I32)
    tile_exp = jnp.where(tile_ids < n_used[0], tile_exp, tile_exp[jnp.maximum(n_used[0] - 1, 0)])
    pad_start = (group_start + counts).astype(I32)
    pad_cnt = (tiles_per * TM_EXPERT - counts).astype(I32)
    chunk_tbl = _chunk_table(seg_len, seg_src.astype(I32), seg_dst.astype(I32), lpost.shape[2], n_exp)

    xs = _dispatch(pad_start, pad_cnt, n_used, chunk_tbl, lpost, h2, n_tiles, n_exp)
    ys = _experts(tile_exp, n_used, xs, w_up.astype(F32), b_up[:, None, :].astype(F32),
                  w_down.astype(F32), b_down[:, None, :].astype(F32))
    out = _combine(chunk_tbl, lpos, tw, x1, ys, n_exp)
    return out.reshape(batch, seq, d)


def kernel(x, mem, norm1_g, w_in, b_forget, b_gate, fox_q_g, fox_k_g, conv_w, mem_norm_g, w_mem_kv, xa_q_g,
           xa_k_g, w_branch, w_out, norm2_g, w_router, b_router, w_up, b_up, w_down, b_down):
    for l in range(norm1_g.shape[0]):
        x = _layer(x, mem, norm1_g[l], w_in[l], b_forget[l], b_gate[l], fox_q_g[l], fox_k_g[l], conv_w[l],
                   mem_norm_g[l], w_mem_kv[l], xa_q_g[l], xa_k_g[l], w_branch[l], w_out[l], norm2_g[l],
                   w_router[l], b_router[l], w_up[l], b_up[l], w_down[l], b_down[l])
    return x
```

```python
import functools

import jax
import jax.numpy as jnp
from jax import lax
from jax.experimental import pallas as pl
from jax.experimental.pallas import tpu as pltpu

F32 = jnp.float32
BF16 = jnp.bfloat16
I32 = jnp.int32

NORM_EPS = 1e-5
QK_EPS = 1e-6
TOP_K = 4
SWIGLU_ALPHA = 1.702
SWIGLU_LIMIT = 7.0
NEG_BIG = -1e30
LOG2_E = 1.4426950408889634

LANES = 128
SUBLANES = 8
MXU_DIM = 256
VMEM_LIMIT_BYTES = 56 * 1024 * 1024

TM_QKV = 512
TQ_FOX = 512
TM_WINDOW = 256
TM_EXPERT = 512
ZERO_ROWS = 256
CHUNK_TABLE_LEN = 512


def _params(n_axes):
    return pltpu.CompilerParams(dimension_semantics=("arbitrary",) * n_axes,
                                vmem_limit_bytes=VMEM_LIMIT_BYTES)


def _const_spec(shape):
    zeros = (0,) * len(shape)
    return pl.BlockSpec(shape, lambda *_: zeros)


def _rms(x, g, eps):
    return x * lax.rsqrt(jnp.mean(x * x, axis=-1, keepdims=True) + eps) * g


def _split_bf16(x):
    hi = x.astype(BF16)
    lo = (x - hi.astype(F32)).astype(BF16)
    return hi, lo


def _dot(a, b):
    return jnp.dot(a, b, preferred_element_type=F32)


def _dot_nt(a, b):
    return lax.dot_general(a, b, (((1,), (1,)), ((), ())), preferred_element_type=F32)


def _memkv_kernel(mem_ref, g_ref, w_ref, kg_ref, mk_ref, mv_ref, *, n_heads, hd):
    y = _rms(mem_ref[...], g_ref[...], NORM_EPS)
    kv = _dot(y.astype(BF16), w_ref[...])
    width = n_heads * hd
    for h in range(n_heads):
        sl = slice(h * hd, (h + 1) * hd)
        mk_ref[:, sl] = _rms(kv[:, sl], kg_ref[...], QK_EPS).astype(BF16)
    mv_ref[...] = kv[:, width:].astype(BF16)


def _memkv(mem2, g, w_kv, kg, n_heads, hd):
    rows, d = mem2.shape
    width = n_heads * hd
    tm = min(rows, 512)
    return pl.pallas_call(
        functools.partial(_memkv_kernel, n_heads=n_heads, hd=hd),
        grid=(rows // tm,),
        in_specs=[pl.BlockSpec((tm, d), lambda i: (i, 0)),
                  _const_spec((1, d)), _const_spec((d, 2 * width)), _const_spec((1, hd))],
        out_specs=[pl.BlockSpec((tm, width), lambda i: (i, 0)),
                   pl.BlockSpec((tm, width), lambda i: (i, 0))],
        out_shape=[jax.ShapeDtypeStruct((rows, width), BF16)] * 2,
        compiler_params=_params(1),
        name="memkv",
    )(mem2, g, w_kv, kg)


def _qkv_kernel(x_ref, g1_ref, wqkv_ref, wf_ref, bf_ref, gq_ref, gk_ref, bd_ref, tri_ref,
                q_ref, k_ref, v_ref, cumt_ref, carry_ref,
                *, tiles_per_batch, width, hd, fh_pad):
    i = pl.program_id(0)

    @pl.when(i % tiles_per_batch == 0)
    def _():
        carry_ref[...] = jnp.zeros_like(carry_ref)

    hb = _rms(x_ref[...], g1_ref[...], NORM_EPS).astype(BF16)
    qkv = _dot(hb, wqkv_ref[...])

    def head_norm(z, g):
        sq = (z * z).astype(BF16)
        parts = []
        for c in range(width // MXU_DIM):
            sl = slice(c * MXU_DIM, (c + 1) * MXU_DIM)
            parts.append(_dot(sq[:, sl], bd_ref[...]))
        ms = jnp.concatenate(parts, axis=-1) * (1.0 / hd)
        return z * lax.rsqrt(ms + QK_EPS) * g

    q = head_norm(qkv[:, :width], gq_ref[...]) * (hd ** -0.5 * LOG2_E)
    k = head_norm(qkv[:, width:2 * width], gk_ref[...])
    q_ref[...] = q.astype(BF16)
    k_ref[...] = k.astype(BF16)
    v_ref[...] = qkv[:, 2 * width:].astype(BF16)

    z = _dot(hb, wf_ref[...]) + bf_ref[...]
    log_f = jnp.minimum(z, 0.0) - jnp.log(1.0 + jnp.exp(-jnp.abs(z)))
    hi, lo = _split_bf16(log_f)
    cum = _dot(tri_ref[...], hi) + _dot(tri_ref[...], lo) + carry_ref[0:1, :]
    tm = cum.shape[0]
    carry_ref[...] = jnp.broadcast_to(cum[tm - 1:tm, :], carry_ref.shape)
    cumt_ref[...] = (cum * LOG2_E).T[:fh_pad, :]


def _qkv(x2, g1, wqkv, wf, bf, gq_t, gk_t, seq, width, hd, fh_pad):
    t, d = x2.shape
    tm = min(TM_QKV, seq)
    bd = (jnp.arange(MXU_DIM)[:, None] // hd == jnp.arange(MXU_DIM)[None, :] // hd).astype(BF16)
    tri = (jnp.arange(tm)[:, None] >= jnp.arange(tm)[None, :]).astype(BF16)
    row = lambda i: (i, 0)
    return pl.pallas_call(
        functools.partial(_qkv_kernel, tiles_per_batch=seq // tm, width=width, hd=hd, fh_pad=fh_pad),
        grid=(t // tm,),
        in_specs=[pl.BlockSpec((tm, d), row), _const_spec((1, d)), _const_spec((d, 3 * width)),
                  _const_spec((d, LANES)), _const_spec((1, LANES)), _const_spec((1, width)),
                  _const_spec((1, width)), _const_spec((MXU_DIM, MXU_DIM)), _const_spec((tm, tm))],
        out_specs=[pl.BlockSpec((tm, width), row)] * 3 + [pl.BlockSpec((fh_pad, tm), lambda i: (0, i))],
        out_shape=[jax.ShapeDtypeStruct((t, width), BF16)] * 3 + [jax.ShapeDtypeStruct((fh_pad, t), F32)],
        scratch_shapes=[pltpu.VMEM((SUBLANES, LANES), F32)],
        compiler_params=_params(1),
        name="qkv",
    )(x2, g1, wqkv, wf, bf, gq_t, gk_t, bd, tri)


def _fox_kernel(q_ref, k_ref, v_ref, cumt_ref, o_ref, s_ref, *, tq, hd):
    hp = pl.program_id(1)
    seq = q_ref.shape[0]
    nh = LANES // hd
    lane = lax.broadcasted_iota(I32, (1, LANES), 1)
    on_or_below_diag = lax.broadcasted_iota(I32, (tq, tq), 1) <= lax.broadcasted_iota(I32, (tq, tq), 0)
    heads = [(hp * nh + hh, (lane >= hh * hd) & (lane < (hh + 1) * hd)) for hh in range(nh)]

    def scores(i, j, slot):
        q = q_ref[i * tq:(i + 1) * tq, :]
        kj = k_ref[j * tq:(j + 1) * tq, :]
        for hh, (h, in_head) in enumerate(heads):
            qh = jnp.where(in_head, q, jnp.zeros_like(q))
            s_ref[slot, hh] = _dot_nt(qh, kj) - cumt_ref[pl.ds(h, 1), j * tq:(j + 1) * tq]

    def consume(j, slot, carry, masked):
        vj = v_ref[j * tq:(j + 1) * tq, :]
        out = []
        for hh, ((_, in_head), (m, acc)) in enumerate(zip(heads, carry)):
            vh = jnp.where(in_head, vj, jnp.ones_like(vj))
            s = s_ref[slot, hh]
            if masked:
                s = jnp.where(on_or_below_diag, s, NEG_BIG)
            m_new = jnp.maximum(m, jnp.max(s, axis=-1, keepdims=True))
            p = jnp.exp2(s - m_new)
            out.append((m_new, jnp.exp2(m - m_new) * acc + _dot(p.astype(BF16), vh)))
        return tuple(out)

    tiles = [(i, j) for i in range(seq // tq) for j in range(i + 1)]
    scores(*tiles[0], 0)
    carry = None
    for n, (i, j) in enumerate(tiles):
        if n + 1 < len(tiles):
            scores(*tiles[n + 1], (n + 1) % 2)
        if j == 0:
            carry = tuple((jnp.full((tq, 1), NEG_BIG, F32), jnp.zeros((tq, LANES), F32)) for _ in heads)
        carry = consume(j, n % 2, carry, j == i)
        if j == i:
            o = None
            for hh in range(nh - 1, -1, -1):
                acc = carry[hh][1]
                other = ((hh + 1) % nh) * hd
                val = acc / acc[:, other:other + 1]
                o = val if o is None else jnp.where(lane < (hh + 1) * hd, val, o)
            o_ref[i * tq:(i + 1) * tq, :] = o.astype(BF16)


def _fox(q, k, v, cumt, batch, seq, width, hd, fh_pad):
    t = q.shape[0]
    tq = min(TQ_FOX, seq)
    assert seq % tq == 0
    seqmap = lambda b, hp: (b, hp)
    return pl.pallas_call(
        functools.partial(_fox_kernel, tq=tq, hd=hd),
        grid=(batch, width // LANES),
        in_specs=[pl.BlockSpec((seq, LANES), seqmap), pl.BlockSpec((seq, LANES), seqmap),
                  pl.BlockSpec((seq, LANES), seqmap), pl.BlockSpec((fh_pad, seq), lambda b, hp: (0, b))],
        out_specs=pl.BlockSpec((seq, LANES), seqmap),
        out_shape=jax.ShapeDtypeStruct((t, width), BF16),
        scratch_shapes=[pltpu.VMEM((2, LANES // hd, tq, tq), F32)],
        compiler_params=_params(2),
        name="fox",
    )(q, k, v, cumt)


def _merge_kernel(x_ref, yf_ref, mk_ref, mv_ref, g1_ref, wcat_ref, bg_ref, cw_ref, xqg_ref,
                  wbr_ref, wout_ref, g2_ref, wrh_ref, wrl_ref, br_ref, ltri_ref, utri_ref,
                  x1_ref, h2_ref, lpos_ref, lpost_ref, tw_ref, tcnt_ref, tbase_ref,
                  prev_ref, halo_ref, carry_ref, hi_ref, lo_ref, *, tiles_per_batch, width, xa_heads, xa_hd):
    i = pl.program_id(0)
    nt = pl.num_programs(0) - 1
    tm, d = x_ref.shape

    @pl.when(i == 0)
    def _():
        prev_ref[...] = jnp.zeros_like(prev_ref)
        halo_ref[...] = jnp.zeros_like(halo_ref)
        carry_ref[...] = jnp.zeros_like(carry_ref)
        hi_ref[...] = jnp.zeros_like(hi_ref)
        lo_ref[...] = jnp.zeros_like(lo_ref)

    routed = _route(hi_ref[...], lo_ref[...], wrh_ref, wrl_ref, br_ref)

    halo = jnp.where(i < nt, jnp.where(i % tiles_per_batch == 0, 0.0, prev_ref[...]), halo_ref[...])
    halo_ref[...] = halo

    x = x_ref[...]
    hb = _rms(x, g1_ref[...], NORM_EPS).astype(BF16)
    pc = _dot(hb, wcat_ref[...])

    uc = pc[:, :width] * pc[:, width:2 * width]
    row = lax.broadcasted_iota(I32, (tm, 1), 0)
    p1 = halo[SUBLANES - 1:SUBLANES, :]
    p2 = halo[SUBLANES - 2:SUBLANES - 1, :]
    m1 = jnp.where(row == 0, p1, pltpu.roll(uc, 1, 0))
    m2 = jnp.where(row == 0, p2, jnp.where(row == 1, p1, pltpu.roll(uc, 2, 0)))
    y_conv = pc[:, 2 * width:3 * width] * (cw_ref[0:1, :] * m2 + cw_ref[1:2, :] * m1 + cw_ref[2:3, :] * uc)
    prev_ref[...] = uc[tm - SUBLANES:tm, :]

    ys = []
    for h in range(xa_heads):
        sl = slice(3 * width + h * xa_hd, 3 * width + (h + 1) * xa_hd)
        ml = slice(h * xa_hd, (h + 1) * xa_hd)
        qh = _rms(pc[:, sl], xqg_ref[...], QK_EPS) * (xa_hd ** -0.5)
        s = _dot_nt(qh.astype(BF16), mk_ref[:, ml])
        p = jnp.exp(s - jnp.max(s, axis=-1, keepdims=True))
        ys.append(_dot(p.astype(BF16), mv_ref[:, ml]) / jnp.sum(p, axis=-1, keepdims=True))
    y_mem = jnp.concatenate(ys, axis=-1)

    _place(i, *routed, ltri_ref, utri_ref, lpos_ref, lpost_ref, tw_ref, tcnt_ref, tbase_ref, carry_ref)

    merged = jnp.zeros((tm, d), F32)
    for n, yb in enumerate((yf_ref[...], y_conv.astype(BF16), y_mem.astype(BF16))):
        gate = jax.nn.sigmoid(pc[:, 4 * width + n * d:4 * width + (n + 1) * d] + bg_ref[n:n + 1, :])
        merged = merged + gate * _dot(yb, wbr_ref[n])
    x1 = x + _dot(merged.astype(BF16), wout_ref[...])
    x1_ref[...] = x1
    hi, lo = _split_bf16(_rms(x1, g2_ref[...], NORM_EPS))
    h2_ref[...] = hi
    hi_ref[...] = hi
    lo_ref[...] = lo


def _route(hi, lo, wrh_ref, wrl_ref, br_ref):
    tm = hi.shape[0]
    logits = _dot(hi, wrh_ref[...]) + _dot(hi, wrl_ref[...]) + _dot(lo, wrh_ref[...]) + br_ref[...]
    lane = lax.broadcasted_iota(I32, (tm, LANES), 1)
    vals, idxs = [], []
    onehot = jnp.zeros((tm, LANES), F32)
    for _ in range(TOP_K):
        m = jnp.max(logits, axis=-1, keepdims=True)
        idx = jnp.min(jnp.where(logits == m, lane, LANES), axis=-1, keepdims=True)
        sel = lane == idx
        logits = jnp.where(sel, -jnp.inf, logits)
        onehot = onehot + sel.astype(F32)
        vals.append(m)
        idxs.append(idx)
    es = [jnp.exp(v - vals[0]) for v in vals]
    den = es[0]
    for e in es[1:]:
        den = den + e
    return onehot, idxs, [e / den for e in es]


def _place(i, onehot, idxs, tws, ltri_ref, utri_ref, lpos_ref, lpost_ref, tw_ref, tcnt_ref, tbase_ref,
           carry_ref):
    tm = onehot.shape[0]
    lane = lax.broadcasted_iota(I32, (tm, LANES), 1)
    cnt = jnp.sum(onehot, axis=0, keepdims=True)
    span = jnp.floor((cnt + (SUBLANES - 1)) * (1.0 / SUBLANES)) * SUBLANES
    s_hi, s_lo = _split_bf16(jnp.broadcast_to(span, (SUBLANES, LANES)))
    seg_start = (_dot(s_hi, utri_ref[...]) + _dot(s_lo, utri_ref[...]))[0:1, :]
    where_to = _dot(ltri_ref[...], onehot.astype(BF16)) + seg_start
    tcnt_ref[0] = jnp.broadcast_to(span, (SUBLANES, LANES))
    tbase_ref[0] = carry_ref[...]
    carry_ref[...] = jnp.where(i > 0, carry_ref[...] + span, 0.0)

    lpos_o = jnp.zeros((tm, LANES), F32)
    tw_o = jnp.zeros((tm, LANES), F32)
    for kk in range(TOP_K):
        lpos = jnp.sum(jnp.where(lane == idxs[kk], where_to, 0.0), axis=-1, keepdims=True)
        lpos_o = jnp.where(lane == kk, lpos, lpos_o)
        tw_o = jnp.where(lane == kk, tws[kk], tw_o)
    lpos_ref[...] = lpos_o.astype(I32)
    lpost_ref[0] = lpos_o.T[:SUBLANES, :].astype(I32)
    tw_ref[...] = tw_o


def _merge(x2, y_fox, mk, mv, g1, wcat, bg, cw, xqg, wbr, wout, g2, wrh, wrl, br,
           seq, n_mem, width, xa_heads, xa_hd):
    t, d = x2.shape
    tm = min(TM_WINDOW, seq)
    nt = t // tm
    tpb = seq // tm
    ltri = (jnp.arange(tm)[:, None] > jnp.arange(tm)[None, :]).astype(BF16)
    utri = (jnp.arange(LANES)[:, None] < jnp.arange(LANES)[None, :]).astype(BF16)
    mix = lambda i: jnp.minimum(i, nt - 1)
    rte = lambda i: jnp.maximum(i - 1, 0)
    row = lambda i: (mix(i), 0)
    rrow = lambda i: (rte(i), 0)
    tile3 = lambda i: (rte(i), 0, 0)
    mem_map = lambda i: (mix(i) // tpb, 0)
    return pl.pallas_call(
        functools.partial(_merge_kernel, tiles_per_batch=tpb, width=width, xa_heads=xa_heads, xa_hd=xa_hd),
        grid=(nt + 1,),
        in_specs=[pl.BlockSpec((tm, d), row), pl.BlockSpec((tm, width), row),
                  pl.BlockSpec((n_mem, width), mem_map), pl.BlockSpec((n_mem, width), mem_map),
                  _const_spec((1, d)), _const_spec(wcat.shape), _const_spec(bg.shape), _const_spec(cw.shape),
                  _const_spec((1, xa_hd)), _const_spec(wbr.shape), _const_spec((d, d)), _const_spec((1, d)),
                  _const_spec((d, LANES)), _const_spec((d, LANES)), _const_spec((1, LANES)),
                  _const_spec((tm, tm)), _const_spec((LANES, LANES))],
        out_specs=[pl.BlockSpec((tm, d), row), pl.BlockSpec((tm, d), row),
                   pl.BlockSpec((tm, LANES), rrow), pl.BlockSpec((1, SUBLANES, tm), tile3),
                   pl.BlockSpec((tm, LANES), rrow),
                   pl.BlockSpec((1, SUBLANES, LANES), tile3), pl.BlockSpec((1, SUBLANES, LANES), tile3)],
        out_shape=[jax.ShapeDtypeStruct((t, d), F32), jax.ShapeDtypeStruct((t, d), BF16),
                   jax.ShapeDtypeStruct((t, LANES), I32), jax.ShapeDtypeStruct((nt, SUBLANES, tm), I32),
                   jax.ShapeDtypeStruct((t, LANES), F32),
                   jax.ShapeDtypeStruct((nt, SUBLANES, LANES), F32),
                   jax.ShapeDtypeStruct((nt, SUBLANES, LANES), F32)],
        scratch_shapes=[pltpu.VMEM((SUBLANES, width), F32), pltpu.VMEM((SUBLANES, width), F32),
                        pltpu.VMEM((SUBLANES, LANES), F32), pltpu.VMEM((tm, d), BF16), pltpu.VMEM((tm, d), BF16)],
        compiler_params=_params(1),
        name="merge",
    )(x2, y_fox, mk, mv, g1, wcat, bg, cw, xqg, wbr, wout, g2, wrh, wrl, br, ltri, utri)


def _block_rows(tm, n_exp):
    return -(-(TOP_K * tm + (SUBLANES - 1) * n_exp) // MXU_DIM) * MXU_DIM


def _chunked_copies(src_ref, src0, dst_ref, dst0, n, sem, max_rows, wait, fixed_src=False):
    for b in range(max_rows.bit_length() - 1, SUBLANES.bit_length() - 2, -1):
        size = 1 << b
        off = (n >> (b + 1)) << (b + 1)

        @pl.when((n >> b) & 1 == 1)
        def _():
            src = 0 if fixed_src else pl.multiple_of(src0 + off, SUBLANES)
            cp = pltpu.make_async_copy(src_ref.at[pl.ds(src, size)],
                                       dst_ref.at[pl.ds(pl.multiple_of(dst0 + off, SUBLANES), size)], sem)
            if wait:
                cp.wait()
            else:
                cp.start()


def _chunk_classes(tm):
    return [1 << b for b in range(SUBLANES.bit_length() - 1, tm.bit_length())]


def _chunk_table(seg_len, seg_src, seg_dst, tm, n_exp):
    sizes = jnp.asarray(_chunk_classes(tm), I32)
    bit = (seg_len[:, :, None] // sizes) % 2
    off = seg_len[:, :, None] // (2 * sizes) * (2 * sizes)
    slot = jnp.cumsum(bit, axis=1) - 1
    hit = (slot[:, :, None, :] == jnp.arange(n_exp, dtype=I32)[None, None, :, None]) & (bit[:, :, None, :] == 1)
    pack = lambda start: jnp.sum(jnp.where(hit, (start[:, :, None] + off)[:, :, None, :], 0), axis=1)
    nw = seg_len.shape[0]
    cols = [jnp.swapaxes(pack(seg_src), 1, 2).reshape(nw, -1), jnp.swapaxes(pack(seg_dst), 1, 2).reshape(nw, -1),
            jnp.sum(bit, axis=1)]
    tbl = jnp.concatenate(cols, axis=1).astype(I32)
    assert tbl.shape[1] <= CHUNK_TABLE_LEN
    return jnp.pad(tbl, ((0, 0), (0, CHUNK_TABLE_LEN - tbl.shape[1]))).reshape(-1)


def _table_copies(tbl_ref, block_ref, sorted_ref, sem, tm, n_exp, to_sorted, wait):
    sizes = _chunk_classes(tm)
    for c, size in enumerate(sizes):
        def body(i, carry, c=c, size=size):
            blk = block_ref.at[pl.ds(pl.multiple_of(tbl_ref[c * n_exp + i], SUBLANES), size)]
            srt = sorted_ref.at[pl.ds(pl.multiple_of(tbl_ref[(len(sizes) + c) * n_exp + i], SUBLANES), size)]
            cp = pltpu.make_async_copy(blk, srt, sem) if to_sorted else pltpu.make_async_copy(srt, blk, sem)
            if wait:
                cp.wait()
            else:
                cp.start()
            return carry
        lax.fori_loop(0, tbl_ref[2 * len(sizes) * n_exp + c], body, 0)


def _dispatch_kernel(pad_start_ref, pad_cnt_ref, n_used_ref, tbl_ref, tbl_prev_ref, lpost_ref, h2_ref, xs_ref,
                     buf_ref, zero_ref, sems, zsem, *, n_exp, n_tiles):
    w = pl.program_id(0)
    nw = pl.num_programs(0)
    tm = h2_ref.shape[0]
    rows = buf_ref.shape[1]
    slot = w % 2

    def clear(wait):
        def per_expert(e, c):
            _chunked_copies(zero_ref, 0, xs_ref, pad_start_ref[e], pad_cnt_ref[e], zsem, ZERO_ROWS, wait,
                            fixed_src=True)
            return c

        def per_tile(i, c):
            for part in range(TM_EXPERT // ZERO_ROWS):
                cp = pltpu.make_async_copy(
                    zero_ref, xs_ref.at[pl.ds(i * TM_EXPERT + part * ZERO_ROWS, ZERO_ROWS)], zsem)
                if wait:
                    cp.wait()
                else:
                    cp.start()
            return c

        lax.fori_loop(0, n_exp, per_expert, 0)
        lax.fori_loop(n_used_ref[0], n_tiles, per_tile, 0)

    @pl.when(w == 0)
    def _():
        zero_ref[...] = jnp.zeros_like(zero_ref)
        clear(False)

    r_iota = lax.broadcasted_iota(I32, (rows, tm), 0)
    sel = r_iota == lpost_ref[0, 0:1, :]
    for kk in range(1, TOP_K):
        sel = sel | (r_iota == lpost_ref[0, kk:kk + 1, :])
    perm = jnp.where(sel, 1.0, 0.0).astype(BF16)

    buf_ref[slot] = _dot(perm, h2_ref[...])
    _table_copies(tbl_ref, buf_ref.at[slot], xs_ref, sems.at[slot], tm, n_exp, True, False)

    @pl.when(w >= 1)
    def _():
        _table_copies(tbl_prev_ref, buf_ref.at[1 - slot], xs_ref, sems.at[1 - slot], tm, n_exp, True, True)

    @pl.when(w == nw - 1)
    def _():
        _table_copies(tbl_ref, buf_ref.at[slot], xs_ref, sems.at[slot], tm, n_exp, True, True)
        clear(True)


def _dispatch(pad_start, pad_cnt, n_used, chunk_tbl, lpost, h2, n_tiles, n_exp):
    t, d = h2.shape
    tm = lpost.shape[2]
    smem_blk = lambda shift: pl.BlockSpec((CHUNK_TABLE_LEN,), lambda i, *_: (jnp.maximum(i + shift, 0),),
                                          memory_space=pltpu.SMEM)
    grid_spec = pltpu.PrefetchScalarGridSpec(
        num_scalar_prefetch=3,
        grid=(t // tm,),
        in_specs=[smem_blk(0), smem_blk(-1),
                  pl.BlockSpec((1, SUBLANES, tm), lambda i, *_: (i, 0, 0)),
                  pl.BlockSpec((tm, d), lambda i, *_: (i, 0))],
        out_specs=pl.BlockSpec(memory_space=pl.ANY),
        scratch_shapes=[pltpu.VMEM((2, _block_rows(tm, n_exp), d), F32), pltpu.VMEM((ZERO_ROWS, d), F32),
                        pltpu.SemaphoreType.DMA((2,)), pltpu.SemaphoreType.DMA(())],
    )
    return pl.pallas_call(
        functools.partial(_dispatch_kernel, n_exp=n_exp, n_tiles=n_tiles),
        grid_spec=grid_spec,
        out_shape=jax.ShapeDtypeStruct((n_tiles * TM_EXPERT, d), F32),
        compiler_params=_params(1),
        name="dispatch",
    )(pad_start, pad_cnt, n_used, chunk_tbl, chunk_tbl, lpost, h2)


def _expert_kernel(tile_exp_ref, n_used_ref, xs_ref, wup_ref, bup_ref, wdn_ref, bdn_ref, ys_ref,
                   wup_bf_ref, wdn_bf_ref, *, d_ff):
    i = pl.program_id(0)

    @pl.when((i == 0) | (tile_exp_ref[i] != tile_exp_ref[jnp.maximum(i - 1, 0)]))
    def _():
        wup_bf_ref[...] = wup_ref[0].astype(BF16)
        wdn_bf_ref[...] = wdn_ref[0].astype(BF16)

    @pl.when(i < n_used_ref[0])
    def _():
        h = _dot(xs_ref[...].astype(BF16), wup_bf_ref[...]) + bup_ref[0]
        glu = jnp.minimum(h[:, :d_ff], SWIGLU_LIMIT)
        lin = jnp.clip(h[:, d_ff:], -SWIGLU_LIMIT, SWIGLU_LIMIT)
        a = glu * jax.nn.sigmoid(SWIGLU_ALPHA * glu) * (lin + 1.0)
        ys_ref[...] = _dot(a.astype(BF16), wdn_bf_ref[...]) + bdn_ref[0]

    @pl.when(i >= n_used_ref[0])
    def _():
        ys_ref[...] = jnp.zeros_like(ys_ref)


def _experts(tile_exp, n_used, xs, wup, bup, wdn, bdn):
    n_rows, d = xs.shape
    d_ff = wdn.shape[1]
    tm = TM_EXPERT
    row = lambda i, te, nu: (i, 0)
    used_row = lambda i, te, nu: (jnp.minimum(i, nu[0] - 1), 0)
    exp3 = lambda i, te, nu: (te[i], 0, 0)
    grid_spec = pltpu.PrefetchScalarGridSpec(
        num_scalar_prefetch=2,
        grid=(n_rows // tm,),
        in_specs=[pl.BlockSpec((tm, d), used_row),
                  pl.BlockSpec((1, d, 2 * d_ff), exp3), pl.BlockSpec((1, 1, 2 * d_ff), exp3),
                  pl.BlockSpec((1, d_ff, d), exp3), pl.BlockSpec((1, 1, d), exp3)],
        out_specs=pl.BlockSpec((tm, d), row),
        scratch_shapes=[pltpu.VMEM((d, 2 * d_ff), BF16), pltpu.VMEM((d_ff, d), BF16)],
    )
    return pl.pallas_call(
        functools.partial(_expert_kernel, d_ff=d_ff),
        grid_spec=grid_spec,
        out_shape=jax.ShapeDtypeStruct((n_rows, d), F32),
        compiler_params=_params(1),
        name="experts",
    )(tile_exp, n_used, xs, wup, bup, wdn, bdn)


def _combine_kernel(tbl_ref, tbl_next_ref, lpos_ref, tw_ref, x1_ref, ys_ref, o_ref, buf_ref, sems, *, n_exp):
    w = pl.program_id(0)
    nw = pl.num_programs(0)
    tm = x1_ref.shape[0]
    rows = buf_ref.shape[1]
    slot = w % 2

    @pl.when(w == 0)
    def _():
        buf_ref[...] = jnp.zeros_like(buf_ref)
        _table_copies(tbl_ref, buf_ref.at[0], ys_ref, sems.at[0], tm, n_exp, False, False)

    @pl.when(w + 1 < nw)
    def _():
        _table_copies(tbl_next_ref, buf_ref.at[1 - slot], ys_ref, sems.at[1 - slot], tm, n_exp, False, False)

    _table_copies(tbl_ref, buf_ref.at[slot], ys_ref, sems.at[slot], tm, n_exp, False, True)

    c_iota = lax.broadcasted_iota(I32, (tm, rows), 1)
    lpos = lpos_ref[...]
    tw = tw_ref[...]
    wperm = jnp.zeros((tm, rows), F32)
    for kk in range(TOP_K):
        wperm = jnp.where(c_iota == lpos[:, kk:kk + 1], tw[:, kk:kk + 1], wperm)
    o_ref[...] = x1_ref[...] + _dot(wperm.astype(BF16), buf_ref[slot].astype(BF16))


def _combine(chunk_tbl, lpos, tw, x1, ys, n_exp):
    t, d = x1.shape
    tm = min(TM_WINDOW, t)
    nw = t // tm
    row = lambda i: (i, 0)
    smem_blk = lambda shift: pl.BlockSpec((CHUNK_TABLE_LEN,), lambda i: (jnp.minimum(i + shift, nw - 1),),
                                          memory_space=pltpu.SMEM)
    return pl.pallas_call(
        functools.partial(_combine_kernel, n_exp=n_exp),
        grid=(nw,),
        in_specs=[smem_blk(0), smem_blk(1),
                  pl.BlockSpec((tm, LANES), row), pl.BlockSpec((tm, LANES), row), pl.BlockSpec((tm, d), row),
                  pl.BlockSpec(memory_space=pl.ANY)],
        out_specs=pl.BlockSpec((tm, d), row),
        out_shape=jax.ShapeDtypeStruct((t, d), F32),
        scratch_shapes=[pltpu.VMEM((2, _block_rows(tm, n_exp), d), F32), pltpu.SemaphoreType.DMA((2,))],
        compiler_params=_params(1),
        name="combine",
    )(chunk_tbl, chunk_tbl, lpos, tw, x1, ys)


def _pad_lanes(a, value=0.0):
    return jnp.pad(a, ((0, 0), (0, LANES - a.shape[-1])), constant_values=value)


def _layer(x, mem, norm1_g, w_in, b_forget, b_gate, fox_q_g, fox_k_g, conv_w, mem_norm_g, w_mem_kv,
           xa_q_g, xa_k_g, w_branch, w_out, norm2_g, w_router, b_router, w_up, b_up, w_down, b_down):
    batch, seq, d = x.shape
    n_mem = mem.shape[1]
    n_branch, width, _ = w_branch.shape
    n_fh = b_forget.shape[0]
    hd = fox_q_g.shape[0]
    xa_hd = xa_q_g.shape[0]
    xa_heads = width // xa_hd
    n_exp = w_router.shape[1]
    assert n_fh * hd == width and n_branch == 3 and conv_w.shape[0] == 3
    assert LANES % hd == 0 and width % MXU_DIM == 0 and n_exp <= LANES and n_fh <= LANES
    fh_pad = -(-n_fh // SUBLANES) * SUBLANES
    t = batch * seq
    x2 = x.reshape(t, d)
    row1 = lambda a: a.reshape(1, -1).astype(F32)

    o_f = 3 * width
    o_conv = o_f + n_fh
    wqkv = w_in[:, :o_f].astype(BF16)
    wf = _pad_lanes(w_in[:, o_f:o_conv]).astype(BF16)
    wcat = w_in[:, o_conv:].astype(BF16)

    mk, mv = _memkv(mem.reshape(batch * n_mem, d), row1(mem_norm_g), w_mem_kv.astype(BF16), row1(xa_k_g),
                    xa_heads, xa_hd)
    q, k, v, cumt = _qkv(x2, row1(norm1_g), wqkv, wf, _pad_lanes(row1(b_forget)),
                         row1(jnp.tile(fox_q_g, n_fh)), row1(jnp.tile(fox_k_g, n_fh)),
                         seq, width, hd, fh_pad)
    y_fox = _fox(q, k, v, cumt, batch, seq, width, hd, fh_pad)

    wr = _pad_lanes(w_router.astype(F32))
    wr_hi = wr.astype(BF16)
    wr_lo = (wr - wr_hi.astype(F32)).astype(BF16)
    x1, h2, lpos, lpost, tw, tcnt, tbase = _merge(
        x2, y_fox, mk, mv, row1(norm1_g), wcat, b_gate.astype(F32), conv_w.astype(F32), row1(xa_q_g),
        w_branch.astype(BF16), w_out.astype(BF16), row1(norm2_g), wr_hi, wr_lo,
        _pad_lanes(row1(b_router), NEG_BIG), seq, n_mem, width, xa_heads, xa_hd)

    seg_len = tcnt[:, 0, :n_exp].astype(I32)
    seg_base = tbase[:, 0, :n_exp].astype(I32)
    counts = seg_base[-1] + seg_len[-1]
    tiles_per = (counts + TM_EXPERT - 1) // TM_EXPERT
    tile_end = jnp.cumsum(tiles_per)
    group_start = (tile_end - tiles_per) * TM_EXPERT
    seg_dst = group_start[None, :] + seg_base
    seg_src = jnp.cumsum(seg_len, axis=1) - seg_len
    n_tiles = -(-(t * TOP_K + (SUBLANES - 1) * seg_len.size) // TM_EXPERT) + n_exp
    n_used = tile_end[-1:].astype(I32)
    tile_ids = jnp.arange(n_tiles, dtype=I32)
    tile_exp = jnp.minimum(jnp.sum(tile_ids[:, None] >= tile_end[None, :], axis=1), n_exp - 1).astype(I32)
    tile_exp = jnp.where(tile_ids < n_used[0], tile_exp, tile_exp[jnp.maximum(n_used[0] - 1, 0)])
    pad_start = (group_start + counts).astype(I32)
    pad_cnt = (tiles_per * TM_EXPERT - counts).astype(I32)
    chunk_tbl = _chunk_table(seg_len, seg_src.astype(I32), seg_dst.astype(I32), lpost.shape[2], n_exp)

    xs = _dispatch(pad_start, pad_cnt, n_used, chunk_tbl, lpost, h2, n_tiles, n_exp)
    ys = _experts(tile_exp, n_used, xs, w_up.astype(F32), b_up[:, None, :].astype(F32),
                  w_down.astype(F32), b_down[:, None, :].astype(F32))
    out = _combine(chunk_tbl, lpos, tw, x1, ys, n_exp)
    return out.reshape(batch, seq, d)


def kernel(x, mem, norm1_g, w_in, b_forget, b_gate, fox_q_g, fox_k_g, conv_w, mem_norm_g, w_mem_kv, xa_q_g,
           xa_k_g, w_branch, w_out, norm2_g, w_router, b_router, w_up, b_up, w_down, b_down):
    for l in range(norm1_g.shape[0]):
        x = _layer(x, mem, norm1_g[l], w_in[l], b_forget[l], b_gate[l], fox_q_g[l], fox_k_g[l], conv_w[l],
                   mem_norm_g[l], w_mem_kv[l], xa_q_g[l], xa_k_g[l], w_branch[l], w_out[l], norm2_g[l],
                   w_router[l], b_router[l], w_up[l], b_up[l], w_down[l], b_down[l])
    return x
```

```python
import functools

import jax
import jax.numpy as jnp
from jax import lax
from jax.experimental import pallas as pl
from jax.experimental.pallas import tpu as pltpu

F32 = jnp.float32
BF16 = jnp.bfloat16
I32 = jnp.int32

NORM_EPS = 1e-5
QK_EPS = 1e-6
TOP_K = 4
SWIGLU_ALPHA = 1.702
SWIGLU_LIMIT = 7.0
NEG_BIG = -1e30
LOG2_E = 1.4426950408889634

LANES = 128
SUBLANES = 8
MXU_DIM = 256
VMEM_LIMIT_BYTES = 56 * 1024 * 1024

TM_QKV = 512
TQ_FOX = 512
TM_WINDOW = 256
TM_EXPERT = 512
ZERO_ROWS = 256
CHUNK_TABLE_LEN = 512


def _params(n_axes):
    return pltpu.CompilerParams(dimension_semantics=("arbitrary",) * n_axes,
                                vmem_limit_bytes=VMEM_LIMIT_BYTES)


def _const_spec(shape):
    zeros = (0,) * len(shape)
    return pl.BlockSpec(shape, lambda *_: zeros)


def _rms(x, g, eps):
    return x * lax.rsqrt(jnp.mean(x * x, axis=-1, keepdims=True) + eps) * g


def _split_bf16(x):
    hi = x.astype(BF16)
    lo = (x - hi.astype(F32)).astype(BF16)
    return hi, lo


def _dot(a, b):
    return jnp.dot(a, b, preferred_element_type=F32)


def _dot_nt(a, b):
    return lax.dot_general(a, b, (((1,), (1,)), ((), ())), preferred_element_type=F32)


def _memkv_kernel(mem_ref, g_ref, w_ref, kg_ref, mk_ref, mv_ref, *, n_heads, hd):
    y = _rms(mem_ref[...], g_ref[...], NORM_EPS)
    kv = _dot(y.astype(BF16), w_ref[...])
    width = n_heads * hd
    for h in range(n_heads):
        sl = slice(h * hd, (h + 1) * hd)
        mk_ref[:, sl] = _rms(kv[:, sl], kg_ref[...], QK_EPS).astype(BF16)
    mv_ref[...] = kv[:, width:].astype(BF16)


def _memkv(mem2, g, w_kv, kg, n_heads, hd):
    rows, d = mem2.shape
    width = n_heads * hd
    tm = min(rows, 512)
    return pl.pallas_call(
        functools.partial(_memkv_kernel, n_heads=n_heads, hd=hd),
        grid=(rows // tm,),
        in_specs=[pl.BlockSpec((tm, d), lambda i: (i, 0)),
                  _const_spec((1, d)), _const_spec((d, 2 * width)), _const_spec((1, hd))],
        out_specs=[pl.BlockSpec((tm, width), lambda i: (i, 0)),
                   pl.BlockSpec((tm, width), lambda i: (i, 0))],
        out_shape=[jax.ShapeDtypeStruct((rows, width), BF16)] * 2,
        compiler_params=_params(1),
        name="memkv",
    )(mem2, g, w_kv, kg)


def _qkv_kernel(x_ref, g1_ref, wqkv_ref, wf_ref, bf_ref, gq_ref, gk_ref, bd_ref, tri_ref,
                q_ref, k_ref, v_ref, cumt_ref, carry_ref,
                *, tiles_per_batch, width, hd, fh_pad):
    i = pl.program_id(0)

    @pl.when(i % tiles_per_batch == 0)
    def _():
        carry_ref[...] = jnp.zeros_like(carry_ref)

    hb = _rms(x_ref[...], g1_ref[...], NORM_EPS).astype(BF16)
    qkv = _dot(hb, wqkv_ref[...])

    def head_norm(z, g):
        sq = (z * z).astype(BF16)
        parts = []
        for c in range(width // MXU_DIM):
            sl = slice(c * MXU_DIM, (c + 1) * MXU_DIM)
            parts.append(_dot(sq[:, sl], bd_ref[...]))
        ms = jnp.concatenate(parts, axis=-1) * (1.0 / hd)
        return z * lax.rsqrt(ms + QK_EPS) * g

    q = head_norm(qkv[:, :width], gq_ref[...]) * (hd ** -0.5 * LOG2_E)
    k = head_norm(qkv[:, width:2 * width], gk_ref[...])
    q_ref[...] = q.astype(BF16)
    k_ref[...] = k.astype(BF16)
    v_ref[...] = qkv[:, 2 * width:].astype(BF16)

    z = _dot(hb, wf_ref[...]) + bf_ref[...]
    log_f = jnp.minimum(z, 0.0) - jnp.log(1.0 + jnp.exp(-jnp.abs(z)))
    hi, lo = _split_bf16(log_f)
    cum = _dot(tri_ref[...], hi) + _dot(tri_ref[...], lo) + carry_ref[0:1, :]
    tm = cum.shape[0]
    carry_ref[...] = jnp.broadcast_to(cum[tm - 1:tm, :], carry_ref.shape)
    cumt_ref[...] = (cum * LOG2_E).T[:fh_pad, :]


def _qkv(x2, g1, wqkv, wf, bf, gq_t, gk_t, seq, width, hd, fh_pad):
    t, d = x2.shape
    tm = min(TM_QKV, seq)
    bd = (jnp.arange(MXU_DIM)[:, None] // hd == jnp.arange(MXU_DIM)[None, :] // hd).astype(BF16)
    tri = (jnp.arange(tm)[:, None] >= jnp.arange(tm)[None, :]).astype(BF16)
    row = lambda i: (i, 0)
    return pl.pallas_call(
        functools.partial(_qkv_kernel, tiles_per_batch=seq // tm, width=width, hd=hd, fh_pad=fh_pad),
        grid=(t // tm,),
        in_specs=[pl.BlockSpec((tm, d), row), _const_spec((1, d)), _const_spec((d, 3 * width)),
                  _const_spec((d, LANES)), _const_spec((1, LANES)), _const_spec((1, width)),
                  _const_spec((1, width)), _const_spec((MXU_DIM, MXU_DIM)), _const_spec((tm, tm))],
        out_specs=[pl.BlockSpec((tm, width), row)] * 3 + [pl.BlockSpec((fh_pad, tm), lambda i: (0, i))],
        out_shape=[jax.ShapeDtypeStruct((t, width), BF16)] * 3 + [jax.ShapeDtypeStruct((fh_pad, t), F32)],
        scratch_shapes=[pltpu.VMEM((SUBLANES, LANES), F32)],
        compiler_params=_params(1),
        name="qkv",
    )(x2, g1, wqkv, wf, bf, gq_t, gk_t, bd, tri)


def _fox_kernel(q_ref, k_ref, v_ref, cumt_ref, o_ref, s_ref, *, tq, hd):
    hp = pl.program_id(1)
    seq = q_ref.shape[0]
    nh = LANES // hd
    lane = lax.broadcasted_iota(I32, (1, LANES), 1)
    on_or_below_diag = lax.broadcasted_iota(I32, (tq, tq), 1) <= lax.broadcasted_iota(I32, (tq, tq), 0)
    heads = [(hp * nh + hh, (lane >= hh * hd) & (lane < (hh + 1) * hd)) for hh in range(nh)]

    def scores(i, j, slot):
        q = q_ref[i * tq:(i + 1) * tq, :]
        kj = k_ref[j * tq:(j + 1) * tq, :]
        for hh, (h, in_head) in enumerate(heads):
            qh = jnp.where(in_head, q, jnp.zeros_like(q))
            s_ref[slot, hh] = _dot_nt(qh, kj) - cumt_ref[pl.ds(h, 1), j * tq:(j + 1) * tq]

    def consume(j, slot, carry, masked):
        vj = v_ref[j * tq:(j + 1) * tq, :]
        out = []
        for hh, ((_, in_head), (m, acc)) in enumerate(zip(heads, carry)):
            vh = jnp.where(in_head, vj, jnp.ones_like(vj))
            s = s_ref[slot, hh]
            if masked:
                s = jnp.where(on_or_below_diag, s, NEG_BIG)
            m_new = jnp.maximum(m, jnp.max(s, axis=-1, keepdims=True))
            p = jnp.exp2(s - m_new)
            out.append((m_new, jnp.exp2(m - m_new) * acc + _dot(p.astype(BF16), vh)))
        return tuple(out)

    tiles = [(i, j) for i in range(seq // tq) for j in range(i + 1)]
    scores(*tiles[0], 0)
    carry = None
    for n, (i, j) in enumerate(tiles):
        if n + 1 < len(tiles):
            scores(*tiles[n + 1], (n + 1) % 2)
        if j == 0:
            carry = tuple((jnp.full((tq, 1), NEG_BIG, F32), jnp.zeros((tq, LANES), F32)) for _ in heads)
        carry = consume(j, n % 2, carry, j == i)
        if j == i:
            o = None
            for hh in range(nh - 1, -1, -1):
                acc = carry[hh][1]
                other = ((hh + 1) % nh) * hd
                val = acc / acc[:, other:other + 1]
                o = val if o is None else jnp.where(lane < (hh + 1) * hd, val, o)
            o_ref[i * tq:(i + 1) * tq, :] = o.astype(BF16)


def _fox(q, k, v, cumt, batch, seq, width, hd, fh_pad):
    t = q.shape[0]
    tq = min(TQ_FOX, seq)
    assert seq % tq == 0
    seqmap = lambda b, hp: (b, hp)
    return pl.pallas_call(
        functools.partial(_fox_kernel, tq=tq, hd=hd),
        grid=(batch, width // LANES),
        in_specs=[pl.BlockSpec((seq, LANES), seqmap), pl.BlockSpec((seq, LANES), seqmap),
                  pl.BlockSpec((seq, LANES), seqmap), pl.BlockSpec((fh_pad, seq), lambda b, hp: (0, b))],
        out_specs=pl.BlockSpec((seq, LANES), seqmap),
        out_shape=jax.ShapeDtypeStruct((t, width), BF16),
        scratch_shapes=[pltpu.VMEM((2, LANES // hd, tq, tq), F32)],
        compiler_params=_params(2),
        name="fox",
    )(q, k, v, cumt)


def _merge_kernel(x_ref, yf_ref, mk_ref, mv_ref, g1_ref, wcat_ref, bg_ref, cw_ref, xqg_ref,
                  wbr_ref, wout_ref, g2_ref, wrh_ref, wrl_ref, br_ref, ltri_ref, utri_ref,
                  x1_ref, h2_ref, lpos_ref, lpost_ref, tw_ref, tcnt_ref, tbase_ref,
                  prev_ref, halo_ref, carry_ref, hi_ref, lo_ref, *, tiles_per_batch, width, xa_heads, xa_hd):
    i = pl.program_id(0)
    nt = pl.num_programs(0) - 1
    tm, d = x_ref.shape

    @pl.when(i == 0)
    def _():
        prev_ref[...] = jnp.zeros_like(prev_ref)
        halo_ref[...] = jnp.zeros_like(halo_ref)
        carry_ref[...] = jnp.zeros_like(carry_ref)
        hi_ref[...] = jnp.zeros_like(hi_ref)
        lo_ref[...] = jnp.zeros_like(lo_ref)

    routed = _route(hi_ref[...], lo_ref[...], wrh_ref, wrl_ref, br_ref)

    halo = jnp.where(i < nt, jnp.where(i % tiles_per_batch == 0, 0.0, prev_ref[...]), halo_ref[...])
    halo_ref[...] = halo

    x = x_ref[...]
    hb = _rms(x, g1_ref[...], NORM_EPS).astype(BF16)
    pc = _dot(hb, wcat_ref[...])

    uc = pc[:, :width] * pc[:, width:2 * width]
    row = lax.broadcasted_iota(I32, (tm, 1), 0)
    p1 = halo[SUBLANES - 1:SUBLANES, :]
    p2 = halo[SUBLANES - 2:SUBLANES - 1, :]
    m1 = jnp.where(row == 0, p1, pltpu.roll(uc, 1, 0))
    m2 = jnp.where(row == 0, p2, jnp.where(row == 1, p1, pltpu.roll(uc, 2, 0)))
    y_conv = pc[:, 2 * width:3 * width] * (cw_ref[0:1, :] * m2 + cw_ref[1:2, :] * m1 + cw_ref[2:3, :] * uc)
    prev_ref[...] = uc[tm - SUBLANES:tm, :]

    ys = []
    for h in range(xa_heads):
        sl = slice(3 * width + h * xa_hd, 3 * width + (h + 1) * xa_hd)
        ml = slice(h * xa_hd, (h + 1) * xa_hd)
        qh = _rms(pc[:, sl], xqg_ref[...], QK_EPS) * (xa_hd ** -0.5)
        s = _dot_nt(qh.astype(BF16), mk_ref[:, ml])
        p = jnp.exp(s - jnp.max(s, axis=-1, keepdims=True))
        ys.append(_dot(p.astype(BF16), mv_ref[:, ml]) / jnp.sum(p, axis=-1, keepdims=True))
    y_mem = jnp.concatenate(ys, axis=-1)

    _place(i, *routed, ltri_ref, utri_ref, lpos_ref, lpost_ref, tw_ref, tcnt_ref, tbase_ref, carry_ref)

    merged = jnp.zeros((tm, d), F32)
    for n, yb in enumerate((yf_ref[...], y_conv.astype(BF16), y_mem.astype(BF16))):
        gate = jax.nn.sigmoid(pc[:, 4 * width + n * d:4 * width + (n + 1) * d] + bg_ref[n:n + 1, :])
        merged = merged + gate * _dot(yb, wbr_ref[n])
    x1 = x + _dot(merged.astype(BF16), wout_ref[...])
    x1_ref[...] = x1
    hi, lo = _split_bf16(_rms(x1, g2_ref[...], NORM_EPS))
    h2_ref[...] = hi
    hi_ref[...] = hi
    lo_ref[...] = lo


def _route(hi, lo, wrh_ref, wrl_ref, br_ref):
    tm = hi.shape[0]
    logits = _dot(hi, wrh_ref[...]) + _dot(hi, wrl_ref[...]) + _dot(lo, wrh_ref[...]) + br_ref[...]
    lane = lax.broadcasted_iota(I32, (tm, LANES), 1)
    vals, idxs = [], []
    onehot = jnp.zeros((tm, LANES), F32)
    for _ in range(TOP_K):
        m = jnp.max(logits, axis=-1, keepdims=True)
        idx = jnp.min(jnp.where(logits == m, lane, LANES), axis=-1, keepdims=True)
        sel = lane == idx
        logits = jnp.where(sel, -jnp.inf, logits)
        onehot = onehot + sel.astype(F32)
        vals.append(m)
        idxs.append(idx)
    es = [jnp.exp(v - vals[0]) for v in vals]
    den = es[0]
    for e in es[1:]:
        den = den + e
    return onehot, idxs, [e / den for e in es]


def _place(i, onehot, idxs, tws, ltri_ref, utri_ref, lpos_ref, lpost_ref, tw_ref, tcnt_ref, tbase_ref,
           carry_ref):
    tm = onehot.shape[0]
    lane = lax.broadcasted_iota(I32, (tm, LANES), 1)
    cnt = jnp.sum(onehot, axis=0, keepdims=True)
    span = jnp.floor((cnt + (SUBLANES - 1)) * (1.0 / SUBLANES)) * SUBLANES
    s_hi, s_lo = _split_bf16(jnp.broadcast_to(span, (SUBLANES, LANES)))
    seg_start = (_dot(s_hi, utri_ref[...]) + _dot(s_lo, utri_ref[...]))[0:1, :]
    where_to = _dot(ltri_ref[...], onehot.astype(BF16)) + seg_start
    tcnt_ref[0] = jnp.broadcast_to(span, (SUBLANES, LANES))
    tbase_ref[0] = carry_ref[...]
    carry_ref[...] = jnp.where(i > 0, carry_ref[...] + span, 0.0)

    lpos_o = jnp.zeros((tm, LANES), F32)
    tw_o = jnp.zeros((tm, LANES), F32)
    for kk in range(TOP_K):
        lpos = jnp.sum(jnp.where(lane == idxs[kk], where_to, 0.0), axis=-1, keepdims=True)
        lpos_o = jnp.where(lane == kk, lpos, lpos_o)
        tw_o = jnp.where(lane == kk, tws[kk], tw_o)
    lpos_ref[...] = lpos_o.astype(I32)
    lpost_ref[0] = lpos_o.T[:SUBLANES, :].astype(I32)
    tw_ref[...] = tw_o


def _merge(x2, y_fox, mk, mv, g1, wcat, bg, cw, xqg, wbr, wout, g2, wrh, wrl, br,
           seq, n_mem, width, xa_heads, xa_hd):
    t, d = x2.shape
    tm = min(TM_WINDOW, seq)
    nt = t // tm
    tpb = seq // tm
    ltri = (jnp.arange(tm)[:, None] > jnp.arange(tm)[None, :]).astype(BF16)
    utri = (jnp.arange(LANES)[:, None] < jnp.arange(LANES)[None, :]).astype(BF16)
    mix = lambda i: jnp.minimum(i, nt - 1)
    rte = lambda i: jnp.maximum(i - 1, 0)
    row = lambda i: (mix(i), 0)
    rrow = lambda i: (rte(i), 0)
    tile3 = lambda i: (rte(i), 0, 0)
    mem_map = lambda i: (mix(i) // tpb, 0)
    return pl.pallas_call(
        functools.partial(_merge_kernel, tiles_per_batch=tpb, width=width, xa_heads=xa_heads, xa_hd=xa_hd),
        grid=(nt + 1,),
        in_specs=[pl.BlockSpec((tm, d), row), pl.BlockSpec((tm, width), row),
                  pl.BlockSpec((n_mem, width), mem_map), pl.BlockSpec((n_mem, width), mem_map),
                  _const_spec((1, d)), _const_spec(wcat.shape), _const_spec(bg.shape), _const_spec(cw.shape),
                  _const_spec((1, xa_hd)), _const_spec(wbr.shape), _const_spec((d, d)), _const_spec((1, d)),
                  _const_spec((d, LANES)), _const_spec((d, LANES)), _const_spec((1, LANES)),
                  _const_spec((tm, tm)), _const_spec((LANES, LANES))],
        out_specs=[pl.BlockSpec((tm, d), row), pl.BlockSpec((tm, d), row),
                   pl.BlockSpec((tm, LANES), rrow), pl.BlockSpec((1, SUBLANES, tm), tile3),
                   pl.BlockSpec((tm, LANES), rrow),
                   pl.BlockSpec((1, SUBLANES, LANES), tile3), pl.BlockSpec((1, SUBLANES, LANES), tile3)],
        out_shape=[jax.ShapeDtypeStruct((t, d), F32), jax.ShapeDtypeStruct((t, d), BF16),
                   jax.ShapeDtypeStruct((t, LANES), I32), jax.ShapeDtypeStruct((nt, SUBLANES, tm), I32),
                   jax.ShapeDtypeStruct((t, LANES), F32),
                   jax.ShapeDtypeStruct((nt, SUBLANES, LANES), F32),
                   jax.ShapeDtypeStruct((nt, SUBLANES, LANES), F32)],
        scratch_shapes=[pltpu.VMEM((SUBLANES, width), F32), pltpu.VMEM((SUBLANES, width), F32),
                        pltpu.VMEM((SUBLANES, LANES), F32), pltpu.VMEM((tm, d), BF16), pltpu.VMEM((tm, d), BF16)],
        compiler_params=_params(1),
        name="merge",
    )(x2, y_fox, mk, mv, g1, wcat, bg, cw, xqg, wbr, wout, g2, wrh, wrl, br, ltri, utri)


def _block_rows(tm, n_exp):
    return -(-(TOP_K * tm + (SUBLANES - 1) * n_exp) // MXU_DIM) * MXU_DIM


def _chunked_copies(src_ref, src0, dst_ref, dst0, n, sem, max_rows, wait, fixed_src=False):
    for b in range(max_rows.bit_length() - 1, SUBLANES.bit_length() - 2, -1):
        size = 1 << b
        off = (n >> (b + 1)) << (b + 1)

        @pl.when((n >> b) & 1 == 1)
        def _():
            src = 0 if fixed_src else pl.multiple_of(src0 + off, SUBLANES)
            cp = pltpu.make_async_copy(src_ref.at[pl.ds(src, size)],
                                       dst_ref.at[pl.ds(pl.multiple_of(dst0 + off, SUBLANES), size)], sem)
            if wait:
                cp.wait()
            else:
                cp.start()


def _chunk_classes(tm):
    return [1 << b for b in range(SUBLANES.bit_length() - 1, tm.bit_length())]


def _chunk_table(seg_len, seg_src, seg_dst, tm, n_exp):
    sizes = jnp.asarray(_chunk_classes(tm), I32)
    bit = (seg_len[:, :, None] // sizes) % 2
    off = seg_len[:, :, None] // (2 * sizes) * (2 * sizes)
    slot = jnp.cumsum(bit, axis=1) - 1
    hit = (slot[:, :, None, :] == jnp.arange(n_exp, dtype=I32)[None, None, :, None]) & (bit[:, :, None, :] == 1)
    pack = lambda start: jnp.sum(jnp.where(hit, (start[:, :, None] + off)[:, :, None, :], 0), axis=1)
    nw = seg_len.shape[0]
    cols = [jnp.swapaxes(pack(seg_src), 1, 2).reshape(nw, -1), jnp.swapaxes(pack(seg_dst), 1, 2).reshape(nw, -1),
            jnp.sum(bit, axis=1)]
    tbl = jnp.concatenate(cols, axis=1).astype(I32)
    assert tbl.shape[1] <= CHUNK_TABLE_LEN
    return jnp.pad(tbl, ((0, 0), (0, CHUNK_TABLE_LEN - tbl.shape[1]))).reshape(-1)


def _table_copies(tbl_ref, block_ref, sorted_ref, sem, tm, n_exp, to_sorted, wait):
    sizes = _chunk_classes(tm)
    for c, size in enumerate(sizes):
        def body(i, carry, c=c, size=size):
            blk = block_ref.at[pl.ds(pl.multiple_of(tbl_ref[c * n_exp + i], SUBLANES), size)]
            srt = sorted_ref.at[pl.ds(pl.multiple_of(tbl_ref[(len(sizes) + c) * n_exp + i], SUBLANES), size)]
            cp = pltpu.make_async_copy(blk, srt, sem) if to_sorted else pltpu.make_async_copy(srt, blk, sem)
            if wait:
                cp.wait()
            else:
                cp.start()
            return carry
        lax.fori_loop(0, tbl_ref[2 * len(sizes) * n_exp + c], body, 0)


def _dispatch_kernel(pad_start_ref, pad_cnt_ref, n_used_ref, tbl_ref, tbl_prev_ref, lpost_ref, h2_ref, xs_ref,
                     buf_ref, zero_ref, sems, zsem, *, n_exp, n_tiles):
    w = pl.program_id(0)
    nw = pl.num_programs(0)
    tm = h2_ref.shape[0]
    rows = buf_ref.shape[1]
    slot = w % 2

    def clear(wait):
        def per_expert(e, c):
            _chunked_copies(zero_ref, 0, xs_ref, pad_start_ref[e], pad_cnt_ref[e], zsem, ZERO_ROWS, wait,
                            fixed_src=True)
            return c

        def per_tile(i, c):
            for part in range(TM_EXPERT // ZERO_ROWS):
                cp = pltpu.make_async_copy(
                    zero_ref, xs_ref.at[pl.ds(i * TM_EXPERT + part * ZERO_ROWS, ZERO_ROWS)], zsem)
                if wait:
                    cp.wait()
                else:
                    cp.start()
            return c

        lax.fori_loop(0, n_exp, per_expert, 0)
        lax.fori_loop(n_used_ref[0], n_tiles, per_tile, 0)

    @pl.when(w == 0)
    def _():
        zero_ref[...] = jnp.zeros_like(zero_ref)
        clear(False)

    r_iota = lax.broadcasted_iota(I32, (rows, tm), 0)
    sel = r_iota == lpost_ref[0, 0:1, :]
    for kk in range(1, TOP_K):
        sel = sel | (r_iota == lpost_ref[0, kk:kk + 1, :])
    perm = jnp.where(sel, 1.0, 0.0).astype(BF16)

    buf_ref[slot] = _dot(perm, h2_ref[...])
    _table_copies(tbl_ref, buf_ref.at[slot], xs_ref, sems.at[slot], tm, n_exp, True, False)

    @pl.when(w >= 1)
    def _():
        _table_copies(tbl_prev_ref, buf_ref.at[1 - slot], xs_ref, sems.at[1 - slot], tm, n_exp, True, True)

    @pl.when(w == nw - 1)
    def _():
        _table_copies(tbl_ref, buf_ref.at[slot], xs_ref, sems.at[slot], tm, n_exp, True, True)
        clear(True)


def _dispatch(pad_start, pad_cnt, n_used, chunk_tbl, lpost, h2, n_tiles, n_exp):
    t, d = h2.shape
    tm = lpost.shape[2]
    smem_blk = lambda shift: pl.BlockSpec((CHUNK_TABLE_LEN,), lambda i, *_: (jnp.maximum(i + shift, 0),),
                                          memory_space=pltpu.SMEM)
    grid_spec = pltpu.PrefetchScalarGridSpec(
        num_scalar_prefetch=3,
        grid=(t // tm,),
        in_specs=[smem_blk(0), smem_blk(-1),
                  pl.BlockSpec((1, SUBLANES, tm), lambda i, *_: (i, 0, 0)),
                  pl.BlockSpec((tm, d), lambda i, *_: (i, 0))],
        out_specs=pl.BlockSpec(memory_space=pl.ANY),
        scratch_shapes=[pltpu.VMEM((2, _block_rows(tm, n_exp), d), F32), pltpu.VMEM((ZERO_ROWS, d), F32),
                        pltpu.SemaphoreType.DMA((2,)), pltpu.SemaphoreType.DMA(())],
    )
    return pl.pallas_call(
        functools.partial(_dispatch_kernel, n_exp=n_exp, n_tiles=n_tiles),
        grid_spec=grid_spec,
        out_shape=jax.ShapeDtypeStruct((n_tiles * TM_EXPERT, d), F32),
        compiler_params=_params(1),
        name="dispatch",
    )(pad_start, pad_cnt, n_used, chunk_tbl, chunk_tbl, lpost, h2)


def _expert_kernel(tile_end_ref, wup_ref, bup_ref, wdn_ref, bdn_ref, xs_ref, ys_ref,
                   wup_bf_ref, wdn_bf_ref, xbuf_ref, ybuf_ref, xsems, ysems, *, d_ff, n_tiles):
    e = pl.program_id(0)
    n_exp = pl.num_programs(0)
    tm = xbuf_ref.shape[1]
    n_used = tile_end_ref[n_exp - 1]
    first = jnp.where(e == 0, 0, tile_end_ref[jnp.maximum(e - 1, 0)])
    last = tile_end_ref[e]

    def rows_of(g):
        return pl.ds(pl.multiple_of(g * tm, tm), tm)

    def x_copy(g):
        return pltpu.make_async_copy(xs_ref.at[rows_of(g)], xbuf_ref.at[g % 2], xsems.at[g % 2])

    def y_copy(g):
        return pltpu.make_async_copy(ybuf_ref.at[g % 2], ys_ref.at[rows_of(g)], ysems.at[g % 2])

    @pl.when((e == 0) & (n_used > 0))
    def _():
        x_copy(0).start()

    wup_bf_ref[...] = wup_ref[0].astype(BF16)
    wdn_bf_ref[...] = wdn_ref[0].astype(BF16)

    def tile(g, carry):
        slot = g % 2
        x_copy(g).wait()

        @pl.when(g + 1 < n_used)
        def _():
            x_copy(g + 1).start()

        @pl.when(g >= 2)
        def _():
            y_copy(g - 2).wait()

        h = _dot(xbuf_ref[slot].astype(BF16), wup_bf_ref[...]) + bup_ref[0]
        glu = jnp.minimum(h[:, :d_ff], SWIGLU_LIMIT)
        lin = jnp.clip(h[:, d_ff:], -SWIGLU_LIMIT, SWIGLU_LIMIT)
        a = glu * jax.nn.sigmoid(SWIGLU_ALPHA * glu) * (lin + 1.0)
        ybuf_ref[slot] = _dot(a.astype(BF16), wdn_bf_ref[...]) + bdn_ref[0]
        y_copy(g).start()
        return carry

    lax.fori_loop(first, last, tile, 0)

    @pl.when(e == n_exp - 1)
    def _():
        @pl.when(n_used >= 2)
        def _():
            y_copy(n_used - 2).wait()

        @pl.when(n_used >= 1)
        def _():
            y_copy(n_used - 1).wait()

        ybuf_ref[0] = jnp.zeros(ybuf_ref.shape[1:], F32)

        def clear(g, wait):
            cp = pltpu.make_async_copy(ybuf_ref.at[0], ys_ref.at[rows_of(g)], ysems.at[0])
            if wait:
                cp.wait()
            else:
                cp.start()

        lax.fori_loop(n_used, n_tiles, lambda g, c: (clear(g, False), c)[1], 0)
        lax.fori_loop(n_used, n_tiles, lambda g, c: (clear(g, True), c)[1], 0)


def _experts(tile_end, xs, wup, bup, wdn, bdn):
    n_rows, d = xs.shape
    n_exp, d_ff, _ = wdn.shape
    tm = TM_EXPERT
    exp3 = lambda e, *_: (e, 0, 0)
    grid_spec = pltpu.PrefetchScalarGridSpec(
        num_scalar_prefetch=1,
        grid=(n_exp,),
        in_specs=[pl.BlockSpec((1, d, 2 * d_ff), exp3), pl.BlockSpec((1, 1, 2 * d_ff), exp3),
                  pl.BlockSpec((1, d_ff, d), exp3), pl.BlockSpec((1, 1, d), exp3),
                  pl.BlockSpec(memory_space=pl.ANY)],
        out_specs=pl.BlockSpec(memory_space=pl.ANY),
        scratch_shapes=[pltpu.VMEM((d, 2 * d_ff), BF16), pltpu.VMEM((d_ff, d), BF16),
                        pltpu.VMEM((2, tm, d), F32), pltpu.VMEM((2, tm, d), F32),
                        pltpu.SemaphoreType.DMA((2,)), pltpu.SemaphoreType.DMA((2,))],
    )
    return pl.pallas_call(
        functools.partial(_expert_kernel, d_ff=d_ff, n_tiles=n_rows // tm),
        grid_spec=grid_spec,
        out_shape=jax.ShapeDtypeStruct((n_rows, d), F32),
        compiler_params=_params(1),
        name="experts",
    )(tile_end, wup, bup, wdn, bdn, xs)


def _combine_kernel(tbl_ref, tbl_next_ref, lpos_ref, tw_ref, x1_ref, ys_ref, o_ref, buf_ref, sems, *, n_exp):
    w = pl.program_id(0)
    nw = pl.num_programs(0)
    tm = x1_ref.shape[0]
    rows = buf_ref.shape[1]
    slot = w % 2

    @pl.when(w == 0)
    def _():
        buf_ref[...] = jnp.zeros_like(buf_ref)
        _table_copies(tbl_ref, buf_ref.at[0], ys_ref, sems.at[0], tm, n_exp, False, False)

    @pl.when(w + 1 < nw)
    def _():
        _table_copies(tbl_next_ref, buf_ref.at[1 - slot], ys_ref, sems.at[1 - slot], tm, n_exp, False, False)

    _table_copies(tbl_ref, buf_ref.at[slot], ys_ref, sems.at[slot], tm, n_exp, False, True)

    c_iota = lax.broadcasted_iota(I32, (tm, rows), 1)
    lpos = lpos_ref[...]
    tw = tw_ref[...]
    wperm = jnp.zeros((tm, rows), F32)
    for kk in range(TOP_K):
        wperm = jnp.where(c_iota == lpos[:, kk:kk + 1], tw[:, kk:kk + 1], wperm)
    o_ref[...] = x1_ref[...] + _dot(wperm.astype(BF16), buf_ref[slot].astype(BF16))


def _combine(chunk_tbl, lpos, tw, x1, ys, n_exp):
    t, d = x1.shape
    tm = min(TM_WINDOW, t)
    nw = t // tm
    row = lambda i: (i, 0)
    smem_blk = lambda shift: pl.BlockSpec((CHUNK_TABLE_LEN,), lambda i: (jnp.minimum(i + shift, nw - 1),),
                                          memory_space=pltpu.SMEM)
    return pl.pallas_call(
        functools.partial(_combine_kernel, n_exp=n_exp),
        grid=(nw,),
        in_specs=[smem_blk(0), smem_blk(1),
                  pl.BlockSpec((tm, LANES), row), pl.BlockSpec((tm, LANES), row), pl.BlockSpec((tm, d), row),
                  pl.BlockSpec(memory_space=pl.ANY)],
        out_specs=pl.BlockSpec((tm, d), row),
        out_shape=jax.ShapeDtypeStruct((t, d), F32),
        scratch_shapes=[pltpu.VMEM((2, _block_rows(tm, n_exp), d), F32), pltpu.SemaphoreType.DMA((2,))],
        compiler_params=_params(1),
        name="combine",
    )(chunk_tbl, chunk_tbl, lpos, tw, x1, ys)


def _pad_lanes(a, value=0.0):
    return jnp.pad(a, ((0, 0), (0, LANES - a.shape[-1])), constant_values=value)


def _layer(x, mem, norm1_g, w_in, b_forget, b_gate, fox_q_g, fox_k_g, conv_w, mem_norm_g, w_mem_kv,
           xa_q_g, xa_k_g, w_branch, w_out, norm2_g, w_router, b_router, w_up, b_up, w_down, b_down):
    batch, seq, d = x.shape
    n_mem = mem.shape[1]
    n_branch, width, _ = w_branch.shape
    n_fh = b_forget.shape[0]
    hd = fox_q_g.shape[0]
    xa_hd = xa_q_g.shape[0]
    xa_heads = width // xa_hd
    n_exp = w_router.shape[1]
    assert n_fh * hd == width and n_branch == 3 and conv_w.shape[0] == 3
    assert LANES % hd == 0 and width % MXU_DIM == 0 and n_exp <= LANES and n_fh <= LANES
    fh_pad = -(-n_fh // SUBLANES) * SUBLANES
    t = batch * seq
    x2 = x.reshape(t, d)
    row1 = lambda a: a.reshape(1, -1).astype(F32)

    o_f = 3 * width
    o_conv = o_f + n_fh
    wqkv = w_in[:, :o_f].astype(BF16)
    wf = _pad_lanes(w_in[:, o_f:o_conv]).astype(BF16)
    wcat = w_in[:, o_conv:].astype(BF16)

    mk, mv = _memkv(mem.reshape(batch * n_mem, d), row1(mem_norm_g), w_mem_kv.astype(BF16), row1(xa_k_g),
                    xa_heads, xa_hd)
    q, k, v, cumt = _qkv(x2, row1(norm1_g), wqkv, wf, _pad_lanes(row1(b_forget)),
                         row1(jnp.tile(fox_q_g, n_fh)), row1(jnp.tile(fox_k_g, n_fh)),
                         seq, width, hd, fh_pad)
    y_fox = _fox(q, k, v, cumt, batch, seq, width, hd, fh_pad)

    wr = _pad_lanes(w_router.astype(F32))
    wr_hi = wr.astype(BF16)
    wr_lo = (wr - wr_hi.astype(F32)).astype(BF16)
    x1, h2, lpos, lpost, tw, tcnt, tbase = _merge(
        x2, y_fox, mk, mv, row1(norm1_g), wcat, b_gate.astype(F32), conv_w.astype(F32), row1(xa_q_g),
        w_branch.astype(BF16), w_out.astype(BF16), row1(norm2_g), wr_hi, wr_lo,
        _pad_lanes(row1(b_router), NEG_BIG), seq, n_mem, width, xa_heads, xa_hd)

    seg_len = tcnt[:, 0, :n_exp].astype(I32)
    seg_base = tbase[:, 0, :n_exp].astype(I32)
    counts = seg_base[-1] + seg_len[-1]
    tiles_per = (counts + TM_EXPERT - 1) // TM_EXPERT
    tile_end = jnp.cumsum(tiles_per)
    group_start = (tile_end - tiles_per) * TM_EXPERT
    seg_dst = group_start[None, :] + seg_base
    seg_src = jnp.cumsum(seg_len, axis=1) - seg_len
    n_tiles = -(-(t * TOP_K + (SUBLANES - 1) * seg_len.size) // TM_EXPERT) + n_exp
    n_used = tile_end[-1:].astype(I32)
    pad_start = (group_start + counts).astype(I32)
    pad_cnt = (tiles_per * TM_EXPERT - counts).astype(I32)
    chunk_tbl = _chunk_table(seg_len, seg_src.astype(I32), seg_dst.astype(I32), lpost.shape[2], n_exp)

    xs = _dispatch(pad_start, pad_cnt, n_used, chunk_tbl, lpost, h2, n_tiles, n_exp)
    ys = _experts(tile_end.astype(I32), xs, w_up.astype(F32), b_up[:, None, :].astype(F32),
                  w_down.astype(F32), b_down[:, None, :].astype(F32))
    out = _combine(chunk_tbl, lpos, tw, x1, ys, n_exp)
    return out.reshape(batch, seq, d)


def kernel(x, mem, norm1_g, w_in, b_forget, b_gate, fox_q_g, fox_k_g, conv_w, mem_norm_g, w_mem_kv, xa_q_g,
           xa_k_g, w_branch, w_out, norm2_g, w_router, b_router, w_up, b_up, w_down, b_down):
    for l in range(norm1_g.shape[0]):
        x = _layer(x, mem, norm1_g[l], w_in[l], b_forget[l], b_gate[l], fox_q_g[l], fox_k_g[l], conv_w[l],
                   mem_norm_g[l], w_mem_kv[l], xa_q_g[l], xa_k_g[l], w_branch[l], w_out[l], norm2_g[l],
                   w_router[l], b_router[l], w_up[l], b_up[l], w_down[l], b_down[l])
    return x
```

```python
import functools

import jax
import jax.numpy as jnp
from jax import lax
from jax.experimental import pallas as pl
from jax.experimental.pallas import tpu as pltpu

F32 = jnp.float32
BF16 = jnp.bfloat16
I32 = jnp.int32

NORM_EPS = 1e-5
QK_EPS = 1e-6
TOP_K = 4
SWIGLU_ALPHA = 1.702
SWIGLU_LIMIT = 7.0
NEG_BIG = -1e30
LOG2_E = 1.4426950408889634

LANES = 128
SUBLANES = 8
MXU_DIM = 256
VMEM_LIMIT_BYTES = 56 * 1024 * 1024

TM_QKV = 512
TQ_FOX = 512
TM_WINDOW = 256
MERGE_WINDOWS = 2
TM_EXPERT = 512
ZERO_ROWS = 256
CHUNK_TABLE_LEN = 512


def _params(n_axes):
    return pltpu.CompilerParams(dimension_semantics=("arbitrary",) * n_axes,
                                vmem_limit_bytes=VMEM_LIMIT_BYTES)


def _const_spec(shape):
    zeros = (0,) * len(shape)
    return pl.BlockSpec(shape, lambda *_: zeros, pipeline_mode=pl.Buffered(1))


def _rms(x, g, eps):
    return x * lax.rsqrt(jnp.mean(x * x, axis=-1, keepdims=True) + eps) * g


def _split_bf16(x):
    hi = x.astype(BF16)
    lo = (x - hi.astype(F32)).astype(BF16)
    return hi, lo


def _dot(a, b):
    return jnp.dot(a, b, preferred_element_type=F32)


def _dot_nt(a, b):
    return lax.dot_general(a, b, (((1,), (1,)), ((), ())), preferred_element_type=F32)


def _memkv_kernel(mem_ref, g_ref, w_ref, kg_ref, mk_ref, mv_ref, *, n_heads, hd):
    y = _rms(mem_ref[...], g_ref[...], NORM_EPS)
    kv = _dot(y.astype(BF16), w_ref[...])
    width = n_heads * hd
    for h in range(n_heads):
        sl = slice(h * hd, (h + 1) * hd)
        mk_ref[:, sl] = _rms(kv[:, sl], kg_ref[...], QK_EPS).astype(BF16)
    mv_ref[...] = kv[:, width:].astype(BF16)


def _memkv(mem2, g, w_kv, kg, n_heads, hd):
    rows, d = mem2.shape
    width = n_heads * hd
    tm = min(rows, 512)
    return pl.pallas_call(
        functools.partial(_memkv_kernel, n_heads=n_heads, hd=hd),
        grid=(rows // tm,),
        in_specs=[pl.BlockSpec((tm, d), lambda i: (i, 0)),
                  _const_spec((1, d)), _const_spec((d, 2 * width)), _const_spec((1, hd))],
        out_specs=[pl.BlockSpec((tm, width), lambda i: (i, 0)),
                   pl.BlockSpec((tm, width), lambda i: (i, 0))],
        out_shape=[jax.ShapeDtypeStruct((rows, width), BF16)] * 2,
        compiler_params=_params(1),
        name="memkv",
    )(mem2, g, w_kv, kg)


def _qkv_kernel(x_ref, g1_ref, wqkv_ref, wf_ref, bf_ref, gq_ref, gk_ref, bd_ref, tri_ref,
                q_ref, k_ref, v_ref, cumt_ref, carry_ref,
                *, tiles_per_batch, width, hd, fh_pad):
    i = pl.program_id(0)

    @pl.when(i % tiles_per_batch == 0)
    def _():
        carry_ref[...] = jnp.zeros_like(carry_ref)

    hb = _rms(x_ref[...], g1_ref[...], NORM_EPS).astype(BF16)
    qkv = _dot(hb, wqkv_ref[...])

    def head_norm(z, g):
        sq = (z * z).astype(BF16)
        parts = []
        for c in range(width // MXU_DIM):
            sl = slice(c * MXU_DIM, (c + 1) * MXU_DIM)
            parts.append(_dot(sq[:, sl], bd_ref[...]))
        ms = jnp.concatenate(parts, axis=-1) * (1.0 / hd)
        return z * lax.rsqrt(ms + QK_EPS) * g

    q = head_norm(qkv[:, :width], gq_ref[...]) * (hd ** -0.5 * LOG2_E)
    k = head_norm(qkv[:, width:2 * width], gk_ref[...])
    q_ref[...] = q.astype(BF16)
    k_ref[...] = k.astype(BF16)
    v_ref[...] = qkv[:, 2 * width:].astype(BF16)

    z = _dot(hb, wf_ref[...]) + bf_ref[...]
    log_f = jnp.minimum(z, 0.0) - jnp.log(1.0 + jnp.exp(-jnp.abs(z)))
    hi, lo = _split_bf16(log_f)
    cum = _dot(tri_ref[...], hi) + _dot(tri_ref[...], lo) + carry_ref[0:1, :]
    tm = cum.shape[0]
    carry_ref[...] = jnp.broadcast_to(cum[tm - 1:tm, :], carry_ref.shape)
    cumt_ref[...] = (cum * LOG2_E).T[:fh_pad, :]


def _qkv(x2, g1, wqkv, wf, bf, gq_t, gk_t, seq, width, hd, fh_pad):
    t, d = x2.shape
    tm = min(TM_QKV, seq)
    bd = (jnp.arange(MXU_DIM)[:, None] // hd == jnp.arange(MXU_DIM)[None, :] // hd).astype(BF16)
    tri = (jnp.arange(tm)[:, None] >= jnp.arange(tm)[None, :]).astype(BF16)
    row = lambda i: (i, 0)
    return pl.pallas_call(
        functools.partial(_qkv_kernel, tiles_per_batch=seq // tm, width=width, hd=hd, fh_pad=fh_pad),
        grid=(t // tm,),
        in_specs=[pl.BlockSpec((tm, d), row), _const_spec((1, d)), _const_spec((d, 3 * width)),
                  _const_spec((d, LANES)), _const_spec((1, LANES)), _const_spec((1, width)),
                  _const_spec((1, width)), _const_spec((MXU_DIM, MXU_DIM)), _const_spec((tm, tm))],
        out_specs=[pl.BlockSpec((tm, width), row)] * 3 + [pl.BlockSpec((fh_pad, tm), lambda i: (0, i))],
        out_shape=[jax.ShapeDtypeStruct((t, width), BF16)] * 3 + [jax.ShapeDtypeStruct((fh_pad, t), F32)],
        scratch_shapes=[pltpu.VMEM((SUBLANES, LANES), F32)],
        compiler_params=_params(1),
        name="qkv",
    )(x2, g1, wqkv, wf, bf, gq_t, gk_t, bd, tri)


def _fox_kernel(q_ref, k_ref, v_ref, cumt_ref, o_ref, s_ref, *, tq, hd):
    hp = pl.program_id(1)
    seq = q_ref.shape[0]
    nh = LANES // hd
    lane = lax.broadcasted_iota(I32, (1, LANES), 1)
    on_or_below_diag = lax.broadcasted_iota(I32, (tq, tq), 1) <= lax.broadcasted_iota(I32, (tq, tq), 0)
    heads = [(hp * nh + hh, (lane >= hh * hd) & (lane < (hh + 1) * hd)) for hh in range(nh)]

    def scores(i, j, slot):
        q = q_ref[i * tq:(i + 1) * tq, :]
        kj = k_ref[j * tq:(j + 1) * tq, :]
        for hh, (h, in_head) in enumerate(heads):
            qh = jnp.where(in_head, q, jnp.zeros_like(q))
            s_ref[slot, hh] = _dot_nt(qh, kj) - cumt_ref[pl.ds(h, 1), j * tq:(j + 1) * tq]

    def consume(j, slot, carry, masked):
        vj = v_ref[j * tq:(j + 1) * tq, :]
        out = []
        for hh, ((_, in_head), (m, acc)) in enumerate(zip(heads, carry)):
            vh = jnp.where(in_head, vj, jnp.ones_like(vj))
            s = s_ref[slot, hh]
            if masked:
                s = jnp.where(on_or_below_diag, s, NEG_BIG)
            m_new = jnp.maximum(m, jnp.max(s, axis=-1, keepdims=True))
            p = jnp.exp2(s - m_new)
            out.append((m_new, jnp.exp2(m - m_new) * acc + _dot(p.astype(BF16), vh)))
        return tuple(out)

    tiles = [(i, j) for i in range(seq // tq) for j in range(i + 1)]
    scores(*tiles[0], 0)
    carry = None
    for n, (i, j) in enumerate(tiles):
        if n + 1 < len(tiles):
            scores(*tiles[n + 1], (n + 1) % 2)
        if j == 0:
            carry = tuple((jnp.full((tq, 1), NEG_BIG, F32), jnp.zeros((tq, LANES), F32)) for _ in heads)
        carry = consume(j, n % 2, carry, j == i)
        if j == i:
            o = None
            for hh in range(nh - 1, -1, -1):
                acc = carry[hh][1]
                other = ((hh + 1) % nh) * hd
                val = acc / acc[:, other:other + 1]
                o = val if o is None else jnp.where(lane < (hh + 1) * hd, val, o)
            o_ref[i * tq:(i + 1) * tq, :] = o.astype(BF16)


def _fox(q, k, v, cumt, batch, seq, width, hd, fh_pad):
    t = q.shape[0]
    tq = min(TQ_FOX, seq)
    assert seq % tq == 0
    seqmap = lambda b, hp: (b, hp)
    return pl.pallas_call(
        functools.partial(_fox_kernel, tq=tq, hd=hd),
        grid=(batch, width // LANES),
        in_specs=[pl.BlockSpec((seq, LANES), seqmap), pl.BlockSpec((seq, LANES), seqmap),
                  pl.BlockSpec((seq, LANES), seqmap), pl.BlockSpec((fh_pad, seq), lambda b, hp: (0, b))],
        out_specs=pl.BlockSpec((seq, LANES), seqmap),
        out_shape=jax.ShapeDtypeStruct((t, width), BF16),
        scratch_shapes=[pltpu.VMEM((2, LANES // hd, tq, tq), F32)],
        compiler_params=_params(2),
        name="fox",
    )(q, k, v, cumt)


def _merge_kernel(x_ref, yf_ref, mk_ref, mv_ref, g1_ref, wcat_ref, bg_ref, cw_ref, xqg_ref,
                  wbr_ref, wout_ref, g2_ref, wrh_ref, wrl_ref, br_ref, ltri_ref, utri_ref,
                  x1_ref, h2_ref, lpos_ref, lpost_ref, tw_ref, tcnt_ref, tbase_ref,
                  prev_ref, halo_ref, carry_ref, hi_ref, lo_ref, *, steps_per_batch, width, xa_heads, xa_hd):
    i = pl.program_id(0)
    nt = pl.num_programs(0) - 1
    tm = ltri_ref.shape[0]
    n_win = x_ref.shape[0] // tm

    @pl.when(i == 0)
    def _():
        prev_ref[...] = jnp.zeros_like(prev_ref)
        halo_ref[...] = jnp.zeros_like(halo_ref)
        carry_ref[...] = jnp.zeros_like(carry_ref)
        hi_ref[...] = jnp.zeros_like(hi_ref)
        lo_ref[...] = jnp.zeros_like(lo_ref)

    routed = [_route(hi_ref[k * tm:(k + 1) * tm, :], lo_ref[k * tm:(k + 1) * tm, :], wrh_ref, wrl_ref, br_ref)
              for k in range(n_win)]

    halo = jnp.where(i < nt, jnp.where(i % steps_per_batch == 0, 0.0, prev_ref[...]), halo_ref[...])
    halo_ref[...] = halo

    for k in range(n_win):
        rows = slice(k * tm, (k + 1) * tm)
        place = functools.partial(_place, i, *routed[k], ltri_ref, utri_ref, lpos_ref, lpost_ref, tw_ref,
                                  tcnt_ref, tbase_ref, carry_ref, rows, k)
        halo = _mix_window(x_ref, yf_ref, mk_ref, mv_ref, g1_ref, wcat_ref, bg_ref, cw_ref, xqg_ref, wbr_ref,
                           wout_ref, g2_ref, x1_ref, h2_ref, hi_ref, lo_ref, rows, halo, place,
                           width=width, xa_heads=xa_heads, xa_hd=xa_hd)
    prev_ref[...] = halo


def _mix_window(x_ref, yf_ref, mk_ref, mv_ref, g1_ref, wcat_ref, bg_ref, cw_ref, xqg_ref, wbr_ref, wout_ref,
                g2_ref, x1_ref, h2_ref, hi_ref, lo_ref, rows, halo, place, *, width, xa_heads, xa_hd):
    tm = rows.stop - rows.start
    d = x_ref.shape[1]
    x = x_ref[rows, :]
    hb = _rms(x, g1_ref[...], NORM_EPS).astype(BF16)
    pc = _dot(hb, wcat_ref[...])

    uc = pc[:, :width] * pc[:, width:2 * width]
    row = lax.broadcasted_iota(I32, (tm, 1), 0)
    p1 = halo[SUBLANES - 1:SUBLANES, :]
    p2 = halo[SUBLANES - 2:SUBLANES - 1, :]
    m1 = jnp.where(row == 0, p1, pltpu.roll(uc, 1, 0))
    m2 = jnp.where(row == 0, p2, jnp.where(row == 1, p1, pltpu.roll(uc, 2, 0)))
    y_conv = pc[:, 2 * width:3 * width] * (cw_ref[0:1, :] * m2 + cw_ref[1:2, :] * m1 + cw_ref[2:3, :] * uc)

    ys = []
    for h in range(xa_heads):
        sl = slice(3 * width + h * xa_hd, 3 * width + (h + 1) * xa_hd)
        ml = slice(h * xa_hd, (h + 1) * xa_hd)
        qh = _rms(pc[:, sl], xqg_ref[...], QK_EPS) * (xa_hd ** -0.5)
        s = _dot_nt(qh.astype(BF16), mk_ref[:, ml])
        p = jnp.exp(s - jnp.max(s, axis=-1, keepdims=True))
        ys.append(_dot(p.astype(BF16), mv_ref[:, ml]) / jnp.sum(p, axis=-1, keepdims=True))
    y_mem = jnp.concatenate(ys, axis=-1)

    place()

    merged = jnp.zeros((tm, d), F32)
    for n, yb in enumerate((yf_ref[rows, :], y_conv.astype(BF16), y_mem.astype(BF16))):
        gate = jax.nn.sigmoid(pc[:, 4 * width + n * d:4 * width + (n + 1) * d] + bg_ref[n:n + 1, :])
        merged = merged + gate * _dot(yb, wbr_ref[n])
    x1 = x + _dot(merged.astype(BF16), wout_ref[...])
    x1_ref[rows, :] = x1
    hi, lo = _split_bf16(_rms(x1, g2_ref[...], NORM_EPS))
    h2_ref[rows, :] = hi
    hi_ref[rows, :] = hi
    lo_ref[rows, :] = lo
    return uc[tm - SUBLANES:tm, :]


def _route(hi, lo, wrh_ref, wrl_ref, br_ref):
    tm = hi.shape[0]
    logits = _dot(hi, wrh_ref[...]) + _dot(hi, wrl_ref[...]) + _dot(lo, wrh_ref[...]) + br_ref[...]
    lane = lax.broadcasted_iota(I32, (tm, LANES), 1)
    vals, idxs = [], []
    onehot = jnp.zeros((tm, LANES), F32)
    for _ in range(TOP_K):
        m = jnp.max(logits, axis=-1, keepdims=True)
        idx = jnp.min(jnp.where(logits == m, lane, LANES), axis=-1, keepdims=True)
        sel = lane == idx
        logits = jnp.where(sel, -jnp.inf, logits)
        onehot = onehot + sel.astype(F32)
        vals.append(m)
        idxs.append(idx)
    es = [jnp.exp(v - vals[0]) for v in vals]
    den = es[0]
    for e in es[1:]:
        den = den + e
    return onehot, idxs, [e / den for e in es]


def _place(i, onehot, idxs, tws, ltri_ref, utri_ref, lpos_ref, lpost_ref, tw_ref, tcnt_ref, tbase_ref,
           carry_ref, rows, k):
    tm = onehot.shape[0]
    lane = lax.broadcasted_iota(I32, (tm, LANES), 1)
    cnt = jnp.sum(onehot, axis=0, keepdims=True)
    span = jnp.floor((cnt + (SUBLANES - 1)) * (1.0 / SUBLANES)) * SUBLANES
    s_hi, s_lo = _split_bf16(jnp.broadcast_to(span, (SUBLANES, LANES)))
    seg_start = (_dot(s_hi, utri_ref[...]) + _dot(s_lo, utri_ref[...]))[0:1, :]
    where_to = _dot(ltri_ref[...], onehot.astype(BF16)) + seg_start
    tcnt_ref[k] = jnp.broadcast_to(span, (SUBLANES, LANES))
    tbase_ref[k] = carry_ref[...]
    carry_ref[...] = jnp.where(i > 0, carry_ref[...] + span, 0.0)

    lpos_o = jnp.zeros((tm, LANES), F32)
    tw_o = jnp.zeros((tm, LANES), F32)
    for kk in range(TOP_K):
        lpos = jnp.sum(jnp.where(lane == idxs[kk], where_to, 0.0), axis=-1, keepdims=True)
        lpos_o = jnp.where(lane == kk, lpos, lpos_o)
        tw_o = jnp.where(lane == kk, tws[kk], tw_o)
    lpos_ref[rows, :] = lpos_o.astype(I32)
    lpost_ref[k] = lpos_o.T[:SUBLANES, :].astype(I32)
    tw_ref[rows, :] = tw_o


def _merge(x2, y_fox, mk, mv, g1, wcat, bg, cw, xqg, wbr, wout, g2, wrh, wrl, br,
           seq, n_mem, width, xa_heads, xa_hd):
    t, d = x2.shape
    tm = min(TM_WINDOW, seq)
    nwin = min(MERGE_WINDOWS, seq // tm)
    tb = nwin * tm
    nt = t // tb
    tpb = seq // tb
    ltri = (jnp.arange(tm)[:, None] > jnp.arange(tm)[None, :]).astype(BF16)
    utri = (jnp.arange(LANES)[:, None] < jnp.arange(LANES)[None, :]).astype(BF16)
    mix = lambda i: jnp.minimum(i, nt - 1)
    rte = lambda i: jnp.maximum(i - 1, 0)
    row = lambda i: (mix(i), 0)
    rrow = lambda i: (rte(i), 0)
    tile3 = lambda i: (rte(i), 0, 0)
    mem_map = lambda i: (mix(i) // tpb, 0)
    return pl.pallas_call(
        functools.partial(_merge_kernel, steps_per_batch=tpb, width=width, xa_heads=xa_heads, xa_hd=xa_hd),
        grid=(nt + 1,),
        in_specs=[pl.BlockSpec((tb, d), row), pl.BlockSpec((tb, width), row),
                  pl.BlockSpec((n_mem, width), mem_map), pl.BlockSpec((n_mem, width), mem_map),
                  _const_spec((1, d)), _const_spec(wcat.shape), _const_spec(bg.shape), _const_spec(cw.shape),
                  _const_spec((1, xa_hd)), _const_spec(wbr.shape), _const_spec((d, d)), _const_spec((1, d)),
                  _const_spec((d, LANES)), _const_spec((d, LANES)), _const_spec((1, LANES)),
                  _const_spec((tm, tm)), _const_spec((LANES, LANES))],
        out_specs=[pl.BlockSpec((tb, d), row), pl.BlockSpec((tb, d), row),
                   pl.BlockSpec((tb, LANES), rrow), pl.BlockSpec((nwin, SUBLANES, tm), tile3),
                   pl.BlockSpec((tb, LANES), rrow),
                   pl.BlockSpec((nwin, SUBLANES, LANES), tile3), pl.BlockSpec((nwin, SUBLANES, LANES), tile3)],
        out_shape=[jax.ShapeDtypeStruct((t, d), F32), jax.ShapeDtypeStruct((t, d), BF16),
                   jax.ShapeDtypeStruct((t, LANES), I32), jax.ShapeDtypeStruct((t // tm, SUBLANES, tm), I32),
                   jax.ShapeDtypeStruct((t, LANES), F32),
                   jax.ShapeDtypeStruct((t // tm, SUBLANES, LANES), F32),
                   jax.ShapeDtypeStruct((t // tm, SUBLANES, LANES), F32)],
        scratch_shapes=[pltpu.VMEM((SUBLANES, width), F32), pltpu.VMEM((SUBLANES, width), F32),
                        pltpu.VMEM((SUBLANES, LANES), F32), pltpu.VMEM((tb, d), BF16), pltpu.VMEM((tb, d), BF16)],
        compiler_params=_params(1),
        name="merge",
    )(x2, y_fox, mk, mv, g1, wcat, bg, cw, xqg, wbr, wout, g2, wrh, wrl, br, ltri, utri)


def _block_rows(tm, n_exp):
    return -(-(TOP_K * tm + (SUBLANES - 1) * n_exp) // MXU_DIM) * MXU_DIM


def _chunked_copies(src_ref, src0, dst_ref, dst0, n, sem, max_rows, wait, fixed_src=False):
    for b in range(max_rows.bit_length() - 1, SUBLANES.bit_length() - 2, -1):
        size = 1 << b
        off = (n >> (b + 1)) << (b + 1)

        @pl.when((n >> b) & 1 == 1)
        def _():
            src = 0 if fixed_src else pl.multiple_of(src0 + off, SUBLANES)
            cp = pltpu.make_async_copy(src_ref.at[pl.ds(src, size)],
                                       dst_ref.at[pl.ds(pl.multiple_of(dst0 + off, SUBLANES), size)], sem)
            if wait:
                cp.wait()
            else:
                cp.start()


def _chunk_classes(tm):
    return [1 << b for b in range(SUBLANES.bit_length() - 1, tm.bit_length())]


def _chunk_table(seg_len, seg_src, seg_dst, tm, n_exp):
    sizes = jnp.asarray(_chunk_classes(tm), I32)
    bit = (seg_len[:, :, None] // sizes) % 2
    off = seg_len[:, :, None] // (2 * sizes) * (2 * sizes)
    slot = jnp.cumsum(bit, axis=1) - 1
    hit = (slot[:, :, None, :] == jnp.arange(n_exp, dtype=I32)[None, None, :, None]) & (bit[:, :, None, :] == 1)
    pack = lambda start: jnp.sum(jnp.where(hit, (start[:, :, None] + off)[:, :, None, :], 0), axis=1)
    nw = seg_len.shape[0]
    cols = [jnp.swapaxes(pack(seg_src), 1, 2).reshape(nw, -1), jnp.swapaxes(pack(seg_dst), 1, 2).reshape(nw, -1),
            jnp.sum(bit, axis=1)]
    tbl = jnp.concatenate(cols, axis=1).astype(I32)
    assert tbl.shape[1] <= CHUNK_TABLE_LEN
    return jnp.pad(tbl, ((0, 0), (0, CHUNK_TABLE_LEN - tbl.shape[1]))).reshape(-1)


def _table_copies(tbl_ref, block_ref, sorted_ref, sem, tm, n_exp, to_sorted, wait):
    sizes = _chunk_classes(tm)
    for c, size in enumerate(sizes):
        def body(i, carry, c=c, size=size):
            blk = block_ref.at[pl.ds(pl.multiple_of(tbl_ref[c * n_exp + i], SUBLANES), size)]
            srt = sorted_ref.at[pl.ds(pl.multiple_of(tbl_ref[(len(sizes) + c) * n_exp + i], SUBLANES), size)]
            cp = pltpu.make_async_copy(blk, srt, sem) if to_sorted else pltpu.make_async_copy(srt, blk, sem)
            if wait:
                cp.wait()
            else:
                cp.start()
            return carry
        lax.fori_loop(0, tbl_ref[2 * len(sizes) * n_exp + c], body, 0)


def _dispatch_kernel(pad_start_ref, pad_cnt_ref, n_used_ref, tbl_ref, tbl_prev_ref, lpost_ref, h2_ref, xs_ref,
                     buf_ref, zero_ref, sems, zsem, *, n_exp, n_tiles):
    w = pl.program_id(0)
    nw = pl.num_programs(0)
    tm = h2_ref.shape[0]
    rows = buf_ref.shape[1]
    slot = w % 2

    def clear(wait):
        def per_expert(e, c):
            _chunked_copies(zero_ref, 0, xs_ref, pad_start_ref[e], pad_cnt_ref[e], zsem, ZERO_ROWS, wait,
                            fixed_src=True)
            return c

        def per_tile(i, c):
            for part in range(TM_EXPERT // ZERO_ROWS):
                cp = pltpu.make_async_copy(
                    zero_ref, xs_ref.at[pl.ds(i * TM_EXPERT + part * ZERO_ROWS, ZERO_ROWS)], zsem)
                if wait:
                    cp.wait()
                else:
                    cp.start()
            return c

        lax.fori_loop(0, n_exp, per_expert, 0)
        lax.fori_loop(n_used_ref[0], n_tiles, per_tile, 0)

    @pl.when(w == 0)
    def _():
        zero_ref[...] = jnp.zeros_like(zero_ref)
        clear(False)

    r_iota = lax.broadcasted_iota(I32, (rows, tm), 0)
    sel = r_iota == lpost_ref[0, 0:1, :]
    for kk in range(1, TOP_K):
        sel = sel | (r_iota == lpost_ref[0, kk:kk + 1, :])
    perm = jnp.where(sel, 1.0, 0.0).astype(BF16)

    buf_ref[slot] = _dot(perm, h2_ref[...])
    _table_copies(tbl_ref, buf_ref.at[slot], xs_ref, sems.at[slot], tm, n_exp, True, False)

    @pl.when(w >= 1)
    def _():
        _table_copies(tbl_prev_ref, buf_ref.at[1 - slot], xs_ref, sems.at[1 - slot], tm, n_exp, True, True)

    @pl.when(w == nw - 1)
    def _():
        _table_copies(tbl_ref, buf_ref.at[slot], xs_ref, sems.at[slot], tm, n_exp, True, True)
        clear(True)


def _dispatch(pad_start, pad_cnt, n_used, chunk_tbl, lpost, h2, n_tiles, n_exp):
    t, d = h2.shape
    tm = lpost.shape[2]
    smem_blk = lambda shift: pl.BlockSpec((CHUNK_TABLE_LEN,), lambda i, *_: (jnp.maximum(i + shift, 0),),
                                          memory_space=pltpu.SMEM)
    grid_spec = pltpu.PrefetchScalarGridSpec(
        num_scalar_prefetch=3,
        grid=(t // tm,),
        in_specs=[smem_blk(0), smem_blk(-1),
                  pl.BlockSpec((1, SUBLANES, tm), lambda i, *_: (i, 0, 0)),
                  pl.BlockSpec((tm, d), lambda i, *_: (i, 0))],
        out_specs=pl.BlockSpec(memory_space=pl.ANY),
        scratch_shapes=[pltpu.VMEM((2, _block_rows(tm, n_exp), d), F32), pltpu.VMEM((ZERO_ROWS, d), F32),
                        pltpu.SemaphoreType.DMA((2,)), pltpu.SemaphoreType.DMA(())],
    )
    return pl.pallas_call(
        functools.partial(_dispatch_kernel, n_exp=n_exp, n_tiles=n_tiles),
        grid_spec=grid_spec,
        out_shape=jax.ShapeDtypeStruct((n_tiles * TM_EXPERT, d), F32),
        compiler_params=_params(1),
        name="dispatch",
    )(pad_start, pad_cnt, n_used, chunk_tbl, chunk_tbl, lpost, h2)


def _expert_kernel(tile_end_ref, wup_ref, bup_ref, wdn_ref, bdn_ref, xs_ref, ys_ref,
                   wup_bf_ref, wdn_bf_ref, xbuf_ref, ybuf_ref, xsems, ysems, *, d_ff, n_tiles):
    e = pl.program_id(0)
    n_exp = pl.num_programs(0)
    tm = xbuf_ref.shape[1]
    n_used = tile_end_ref[n_exp - 1]
    first = jnp.where(e == 0, 0, tile_end_ref[jnp.maximum(e - 1, 0)])
    last = tile_end_ref[e]

    def rows_of(g):
        return pl.ds(pl.multiple_of(g * tm, tm), tm)

    def x_copy(g):
        return pltpu.make_async_copy(xs_ref.at[rows_of(g)], xbuf_ref.at[g % 2], xsems.at[g % 2])

    def y_copy(g):
        return pltpu.make_async_copy(ybuf_ref.at[g % 2], ys_ref.at[rows_of(g)], ysems.at[g % 2])

    @pl.when((e == 0) & (n_used > 0))
    def _():
        x_copy(0).start()

    wup_bf_ref[...] = wup_ref[0].astype(BF16)
    wdn_bf_ref[...] = wdn_ref[0].astype(BF16)

    def tile(g, carry):
        slot = g % 2
        x_copy(g).wait()

        @pl.when(g + 1 < n_used)
        def _():
            x_copy(g + 1).start()

        @pl.when(g >= 2)
        def _():
            y_copy(g - 2).wait()

        h = _dot(xbuf_ref[slot].astype(BF16), wup_bf_ref[...]) + bup_ref[0]
        glu = jnp.minimum(h[:, :d_ff], SWIGLU_LIMIT)
        lin = jnp.clip(h[:, d_ff:], -SWIGLU_LIMIT, SWIGLU_LIMIT)
        a = glu * jax.nn.sigmoid(SWIGLU_ALPHA * glu) * (lin + 1.0)
        ybuf_ref[slot] = _dot(a.astype(BF16), wdn_bf_ref[...]) + bdn_ref[0]
        y_copy(g).start()
        return carry

    lax.fori_loop(first, last, tile, 0)

    @pl.when(e == n_exp - 1)
    def _():
        @pl.when(n_used >= 2)
        def _():
            y_copy(n_used - 2).wait()

        @pl.when(n_used >= 1)
        def _():
            y_copy(n_used - 1).wait()

        ybuf_ref[0] = jnp.zeros(ybuf_ref.shape[1:], F32)

        def clear(g, wait):
            cp = pltpu.make_async_copy(ybuf_ref.at[0], ys_ref.at[rows_of(g)], ysems.at[0])
            if wait:
                cp.wait()
            else:
                cp.start()

        lax.fori_loop(n_used, n_tiles, lambda g, c: (clear(g, False), c)[1], 0)
        lax.fori_loop(n_used, n_tiles, lambda g, c: (clear(g, True), c)[1], 0)


def _experts(tile_end, xs, wup, bup, wdn, bdn):
    n_rows, d = xs.shape
    n_exp, d_ff, _ = wdn.shape
    tm = TM_EXPERT
    exp3 = lambda e, *_: (e, 0, 0)
    grid_spec = pltpu.PrefetchScalarGridSpec(
        num_scalar_prefetch=1,
        grid=(n_exp,),
        in_specs=[pl.BlockSpec((1, d, 2 * d_ff), exp3), pl.BlockSpec((1, 1, 2 * d_ff), exp3),
                  pl.BlockSpec((1, d_ff, d), exp3), pl.BlockSpec((1, 1, d), exp3),
                  pl.BlockSpec(memory_space=pl.ANY)],
        out_specs=pl.BlockSpec(memory_space=pl.ANY),
        scratch_shapes=[pltpu.VMEM((d, 2 * d_ff), BF16), pltpu.VMEM((d_ff, d), BF16),
                        pltpu.VMEM((2, tm, d), F32), pltpu.VMEM((2, tm, d), F32),
                        pltpu.SemaphoreType.DMA((2,)), pltpu.SemaphoreType.DMA((2,))],
    )
    return pl.pallas_call(
        functools.partial(_expert_kernel, d_ff=d_ff, n_tiles=n_rows // tm),
        grid_spec=grid_spec,
        out_shape=jax.ShapeDtypeStruct((n_rows, d), F32),
        compiler_params=_params(1),
        name="experts",
    )(tile_end, wup, bup, wdn, bdn, xs)


def _combine_kernel(tbl_ref, tbl_next_ref, lpos_ref, tw_ref, x1_ref, ys_ref, o_ref, buf_ref, sems, *, n_exp):
    w = pl.program_id(0)
    nw = pl.num_programs(0)
    tm = x1_ref.shape[0]
    rows = buf_ref.shape[1]
    slot = w % 2

    @pl.when(w == 0)
    def _():
        buf_ref[...] = jnp.zeros_like(buf_ref)
        _table_copies(tbl_ref, buf_ref.at[0], ys_ref, sems.at[0], tm, n_exp, False, False)

    @pl.when(w + 1 < nw)
    def _():
        _table_copies(tbl_next_ref, buf_ref.at[1 - slot], ys_ref, sems.at[1 - slot], tm, n_exp, False, False)

    _table_copies(tbl_ref, buf_ref.at[slot], ys_ref, sems.at[slot], tm, n_exp, False, True)

    c_iota = lax.broadcasted_iota(I32, (tm, rows), 1)
    lpos = lpos_ref[...]
    tw = tw_ref[...]
    wperm = jnp.zeros((tm, rows), F32)
    for kk in range(TOP_K):
        wperm = jnp.where(c_iota == lpos[:, kk:kk + 1], tw[:, kk:kk + 1], wperm)
    o_ref[...] = x1_ref[...] + _dot(wperm.astype(BF16), buf_ref[slot].astype(BF16))


def _combine(chunk_tbl, lpos, tw, x1, ys, n_exp):
    t, d = x1.shape
    tm = min(TM_WINDOW, t)
    nw = t // tm
    row = lambda i: (i, 0)
    smem_blk = lambda shift: pl.BlockSpec((CHUNK_TABLE_LEN,), lambda i: (jnp.minimum(i + shift, nw - 1),),
                                          memory_space=pltpu.SMEM)
    return pl.pallas_call(
        functools.partial(_combine_kernel, n_exp=n_exp),
        grid=(nw,),
        in_specs=[smem_blk(0), smem_blk(1),
                  pl.BlockSpec((tm, LANES), row), pl.BlockSpec((tm, LANES), row), pl.BlockSpec((tm, d), row),
                  pl.BlockSpec(memory_space=pl.ANY)],
        out_specs=pl.BlockSpec((tm, d), row),
        out_shape=jax.ShapeDtypeStruct((t, d), F32),
        scratch_shapes=[pltpu.VMEM((2, _block_rows(tm, n_exp), d), F32), pltpu.SemaphoreType.DMA((2,))],
        compiler_params=_params(1),
        name="combine",
    )(chunk_tbl, chunk_tbl, lpos, tw, x1, ys)


def _pad_lanes(a, value=0.0):
    return jnp.pad(a, ((0, 0), (0, LANES - a.shape[-1])), constant_values=value)


def _layer(x, mem, norm1_g, w_in, b_forget, b_gate, fox_q_g, fox_k_g, conv_w, mem_norm_g, w_mem_kv,
           xa_q_g, xa_k_g, w_branch, w_out, norm2_g, w_router, b_router, w_up, b_up, w_down, b_down):
    batch, seq, d = x.shape
    n_mem = mem.shape[1]
    n_branch, width, _ = w_branch.shape
    n_fh = b_forget.shape[0]
    hd = fox_q_g.shape[0]
    xa_hd = xa_q_g.shape[0]
    xa_heads = width // xa_hd
    n_exp = w_router.shape[1]
    assert n_fh * hd == width and n_branch == 3 and conv_w.shape[0] == 3
    assert LANES % hd == 0 and width % MXU_DIM == 0 and n_exp <= LANES and n_fh <= LANES
    fh_pad = -(-n_fh // SUBLANES) * SUBLANES
    t = batch * seq
    x2 = x.reshape(t, d)
    row1 = lambda a: a.reshape(1, -1).astype(F32)

    o_f = 3 * width
    o_conv = o_f + n_fh
    wqkv = w_in[:, :o_f].astype(BF16)
    wf = _pad_lanes(w_in[:, o_f:o_conv]).astype(BF16)
    wcat = w_in[:, o_conv:].astype(BF16)

    mk, mv = _memkv(mem.reshape(batch * n_mem, d), row1(mem_norm_g), w_mem_kv.astype(BF16), row1(xa_k_g),
                    xa_heads, xa_hd)
    q, k, v, cumt = _qkv(x2, row1(norm1_g), wqkv, wf, _pad_lanes(row1(b_forget)),
                         row1(jnp.tile(fox_q_g, n_fh)), row1(jnp.tile(fox_k_g, n_fh)),
                         seq, width, hd, fh_pad)
    y_fox = _fox(q, k, v, cumt, batch, seq, width, hd, fh_pad)

    wr = _pad_lanes(w_router.astype(F32))
    wr_hi = wr.astype(BF16)
    wr_lo = (wr - wr_hi.astype(F32)).astype(BF16)
    x1, h2, lpos, lpost, tw, tcnt, tbase = _merge(
        x2, y_fox, mk, mv, row1(norm1_g), wcat, b_gate.astype(F32), conv_w.astype(F32), row1(xa_q_g),
        w_branch.astype(BF16), w_out.astype(BF16), row1(norm2_g), wr_hi, wr_lo,
        _pad_lanes(row1(b_router), NEG_BIG), seq, n_mem, width, xa_heads, xa_hd)

    seg_len = tcnt[:, 0, :n_exp].astype(I32)
    seg_base = tbase[:, 0, :n_exp].astype(I32)
    counts = seg_base[-1] + seg_len[-1]
    tiles_per = (counts + TM_EXPERT - 1) // TM_EXPERT
    tile_end = jnp.cumsum(tiles_per)
    group_start = (tile_end - tiles_per) * TM_EXPERT
    seg_dst = group_start[None, :] + seg_base
    seg_src = jnp.cumsum(seg_len, axis=1) - seg_len
    n_tiles = -(-(t * TOP_K + (SUBLANES - 1) * seg_len.size) // TM_EXPERT) + n_exp
    n_used = tile_end[-1:].astype(I32)
    pad_start = (group_start + counts).astype(I32)
    pad_cnt = (tiles_per * TM_EXPERT - counts).astype(I32)
    chunk_tbl = _chunk_table(seg_len, seg_src.astype(I32), seg_dst.astype(I32), lpost.shape[2], n_exp)

    xs = _dispatch(pad_start, pad_cnt, n_used, chunk_tbl, lpost, h2, n_tiles, n_exp)
    ys = _experts(tile_end.astype(I32), xs, w_up.astype(F32), b_up[:, None, :].astype(F32),
                  w_down.astype(F32), b_down[:, None, :].astype(F32))
    out = _combine(chunk_tbl, lpos, tw, x1, ys, n_exp)
    return out.reshape(batch, seq, d)


def kernel(x, mem, norm1_g, w_in, b_forget, b_gate, fox_q_g, fox_k_g, conv_w, mem_norm_g, w_mem_kv, xa_q_g,
           xa_k_g, w_branch, w_out, norm2_g, w_router, b_router, w_up, b_up, w_down, b_down):
    for l in range(norm1_g.shape[0]):
        x = _layer(x, mem, norm1_g[l], w_in[l], b_forget[l], b_gate[l], fox_q_g[l], fox_k_g[l], conv_w[l],
                   mem_norm_g[l], w_mem_kv[l], xa_q_g[l], xa_k_g[l], w_branch[l], w_out[l], norm2_g[l],
                   w_router[l], b_router[l], w_up[l], b_up[l], w_down[l], b_down[l])
    return x
```

```python
import functools

import jax
import jax.numpy as jnp
from jax import lax
from jax.experimental import pallas as pl
from jax.experimental.pallas import tpu as pltpu

F32 = jnp.float32
BF16 = jnp.bfloat16
I32 = jnp.int32

NORM_EPS = 1e-5
QK_EPS = 1e-6
TOP_K = 4
SWIGLU_ALPHA = 1.702
SWIGLU_LIMIT = 7.0
NEG_BIG = -1e30
LOG2_E = 1.4426950408889634

LANES = 128
SUBLANES = 8
MXU_DIM = 256
VMEM_LIMIT_BYTES = 56 * 1024 * 1024

TM_QKV = 512
TQ_FOX = 512
TM_WINDOW = 256
MERGE_WINDOWS = 2
ROUTE_WINDOWS = 2
TM_EXPERT = 512
ZERO_ROWS = 256
CHUNK_TABLE_LEN = 512


def _params(n_axes):
    return pltpu.CompilerParams(dimension_semantics=("arbitrary",) * n_axes,
                                vmem_limit_bytes=VMEM_LIMIT_BYTES)


def _const_spec(shape):
    zeros = (0,) * len(shape)
    return pl.BlockSpec(shape, lambda *_: zeros, pipeline_mode=pl.Buffered(1))


def _rms(x, g, eps):
    return x * lax.rsqrt(jnp.mean(x * x, axis=-1, keepdims=True) + eps) * g


def _split_bf16(x):
    hi = x.astype(BF16)
    lo = (x - hi.astype(F32)).astype(BF16)
    return hi, lo


def _dot(a, b):
    return jnp.dot(a, b, preferred_element_type=F32)


def _dot_nt(a, b):
    return lax.dot_general(a, b, (((1,), (1,)), ((), ())), preferred_element_type=F32)


def _memkv_kernel(mem_ref, g_ref, w_ref, kg_ref, mk_ref, mv_ref, *, n_heads, hd):
    y = _rms(mem_ref[...], g_ref[...], NORM_EPS)
    kv = _dot(y.astype(BF16), w_ref[...])
    width = n_heads * hd
    for h in range(n_heads):
        sl = slice(h * hd, (h + 1) * hd)
        mk_ref[:, sl] = _rms(kv[:, sl], kg_ref[...], QK_EPS).astype(BF16)
    mv_ref[...] = kv[:, width:].astype(BF16)


def _memkv(mem2, g, w_kv, kg, n_heads, hd):
    rows, d = mem2.shape
    width = n_heads * hd
    tm = min(rows, 512)
    return pl.pallas_call(
        functools.partial(_memkv_kernel, n_heads=n_heads, hd=hd),
        grid=(rows // tm,),
        in_specs=[pl.BlockSpec((tm, d), lambda i: (i, 0)),
                  _const_spec((1, d)), _const_spec((d, 2 * width)), _const_spec((1, hd))],
        out_specs=[pl.BlockSpec((tm, width), lambda i: (i, 0)),
                   pl.BlockSpec((tm, width), lambda i: (i, 0))],
        out_shape=[jax.ShapeDtypeStruct((rows, width), BF16)] * 2,
        compiler_params=_params(1),
        name="memkv",
    )(mem2, g, w_kv, kg)


def _qkv_kernel(x_ref, g1_ref, wqkv_ref, wf_ref, bf_ref, gq_ref, gk_ref, bd_ref, tri_ref,
                q_ref, k_ref, v_ref, cumt_ref, carry_ref,
                *, tiles_per_batch, width, hd, fh_pad):
    i = pl.program_id(0)

    @pl.when(i % tiles_per_batch == 0)
    def _():
        carry_ref[...] = jnp.zeros_like(carry_ref)

    hb = _rms(x_ref[...], g1_ref[...], NORM_EPS).astype(BF16)
    qkv = _dot(hb, wqkv_ref[...])

    def head_norm(z, g):
        sq = (z * z).astype(BF16)
        parts = []
        for c in range(width // MXU_DIM):
            sl = slice(c * MXU_DIM, (c + 1) * MXU_DIM)
            parts.append(_dot(sq[:, sl], bd_ref[...]))
        ms = jnp.concatenate(parts, axis=-1) * (1.0 / hd)
        return z * lax.rsqrt(ms + QK_EPS) * g

    q = head_norm(qkv[:, :width], gq_ref[...]) * (hd ** -0.5 * LOG2_E)
    k = head_norm(qkv[:, width:2 * width], gk_ref[...])
    q_ref[...] = q.astype(BF16)
    k_ref[...] = k.astype(BF16)
    v_ref[...] = qkv[:, 2 * width:].astype(BF16)

    z = _dot(hb, wf_ref[...]) + bf_ref[...]
    log_f = jnp.minimum(z, 0.0) - jnp.log(1.0 + jnp.exp(-jnp.abs(z)))
    hi, lo = _split_bf16(log_f)
    cum = _dot(tri_ref[...], hi) + _dot(tri_ref[...], lo) + carry_ref[0:1, :]
    tm = cum.shape[0]
    carry_ref[...] = jnp.broadcast_to(cum[tm - 1:tm, :], carry_ref.shape)
    cumt_ref[...] = (cum * LOG2_E).T[:fh_pad, :]


def _qkv(x2, g1, wqkv, wf, bf, gq_t, gk_t, seq, width, hd, fh_pad):
    t, d = x2.shape
    tm = min(TM_QKV, seq)
    bd = (jnp.arange(MXU_DIM)[:, None] // hd == jnp.arange(MXU_DIM)[None, :] // hd).astype(BF16)
    tri = (jnp.arange(tm)[:, None] >= jnp.arange(tm)[None, :]).astype(BF16)
    row = lambda i: (i, 0)
    return pl.pallas_call(
        functools.partial(_qkv_kernel, tiles_per_batch=seq // tm, width=width, hd=hd, fh_pad=fh_pad),
        grid=(t // tm,),
        in_specs=[pl.BlockSpec((tm, d), row), _const_spec((1, d)), _const_spec((d, 3 * width)),
                  _const_spec((d, LANES)), _const_spec((1, LANES)), _const_spec((1, width)),
                  _const_spec((1, width)), _const_spec((MXU_DIM, MXU_DIM)), _const_spec((tm, tm))],
        out_specs=[pl.BlockSpec((tm, width), row)] * 3 + [pl.BlockSpec((fh_pad, tm), lambda i: (0, i))],
        out_shape=[jax.ShapeDtypeStruct((t, width), BF16)] * 3 + [jax.ShapeDtypeStruct((fh_pad, t), F32)],
        scratch_shapes=[pltpu.VMEM((SUBLANES, LANES), F32)],
        compiler_params=_params(1),
        name="qkv",
    )(x2, g1, wqkv, wf, bf, gq_t, gk_t, bd, tri)


def _fox_kernel(q_ref, k_ref, v_ref, cumt_ref, o_ref, s_ref, *, tq, hd):
    hp = pl.program_id(1)
    seq = q_ref.shape[0]
    nh = LANES // hd
    lane = lax.broadcasted_iota(I32, (1, LANES), 1)
    on_or_below_diag = lax.broadcasted_iota(I32, (tq, tq), 1) <= lax.broadcasted_iota(I32, (tq, tq), 0)
    heads = [(hp * nh + hh, (lane >= hh * hd) & (lane < (hh + 1) * hd)) for hh in range(nh)]

    def scores(i, j, slot):
        q = q_ref[i * tq:(i + 1) * tq, :]
        kj = k_ref[j * tq:(j + 1) * tq, :]
        for hh, (h, in_head) in enumerate(heads):
            qh = jnp.where(in_head, q, jnp.zeros_like(q))
            s_ref[slot, hh] = _dot_nt(qh, kj) - cumt_ref[pl.ds(h, 1), j * tq:(j + 1) * tq]

    def consume(j, slot, carry, masked):
        vj = v_ref[j * tq:(j + 1) * tq, :]
        out = []
        for hh, ((_, in_head), (m, acc)) in enumerate(zip(heads, carry)):
            vh = jnp.where(in_head, vj, jnp.ones_like(vj))
            s = s_ref[slot, hh]
            if masked:
                s = jnp.where(on_or_below_diag, s, NEG_BIG)
            m_new = jnp.maximum(m, jnp.max(s, axis=-1, keepdims=True))
            p = jnp.exp2(s - m_new)
            out.append((m_new, jnp.exp2(m - m_new) * acc + _dot(p.astype(BF16), vh)))
        return tuple(out)

    tiles = [(i, j) for i in range(seq // tq) for j in range(i + 1)]
    scores(*tiles[0], 0)
    carry = None
    for n, (i, j) in enumerate(tiles):
        if n + 1 < len(tiles):
            scores(*tiles[n + 1], (n + 1) % 2)
        if j == 0:
            carry = tuple((jnp.full((tq, 1), NEG_BIG, F32), jnp.zeros((tq, LANES), F32)) for _ in heads)
        carry = consume(j, n % 2, carry, j == i)
        if j == i:
            o = None
            for hh in range(nh - 1, -1, -1):
                acc = carry[hh][1]
                other = ((hh + 1) % nh) * hd
                val = acc / acc[:, other:other + 1]
                o = val if o is None else jnp.where(lane < (hh + 1) * hd, val, o)
            o_ref[i * tq:(i + 1) * tq, :] = o.astype(BF16)


def _fox(q, k, v, cumt, batch, seq, width, hd, fh_pad):
    t = q.shape[0]
    tq = min(TQ_FOX, seq)
    assert seq % tq == 0
    seqmap = lambda b, hp: (b, hp)
    return pl.pallas_call(
        functools.partial(_fox_kernel, tq=tq, hd=hd),
        grid=(batch, width // LANES),
        in_specs=[pl.BlockSpec((seq, LANES), seqmap), pl.BlockSpec((seq, LANES), seqmap),
                  pl.BlockSpec((seq, LANES), seqmap), pl.BlockSpec((fh_pad, seq), lambda b, hp: (0, b))],
        out_specs=pl.BlockSpec((seq, LANES), seqmap),
        out_shape=jax.ShapeDtypeStruct((t, width), BF16),
        scratch_shapes=[pltpu.VMEM((2, LANES // hd, tq, tq), F32)],
        compiler_params=_params(2),
        name="fox",
    )(q, k, v, cumt)


def _merge_kernel(x_ref, yf_ref, mk_ref, mv_ref, g1_ref, wcat_ref, bg_ref, cw_ref, xqg_ref,
                  wbr_ref, wout_ref, g2_ref, wrh_ref, wrl_ref, br_ref, ltri_ref, utri_ref,
                  x1_ref, h2_ref, lpos_ref, lpost_ref, tw_ref, tcnt_ref, tbase_ref,
                  prev_ref, halo_ref, carry_ref, hi_ref, lo_ref, *, steps_per_batch, width, xa_heads, xa_hd):
    i = pl.program_id(0)
    nt = pl.num_programs(0) - 1
    tm = ltri_ref.shape[0]
    n_win = x_ref.shape[0] // tm

    @pl.when(i == 0)
    def _():
        prev_ref[...] = jnp.zeros_like(prev_ref)
        halo_ref[...] = jnp.zeros_like(halo_ref)
        carry_ref[...] = jnp.zeros_like(carry_ref)
        hi_ref[...] = jnp.zeros_like(hi_ref)
        lo_ref[...] = jnp.zeros_like(lo_ref)

    routed = [_route(hi_ref[k * tm:(k + 1) * tm, :], lo_ref[k * tm:(k + 1) * tm, :], wrh_ref, wrl_ref, br_ref)
              for k in range(n_win)]

    halo = jnp.where(i < nt, jnp.where(i % steps_per_batch == 0, 0.0, prev_ref[...]), halo_ref[...])
    halo_ref[...] = halo

    for k in range(n_win):
        rows = slice(k * tm, (k + 1) * tm)
        place = functools.partial(_place, i, *routed[k], ltri_ref, utri_ref, lpos_ref, lpost_ref, tw_ref,
                                  tcnt_ref, tbase_ref, carry_ref, rows, k)
        halo = _mix_window(x_ref, yf_ref, mk_ref, mv_ref, g1_ref, wcat_ref, bg_ref, cw_ref, xqg_ref, wbr_ref,
                           wout_ref, g2_ref, x1_ref, h2_ref, hi_ref, lo_ref, rows, halo, place,
                           width=width, xa_heads=xa_heads, xa_hd=xa_hd)
    prev_ref[...] = halo


def _mix_window(x_ref, yf_ref, mk_ref, mv_ref, g1_ref, wcat_ref, bg_ref, cw_ref, xqg_ref, wbr_ref, wout_ref,
                g2_ref, x1_ref, h2_ref, hi_ref, lo_ref, rows, halo, place, *, width, xa_heads, xa_hd):
    tm = rows.stop - rows.start
    d = x_ref.shape[1]
    x = x_ref[rows, :]
    hb = _rms(x, g1_ref[...], NORM_EPS).astype(BF16)
    pc = _dot(hb, wcat_ref[...])

    uc = pc[:, :width] * pc[:, width:2 * width]
    row = lax.broadcasted_iota(I32, (tm, 1), 0)
    p1 = halo[SUBLANES - 1:SUBLANES, :]
    p2 = halo[SUBLANES - 2:SUBLANES - 1, :]
    m1 = jnp.where(row == 0, p1, pltpu.roll(uc, 1, 0))
    m2 = jnp.where(row == 0, p2, jnp.where(row == 1, p1, pltpu.roll(uc, 2, 0)))
    y_conv = pc[:, 2 * width:3 * width] * (cw_ref[0:1, :] * m2 + cw_ref[1:2, :] * m1 + cw_ref[2:3, :] * uc)

    ys = []
    for h in range(xa_heads):
        sl = slice(3 * width + h * xa_hd, 3 * width + (h + 1) * xa_hd)
        ml = slice(h * xa_hd, (h + 1) * xa_hd)
        qh = _rms(pc[:, sl], xqg_ref[...], QK_EPS) * (xa_hd ** -0.5)
        s = _dot_nt(qh.astype(BF16), mk_ref[:, ml])
        p = jnp.exp(s - jnp.max(s, axis=-1, keepdims=True))
        ys.append(_dot(p.astype(BF16), mv_ref[:, ml]) / jnp.sum(p, axis=-1, keepdims=True))
    y_mem = jnp.concatenate(ys, axis=-1)

    place()

    merged = jnp.zeros((tm, d), F32)
    for n, yb in enumerate((yf_ref[rows, :], y_conv.astype(BF16), y_mem.astype(BF16))):
        gate = jax.nn.sigmoid(pc[:, 4 * width + n * d:4 * width + (n + 1) * d] + bg_ref[n:n + 1, :])
        merged = merged + gate * _dot(yb, wbr_ref[n])
    x1 = x + _dot(merged.astype(BF16), wout_ref[...])
    x1_ref[rows, :] = x1
    hi, lo = _split_bf16(_rms(x1, g2_ref[...], NORM_EPS))
    h2_ref[rows, :] = hi
    hi_ref[rows, :] = hi
    lo_ref[rows, :] = lo
    return uc[tm - SUBLANES:tm, :]


def _route(hi, lo, wrh_ref, wrl_ref, br_ref):
    tm = hi.shape[0]
    logits = _dot(hi, wrh_ref[...]) + _dot(hi, wrl_ref[...]) + _dot(lo, wrh_ref[...]) + br_ref[...]
    lane = lax.broadcasted_iota(I32, (tm, LANES), 1)
    vals, idxs = [], []
    onehot = jnp.zeros((tm, LANES), F32)
    for _ in range(TOP_K):
        m = jnp.max(logits, axis=-1, keepdims=True)
        idx = jnp.min(jnp.where(logits == m, lane, LANES), axis=-1, keepdims=True)
        sel = lane == idx
        logits = jnp.where(sel, -jnp.inf, logits)
        onehot = onehot + sel.astype(F32)
        vals.append(m)
        idxs.append(idx)
    es = [jnp.exp(v - vals[0]) for v in vals]
    den = es[0]
    for e in es[1:]:
        den = den + e
    return onehot, idxs, [e / den for e in es]


def _place(i, onehot, idxs, tws, ltri_ref, utri_ref, lpos_ref, lpost_ref, tw_ref, tcnt_ref, tbase_ref,
           carry_ref, rows, k):
    tm = onehot.shape[0]
    lane = lax.broadcasted_iota(I32, (tm, LANES), 1)
    cnt = jnp.sum(onehot, axis=0, keepdims=True)
    span = jnp.floor((cnt + (SUBLANES - 1)) * (1.0 / SUBLANES)) * SUBLANES
    s_hi, s_lo = _split_bf16(jnp.broadcast_to(span, (SUBLANES, LANES)))
    seg_start = (_dot(s_hi, utri_ref[...]) + _dot(s_lo, utri_ref[...]))[0:1, :]
    where_to = _dot(ltri_ref[...], onehot.astype(BF16)) + seg_start
    tcnt_ref[k] = jnp.broadcast_to(span, (SUBLANES, LANES))
    tbase_ref[k] = carry_ref[...]
    carry_ref[...] = jnp.where(i > 0, carry_ref[...] + span, 0.0)

    lpos_o = jnp.zeros((tm, LANES), F32)
    tw_o = jnp.zeros((tm, LANES), F32)
    for kk in range(TOP_K):
        lpos = jnp.sum(jnp.where(lane == idxs[kk], where_to, 0.0), axis=-1, keepdims=True)
        lpos_o = jnp.where(lane == kk, lpos, lpos_o)
        tw_o = jnp.where(lane == kk, tws[kk], tw_o)
    lpos_ref[rows, :] = lpos_o.astype(I32)
    lpost_ref[k] = lpos_o.T[:SUBLANES, :].astype(I32)
    tw_ref[rows, :] = tw_o


def _merge(x2, y_fox, mk, mv, g1, wcat, bg, cw, xqg, wbr, wout, g2, wrh, wrl, br,
           seq, n_mem, width, xa_heads, xa_hd):
    t, d = x2.shape
    tm = min(TM_WINDOW, seq)
    nwin = min(MERGE_WINDOWS, seq // tm)
    tb = nwin * tm
    nt = t // tb
    tpb = seq // tb
    ltri = (jnp.arange(tm)[:, None] > jnp.arange(tm)[None, :]).astype(BF16)
    utri = (jnp.arange(LANES)[:, None] < jnp.arange(LANES)[None, :]).astype(BF16)
    mix = lambda i: jnp.minimum(i, nt - 1)
    rte = lambda i: jnp.maximum(i - 1, 0)
    row = lambda i: (mix(i), 0)
    rrow = lambda i: (rte(i), 0)
    tile3 = lambda i: (rte(i), 0, 0)
    mem_map = lambda i: (mix(i) // tpb, 0)
    return pl.pallas_call(
        functools.partial(_merge_kernel, steps_per_batch=tpb, width=width, xa_heads=xa_heads, xa_hd=xa_hd),
        grid=(nt + 1,),
        in_specs=[pl.BlockSpec((tb, d), row), pl.BlockSpec((tb, width), row),
                  pl.BlockSpec((n_mem, width), mem_map), pl.BlockSpec((n_mem, width), mem_map),
                  _const_spec((1, d)), _const_spec(wcat.shape), _const_spec(bg.shape), _const_spec(cw.shape),
                  _const_spec((1, xa_hd)), _const_spec(wbr.shape), _const_spec((d, d)), _const_spec((1, d)),
                  _const_spec((d, LANES)), _const_spec((d, LANES)), _const_spec((1, LANES)),
                  _const_spec((tm, tm)), _const_spec((LANES, LANES))],
        out_specs=[pl.BlockSpec((tb, d), row), pl.BlockSpec((tb, d), row),
                   pl.BlockSpec((tb, LANES), rrow), pl.BlockSpec((nwin, SUBLANES, tm), tile3),
                   pl.BlockSpec((tb, LANES), rrow),
                   pl.BlockSpec((nwin, SUBLANES, LANES), tile3), pl.BlockSpec((nwin, SUBLANES, LANES), tile3)],
        out_shape=[jax.ShapeDtypeStruct((t, d), F32), jax.ShapeDtypeStruct((t, d), BF16),
                   jax.ShapeDtypeStruct((t, LANES), I32), jax.ShapeDtypeStruct((t // tm, SUBLANES, tm), I32),
                   jax.ShapeDtypeStruct((t, LANES), F32),
                   jax.ShapeDtypeStruct((t // tm, SUBLANES, LANES), F32),
                   jax.ShapeDtypeStruct((t // tm, SUBLANES, LANES), F32)],
        scratch_shapes=[pltpu.VMEM((SUBLANES, width), F32), pltpu.VMEM((SUBLANES, width), F32),
                        pltpu.VMEM((SUBLANES, LANES), F32), pltpu.VMEM((tb, d), BF16), pltpu.VMEM((tb, d), BF16)],
        compiler_params=_params(1),
        name="merge",
    )(x2, y_fox, mk, mv, g1, wcat, bg, cw, xqg, wbr, wout, g2, wrh, wrl, br, ltri, utri)


def _block_rows(tm, n_exp):
    return -(-(TOP_K * tm + (SUBLANES - 1) * n_exp) // MXU_DIM) * MXU_DIM


def _chunked_copies(src_ref, src0, dst_ref, dst0, n, sem, max_rows, wait, fixed_src=False):
    for b in range(max_rows.bit_length() - 1, SUBLANES.bit_length() - 2, -1):
        size = 1 << b
        off = (n >> (b + 1)) << (b + 1)

        @pl.when((n >> b) & 1 == 1)
        def _():
            src = 0 if fixed_src else pl.multiple_of(src0 + off, SUBLANES)
            cp = pltpu.make_async_copy(src_ref.at[pl.ds(src, size)],
                                       dst_ref.at[pl.ds(pl.multiple_of(dst0 + off, SUBLANES), size)], sem)
            if wait:
                cp.wait()
            else:
                cp.start()


def _chunk_classes(tm):
    return [1 << b for b in range(SUBLANES.bit_length() - 1, tm.bit_length())]


def _chunk_table(seg_len, seg_src, seg_dst, tm, n_exp):
    sizes = jnp.asarray(_chunk_classes(tm), I32)
    bit = (seg_len[:, :, None] // sizes) % 2
    off = seg_len[:, :, None] // (2 * sizes) * (2 * sizes)
    slot = jnp.cumsum(bit, axis=1) - 1
    hit = (slot[:, :, None, :] == jnp.arange(n_exp, dtype=I32)[None, None, :, None]) & (bit[:, :, None, :] == 1)
    pack = lambda start: jnp.sum(jnp.where(hit, (start[:, :, None] + off)[:, :, None, :], 0), axis=1)
    nw = seg_len.shape[0]
    cols = [jnp.swapaxes(pack(seg_src), 1, 2).reshape(nw, -1), jnp.swapaxes(pack(seg_dst), 1, 2).reshape(nw, -1),
            jnp.sum(bit, axis=1)]
    tbl = jnp.concatenate(cols, axis=1).astype(I32)
    assert tbl.shape[1] <= CHUNK_TABLE_LEN
    return jnp.pad(tbl, ((0, 0), (0, CHUNK_TABLE_LEN - tbl.shape[1]))).reshape(-1)


def _table_copies(tbl_ref, k, block_ref, sorted_ref, sem, tm, n_exp, to_sorted, wait):
    sizes = _chunk_classes(tm)
    base = k * CHUNK_TABLE_LEN
    for c, size in enumerate(sizes):
        def body(i, carry, c=c, size=size):
            blk = block_ref.at[pl.ds(pl.multiple_of(tbl_ref[base + c * n_exp + i], SUBLANES), size)]
            srt = sorted_ref.at[
                pl.ds(pl.multiple_of(tbl_ref[base + (len(sizes) + c) * n_exp + i], SUBLANES), size)]
            cp = pltpu.make_async_copy(blk, srt, sem) if to_sorted else pltpu.make_async_copy(srt, blk, sem)
            if wait:
                cp.wait()
            else:
                cp.start()
            return carry
        lax.fori_loop(0, tbl_ref[base + 2 * len(sizes) * n_exp + c], body, 0)


def _dispatch_kernel(pad_start_ref, pad_cnt_ref, n_used_ref, tbl_ref, tbl_prev_ref, lpost_ref, h2_ref, xs_ref,
                     buf_ref, zero_ref, sems, zsem, *, n_exp, n_tiles):
    s = pl.program_id(0)
    ns = pl.num_programs(0)
    n_win, _, tm = lpost_ref.shape
    rows = buf_ref.shape[1]
    mine = (s % 2) * n_win
    other = n_win - mine

    def clear(wait):
        def per_expert(e, c):
            _chunked_copies(zero_ref, 0, xs_ref, pad_start_ref[e], pad_cnt_ref[e], zsem, ZERO_ROWS, wait,
                            fixed_src=True)
            return c

        def per_tile(i, c):
            for part in range(TM_EXPERT // ZERO_ROWS):
                cp = pltpu.make_async_copy(
                    zero_ref, xs_ref.at[pl.ds(i * TM_EXPERT + part * ZERO_ROWS, ZERO_ROWS)], zsem)
                if wait:
                    cp.wait()
                else:
                    cp.start()
            return c

        lax.fori_loop(0, n_exp, per_expert, 0)
        lax.fori_loop(n_used_ref[0], n_tiles, per_tile, 0)

    @pl.when(s == 0)
    def _():
        zero_ref[...] = jnp.zeros_like(zero_ref)
        clear(False)

    r_iota = lax.broadcasted_iota(I32, (rows, tm), 0)
    for k in range(n_win):
        sel = r_iota == lpost_ref[k, 0:1, :]
        for kk in range(1, TOP_K):
            sel = sel | (r_iota == lpost_ref[k, kk:kk + 1, :])
        perm = jnp.where(sel, 1.0, 0.0).astype(BF16)
        buf_ref[mine + k] = _dot(perm, h2_ref[k * tm:(k + 1) * tm, :])

    def window_copies(table_ref, first_slot, wait):
        for k in range(n_win):
            _table_copies(table_ref, k, buf_ref.at[first_slot + k], xs_ref, sems.at[first_slot + k], tm, n_exp,
                          True, wait)

    window_copies(tbl_ref, mine, False)

    @pl.when(s >= 1)
    def _():
        window_copies(tbl_prev_ref, other, True)

    @pl.when(s == ns - 1)
    def _():
        window_copies(tbl_ref, mine, True)
        clear(True)


def _dispatch(pad_start, pad_cnt, n_used, chunk_tbl, lpost, h2, n_tiles, n_exp):
    t, d = h2.shape
    nw, _, tm = lpost.shape
    n_win = min(ROUTE_WINDOWS, nw)
    smem_blk = lambda shift: pl.BlockSpec((n_win * CHUNK_TABLE_LEN,), lambda i, *_: (jnp.maximum(i + shift, 0),),
                                          memory_space=pltpu.SMEM)
    grid_spec = pltpu.PrefetchScalarGridSpec(
        num_scalar_prefetch=3,
        grid=(nw // n_win,),
        in_specs=[smem_blk(0), smem_blk(-1),
                  pl.BlockSpec((n_win, SUBLANES, tm), lambda i, *_: (i, 0, 0)),
                  pl.BlockSpec((n_win * tm, d), lambda i, *_: (i, 0))],
        out_specs=pl.BlockSpec(memory_space=pl.ANY),
        scratch_shapes=[pltpu.VMEM((2 * n_win, _block_rows(tm, n_exp), d), F32), pltpu.VMEM((ZERO_ROWS, d), F32),
                        pltpu.SemaphoreType.DMA((2 * n_win,)), pltpu.SemaphoreType.DMA(())],
    )
    return pl.pallas_call(
        functools.partial(_dispatch_kernel, n_exp=n_exp, n_tiles=n_tiles),
        grid_spec=grid_spec,
        out_shape=jax.ShapeDtypeStruct((n_tiles * TM_EXPERT, d), F32),
        compiler_params=_params(1),
        name="dispatch",
    )(pad_start, pad_cnt, n_used, chunk_tbl, chunk_tbl, lpost, h2)


def _expert_kernel(tile_end_ref, wup_ref, bup_ref, wdn_ref, bdn_ref, xs_ref, ys_ref,
                   wup_bf_ref, wdn_bf_ref, xbuf_ref, ybuf_ref, xsems, ysems, *, d_ff, n_tiles):
    e = pl.program_id(0)
    n_exp = pl.num_programs(0)
    tm = xbuf_ref.shape[1]
    n_used = tile_end_ref[n_exp - 1]
    first = jnp.where(e == 0, 0, tile_end_ref[jnp.maximum(e - 1, 0)])
    last = tile_end_ref[e]

    def rows_of(g):
        return pl.ds(pl.multiple_of(g * tm, tm), tm)

    def x_copy(g):
        return pltpu.make_async_copy(xs_ref.at[rows_of(g)], xbuf_ref.at[g % 2], xsems.at[g % 2])

    def y_copy(g):
        return pltpu.make_async_copy(ybuf_ref.at[g % 2], ys_ref.at[rows_of(g)], ysems.at[g % 2])

    @pl.when((e == 0) & (n_used > 0))
    def _():
        x_copy(0).start()

    wup_bf_ref[...] = wup_ref[0].astype(BF16)
    wdn_bf_ref[...] = wdn_ref[0].astype(BF16)

    def tile(g, carry):
        slot = g % 2
        x_copy(g).wait()

        @pl.when(g + 1 < n_used)
        def _():
            x_copy(g + 1).start()

        @pl.when(g >= 2)
        def _():
            y_copy(g - 2).wait()

        h = _dot(xbuf_ref[slot].astype(BF16), wup_bf_ref[...]) + bup_ref[0]
        glu = jnp.minimum(h[:, :d_ff], SWIGLU_LIMIT)
        lin = jnp.clip(h[:, d_ff:], -SWIGLU_LIMIT, SWIGLU_LIMIT)
        a = glu * jax.nn.sigmoid(SWIGLU_ALPHA * glu) * (lin + 1.0)
        ybuf_ref[slot] = _dot(a.astype(BF16), wdn_bf_ref[...]) + bdn_ref[0]
        y_copy(g).start()
        return carry

    lax.fori_loop(first, last, tile, 0)

    @pl.when(e == n_exp - 1)
    def _():
        @pl.when(n_used >= 2)
        def _():
            y_copy(n_used - 2).wait()

        @pl.when(n_used >= 1)
        def _():
            y_copy(n_used - 1).wait()

        ybuf_ref[0] = jnp.zeros(ybuf_ref.shape[1:], F32)

        def clear(g, wait):
            cp = pltpu.make_async_copy(ybuf_ref.at[0], ys_ref.at[rows_of(g)], ysems.at[0])
            if wait:
                cp.wait()
            else:
                cp.start()

        lax.fori_loop(n_used, n_tiles, lambda g, c: (clear(g, False), c)[1], 0)
        lax.fori_loop(n_used, n_tiles, lambda g, c: (clear(g, True), c)[1], 0)


def _experts(tile_end, xs, wup, bup, wdn, bdn):
    n_rows, d = xs.shape
    n_exp, d_ff, _ = wdn.shape
    tm = TM_EXPERT
    exp3 = lambda e, *_: (e, 0, 0)
    grid_spec = pltpu.PrefetchScalarGridSpec(
        num_scalar_prefetch=1,
        grid=(n_exp,),
        in_specs=[pl.BlockSpec((1, d, 2 * d_ff), exp3), pl.BlockSpec((1, 1, 2 * d_ff), exp3),
                  pl.BlockSpec((1, d_ff, d), exp3), pl.BlockSpec((1, 1, d), exp3),
                  pl.BlockSpec(memory_space=pl.ANY)],
        out_specs=pl.BlockSpec(memory_space=pl.ANY),
        scratch_shapes=[pltpu.VMEM((d, 2 * d_ff), BF16), pltpu.VMEM((d_ff, d), BF16),
                        pltpu.VMEM((2, tm, d), F32), pltpu.VMEM((2, tm, d), F32),
                        pltpu.SemaphoreType.DMA((2,)), pltpu.SemaphoreType.DMA((2,))],
    )
    return pl.pallas_call(
        functools.partial(_expert_kernel, d_ff=d_ff, n_tiles=n_rows // tm),
        grid_spec=grid_spec,
        out_shape=jax.ShapeDtypeStruct((n_rows, d), F32),
        compiler_params=_params(1),
        name="experts",
    )(tile_end, wup, bup, wdn, bdn, xs)


def _combine_kernel(tbl_ref, tbl_next_ref, lpos_ref, tw_ref, x1_ref, ys_ref, o_ref, buf_ref, sems,
                    *, n_exp, n_win):
    s = pl.program_id(0)
    ns = pl.num_programs(0)
    tm = x1_ref.shape[0] // n_win
    rows = buf_ref.shape[1]
    mine = (s % 2) * n_win
    other = n_win - mine

    def window_copies(table_ref, first_slot, wait):
        for k in range(n_win):
            _table_copies(table_ref, k, buf_ref.at[first_slot + k], ys_ref, sems.at[first_slot + k], tm, n_exp,
                          False, wait)

    @pl.when(s == 0)
    def _():
        buf_ref[...] = jnp.zeros_like(buf_ref)
        window_copies(tbl_ref, 0, False)

    @pl.when(s + 1 < ns)
    def _():
        window_copies(tbl_next_ref, other, False)

    window_copies(tbl_ref, mine, True)

    c_iota = lax.broadcasted_iota(I32, (tm, rows), 1)
    for k in range(n_win):
        sl = slice(k * tm, (k + 1) * tm)
        lpos = lpos_ref[sl, :]
        tw = tw_ref[sl, :]
        wperm = jnp.zeros((tm, rows), F32)
        for kk in range(TOP_K):
            wperm = jnp.where(c_iota == lpos[:, kk:kk + 1], tw[:, kk:kk + 1], wperm)
        o_ref[sl, :] = x1_ref[sl, :] + _dot(wperm.astype(BF16), buf_ref[mine + k].astype(BF16))


def _combine(chunk_tbl, lpos, tw, x1, ys, n_exp):
    t, d = x1.shape
    tm = min(TM_WINDOW, t)
    n_win = min(ROUTE_WINDOWS, t // tm)
    ns = t // (n_win * tm)
    row = lambda i: (i, 0)
    smem_blk = lambda shift: pl.BlockSpec((n_win * CHUNK_TABLE_LEN,), lambda i: (jnp.minimum(i + shift, ns - 1),),
                                          memory_space=pltpu.SMEM)
    return pl.pallas_call(
        functools.partial(_combine_kernel, n_exp=n_exp, n_win=n_win),
        grid=(ns,),
        in_specs=[smem_blk(0), smem_blk(1),
                  pl.BlockSpec((n_win * tm, LANES), row), pl.BlockSpec((n_win * tm, LANES), row),
                  pl.BlockSpec((n_win * tm, d), row), pl.BlockSpec(memory_space=pl.ANY)],
        out_specs=pl.BlockSpec((n_win * tm, d), row),
        out_shape=jax.ShapeDtypeStruct((t, d), F32),
        scratch_shapes=[pltpu.VMEM((2 * n_win, _block_rows(tm, n_exp), d), F32),
                        pltpu.SemaphoreType.DMA((2 * n_win,))],
        compiler_params=_params(1),
        name="combine",
    )(chunk_tbl, chunk_tbl, lpos, tw, x1, ys)


def _pad_lanes(a, value=0.0):
    return jnp.pad(a, ((0, 0), (0, LANES - a.shape[-1])), constant_values=value)


def _layer(x, mem, norm1_g, w_in, b_forget, b_gate, fox_q_g, fox_k_g, conv_w, mem_norm_g, w_mem_kv,
           xa_q_g, xa_k_g, w_branch, w_out, norm2_g, w_router, b_router, w_up, b_up, w_down, b_down):
    batch, seq, d = x.shape
    n_mem = mem.shape[1]
    n_branch, width, _ = w_branch.shape
    n_fh = b_forget.shape[0]
    hd = fox_q_g.shape[0]
    xa_hd = xa_q_g.shape[0]
    xa_heads = width // xa_hd
    n_exp = w_router.shape[1]
    assert n_fh * hd == width and n_branch == 3 and conv_w.shape[0] == 3
    assert LANES % hd == 0 and width % MXU_DIM == 0 and n_exp <= LANES and n_fh <= LANES
    fh_pad = -(-n_fh // SUBLANES) * SUBLANES
    t = batch * seq
    x2 = x.reshape(t, d)
    row1 = lambda a: a.reshape(1, -1).astype(F32)

    o_f = 3 * width
    o_conv = o_f + n_fh
    wqkv = w_in[:, :o_f].astype(BF16)
    wf = _pad_lanes(w_in[:, o_f:o_conv]).astype(BF16)
    wcat = w_in[:, o_conv:].astype(BF16)

    mk, mv = _memkv(mem.reshape(batch * n_mem, d), row1(mem_norm_g), w_mem_kv.astype(BF16), row1(xa_k_g),
                    xa_heads, xa_hd)
    q, k, v, cumt = _qkv(x2, row1(norm1_g), wqkv, wf, _pad_lanes(row1(b_forget)),
                         row1(jnp.tile(fox_q_g, n_fh)), row1(jnp.tile(fox_k_g, n_fh)),
                         seq, width, hd, fh_pad)
    y_fox = _fox(q, k, v, cumt, batch, seq, width, hd, fh_pad)

    wr = _pad_lanes(w_router.astype(F32))
    wr_hi = wr.astype(BF16)
    wr_lo = (wr - wr_hi.astype(F32)).astype(BF16)
    x1, h2, lpos, lpost, tw, tcnt, tbase = _merge(
        x2, y_fox, mk, mv, row1(norm1_g), wcat, b_gate.astype(F32), conv_w.astype(F32), row1(xa_q_g),
        w_branch.astype(BF16), w_out.astype(BF16), row1(norm2_g), wr_hi, wr_lo,
        _pad_lanes(row1(b_router), NEG_BIG), seq, n_mem, width, xa_heads, xa_hd)

    seg_len = tcnt[:, 0, :n_exp].astype(I32)
    seg_base = tbase[:, 0, :n_exp].astype(I32)
    counts = seg_base[-1] + seg_len[-1]
    tiles_per = (counts + TM_EXPERT - 1) // TM_EXPERT
    tile_end = jnp.cumsum(tiles_per)
    group_start = (tile_end - tiles_per) * TM_EXPERT
    seg_dst = group_start[None, :] + seg_base
    seg_src = jnp.cumsum(seg_len, axis=1) - seg_len
    n_tiles = -(-(t * TOP_K + (SUBLANES - 1) * seg_len.size) // TM_EXPERT) + n_exp
    n_used = tile_end[-1:].astype(I32)
    pad_start = (group_start + counts).astype(I32)
    pad_cnt = (tiles_per * TM_EXPERT - counts).astype(I32)
    chunk_tbl = _chunk_table(seg_len, seg_src.astype(I32), seg_dst.astype(I32), lpost.shape[2], n_exp)

    xs = _dispatch(pad_start, pad_cnt, n_used, chunk_tbl, lpost, h2, n_tiles, n_exp)
    ys = _experts(tile_end.astype(I32), xs, w_up.astype(F32), b_up[:, None, :].astype(F32),
                  w_down.astype(F32), b_down[:, None, :].astype(F32))
    out = _combine(chunk_tbl, lpos, tw, x1, ys, n_exp)
    return out.reshape(batch, seq, d)


def kernel(x, mem, norm1_g, w_in, b_forget, b_gate, fox_q_g, fox_k_g, conv_w, mem_norm_g, w_mem_kv, xa_q_g,
           xa_k_g, w_branch, w_out, norm2_g, w_router, b_router, w_up, b_up, w_down, b_down):
    for l in range(norm1_g.shape[0]):
        x = _layer(x, mem, norm1_g[l], w_in[l], b_forget[l], b_gate[l], fox_q_g[l], fox_k_g[l], conv_w[l],
                   mem_norm_g[l], w_mem_kv[l], xa_q_g[l], xa_k_g[l], w_branch[l], w_out[l], norm2_g[l],
                   w_router[l], b_router[l], w_up[l], b_up[l], w_down[l], b_down[l])
    return x
```

```python
import functools

import jax
import jax.numpy as jnp
from jax import lax
from jax.experimental import pallas as pl
from jax.experimental.pallas import tpu as pltpu

F32 = jnp.float32
BF16 = jnp.bfloat16
I32 = jnp.int32

NORM_EPS = 1e-5
QK_EPS = 1e-6
TOP_K = 4
SWIGLU_ALPHA = 1.702
SWIGLU_LIMIT = 7.0
NEG_BIG = -1e30
LOG2_E = 1.4426950408889634

LANES = 128
SUBLANES = 8
MXU_DIM = 256
VMEM_LIMIT_BYTES = 56 * 1024 * 1024

TM_QKV = 512
QKV_CHUNKS = 4
TQ_FOX = 512
TM_WINDOW = 256
MERGE_WINDOWS = 2
ROUTE_WINDOWS = 2
TM_EXPERT = 512
ZERO_ROWS = 256
CHUNK_TABLE_LEN = 512


def _params(n_axes):
    return pltpu.CompilerParams(dimension_semantics=("arbitrary",) * n_axes,
                                vmem_limit_bytes=VMEM_LIMIT_BYTES)


def _const_spec(shape):
    zeros = (0,) * len(shape)
    return pl.BlockSpec(shape, lambda *_: zeros, pipeline_mode=pl.Buffered(1))


def _rms(x, g, eps):
    return x * lax.rsqrt(jnp.mean(x * x, axis=-1, keepdims=True) + eps) * g


def _split_bf16(x):
    hi = x.astype(BF16)
    lo = (x - hi.astype(F32)).astype(BF16)
    return hi, lo


def _dot(a, b):
    return jnp.dot(a, b, preferred_element_type=F32)


def _dot_nt(a, b):
    return lax.dot_general(a, b, (((1,), (1,)), ((), ())), preferred_element_type=F32)


def _memkv_kernel(mem_ref, g_ref, w_ref, kg_ref, mk_ref, mv_ref, *, n_heads, hd):
    y = _rms(mem_ref[...], g_ref[...], NORM_EPS)
    kv = _dot(y.astype(BF16), w_ref[...])
    width = n_heads * hd
    for h in range(n_heads):
        sl = slice(h * hd, (h + 1) * hd)
        mk_ref[:, sl] = _rms(kv[:, sl], kg_ref[...], QK_EPS).astype(BF16)
    mv_ref[...] = kv[:, width:].astype(BF16)


def _memkv(mem2, g, w_kv, kg, n_heads, hd):
    rows, d = mem2.shape
    width = n_heads * hd
    tm = min(rows, 512)
    return pl.pallas_call(
        functools.partial(_memkv_kernel, n_heads=n_heads, hd=hd),
        grid=(rows // tm,),
        in_specs=[pl.BlockSpec((tm, d), lambda i: (i, 0)),
                  _const_spec((1, d)), _const_spec((d, 2 * width)), _const_spec((1, hd))],
        out_specs=[pl.BlockSpec((tm, width), lambda i: (i, 0)),
                   pl.BlockSpec((tm, width), lambda i: (i, 0))],
        out_shape=[jax.ShapeDtypeStruct((rows, width), BF16)] * 2,
        compiler_params=_params(1),
        name="memkv",
    )(mem2, g, w_kv, kg)


def _qkv_kernel(x_ref, g1_ref, wqkv_ref, wf_ref, bf_ref, gq_ref, gk_ref, bd_ref, tri_ref,
                q_ref, k_ref, v_ref, cumt_ref, carry_ref,
                *, tiles_per_batch, width, hd, fh_pad):
    i = pl.program_id(0)
    tm = tri_ref.shape[0]

    @pl.when(i % tiles_per_batch == 0)
    def _():
        carry_ref[...] = jnp.zeros_like(carry_ref)

    def head_norm(z, g):
        sq = (z * z).astype(BF16)
        parts = []
        for c in range(width // MXU_DIM):
            sl = slice(c * MXU_DIM, (c + 1) * MXU_DIM)
            parts.append(_dot(sq[:, sl], bd_ref[...]))
        ms = jnp.concatenate(parts, axis=-1) * (1.0 / hd)
        return z * lax.rsqrt(ms + QK_EPS) * g

    prefix = carry_ref[0:1, :]
    for c in range(x_ref.shape[0] // tm):
        rows = slice(c * tm, (c + 1) * tm)
        hb = _rms(x_ref[rows, :], g1_ref[...], NORM_EPS).astype(BF16)
        qkv = _dot(hb, wqkv_ref[...])
        q = head_norm(qkv[:, :width], gq_ref[...]) * (hd ** -0.5 * LOG2_E)
        k = head_norm(qkv[:, width:2 * width], gk_ref[...])
        q_ref[rows, :] = q.astype(BF16)
        k_ref[rows, :] = k.astype(BF16)
        v_ref[rows, :] = qkv[:, 2 * width:].astype(BF16)

        z = _dot(hb, wf_ref[...]) + bf_ref[...]
        log_f = jnp.minimum(z, 0.0) - jnp.log(1.0 + jnp.exp(-jnp.abs(z)))
        hi, lo = _split_bf16(log_f)
        cum = _dot(tri_ref[...], hi) + _dot(tri_ref[...], lo) + prefix
        prefix = cum[tm - 1:tm, :]
        cumt_ref[:, rows] = (cum * LOG2_E).T[:fh_pad, :]
    carry_ref[...] = jnp.broadcast_to(prefix, carry_ref.shape)


def _qkv(x2, g1, wqkv, wf, bf, gq_t, gk_t, seq, width, hd, fh_pad):
    t, d = x2.shape
    tm = min(TM_QKV, seq)
    tb = min(QKV_CHUNKS * tm, seq)
    bd = (jnp.arange(MXU_DIM)[:, None] // hd == jnp.arange(MXU_DIM)[None, :] // hd).astype(BF16)
    tri = (jnp.arange(tm)[:, None] >= jnp.arange(tm)[None, :]).astype(BF16)
    row = lambda i: (i, 0)
    return pl.pallas_call(
        functools.partial(_qkv_kernel, tiles_per_batch=seq // tb, width=width, hd=hd, fh_pad=fh_pad),
        grid=(t // tb,),
        in_specs=[pl.BlockSpec((tb, d), row), _const_spec((1, d)), _const_spec((d, 3 * width)),
                  _const_spec((d, LANES)), _const_spec((1, LANES)), _const_spec((1, width)),
                  _const_spec((1, width)), _const_spec((MXU_DIM, MXU_DIM)), _const_spec((tm, tm))],
        out_specs=[pl.BlockSpec((tb, width), row)] * 3 + [pl.BlockSpec((fh_pad, tb), lambda i: (0, i))],
        out_shape=[jax.ShapeDtypeStruct((t, width), BF16)] * 3 + [jax.ShapeDtypeStruct((fh_pad, t), F32)],
        scratch_shapes=[pltpu.VMEM((SUBLANES, LANES), F32)],
        compiler_params=_params(1),
        name="qkv",
    )(x2, g1, wqkv, wf, bf, gq_t, gk_t, bd, tri)


def _fox_kernel(q_ref, k_ref, v_ref, cumt_ref, o_ref, s_ref, *, tq, hd):
    hp = pl.program_id(1)
    seq = q_ref.shape[0]
    nh = LANES // hd
    lane = lax.broadcasted_iota(I32, (1, LANES), 1)
    on_or_below_diag = lax.broadcasted_iota(I32, (tq, tq), 1) <= lax.broadcasted_iota(I32, (tq, tq), 0)
    heads = [(hp * nh + hh, (lane >= hh * hd) & (lane < (hh + 1) * hd)) for hh in range(nh)]

    def scores(i, j, slot):
        q = q_ref[i * tq:(i + 1) * tq, :]
        kj = k_ref[j * tq:(j + 1) * tq, :]
        for hh, (h, in_head) in enumerate(heads):
            qh = jnp.where(in_head, q, jnp.zeros_like(q))
            s_ref[slot, hh] = _dot_nt(qh, kj) - cumt_ref[pl.ds(h, 1), j * tq:(j + 1) * tq]

    def consume(j, slot, carry, masked):
        vj = v_ref[j * tq:(j + 1) * tq, :]
        out = []
        for hh, ((_, in_head), (m, acc)) in enumerate(zip(heads, carry)):
            vh = jnp.where(in_head, vj, jnp.ones_like(vj))
            s = s_ref[slot, hh]
            if masked:
                s = jnp.where(on_or_below_diag, s, NEG_BIG)
            m_new = jnp.maximum(m, jnp.max(s, axis=-1, keepdims=True))
            p = jnp.exp2(s - m_new)
            out.append((m_new, jnp.exp2(m - m_new) * acc + _dot(p.astype(BF16), vh)))
        return tuple(out)

    tiles = [(i, j) for i in range(seq // tq) for j in range(i + 1)]
    scores(*tiles[0], 0)
    carry = None
    for n, (i, j) in enumerate(tiles):
        if n + 1 < len(tiles):
            scores(*tiles[n + 1], (n + 1) % 2)
        if j == 0:
            carry = tuple((jnp.full((tq, 1), NEG_BIG, F32), jnp.zeros((tq, LANES), F32)) for _ in heads)
        carry = consume(j, n % 2, carry, j == i)
        if j == i:
            o = None
            for hh in range(nh - 1, -1, -1):
                acc = carry[hh][1]
                other = ((hh + 1) % nh) * hd
                val = acc / acc[:, other:other + 1]
                o = val if o is None else jnp.where(lane < (hh + 1) * hd, val, o)
            o_ref[i * tq:(i + 1) * tq, :] = o.astype(BF16)


def _fox(q, k, v, cumt, batch, seq, width, hd, fh_pad):
    t = q.shape[0]
    tq = min(TQ_FOX, seq)
    assert seq % tq == 0
    seqmap = lambda b, hp: (b, hp)
    return pl.pallas_call(
        functools.partial(_fox_kernel, tq=tq, hd=hd),
        grid=(batch, width // LANES),
        in_specs=[pl.BlockSpec((seq, LANES), seqmap), pl.BlockSpec((seq, LANES), seqmap),
                  pl.BlockSpec((seq, LANES), seqmap), pl.BlockSpec((fh_pad, seq), lambda b, hp: (0, b))],
        out_specs=pl.BlockSpec((seq, LANES), seqmap),
        out_shape=jax.ShapeDtypeStruct((t, width), BF16),
        scratch_shapes=[pltpu.VMEM((2, LANES // hd, tq, tq), F32)],
        compiler_params=_params(2),
        name="fox",
    )(q, k, v, cumt)


def _merge_kernel(x_ref, yf_ref, mk_ref, mv_ref, g1_ref, wcat_ref, bg_ref, cw_ref, xqg_ref,
                  wbr_ref, wout_ref, g2_ref, wrh_ref, wrl_ref, br_ref, ltri_ref, utri_ref,
                  x1_ref, h2_ref, lpos_ref, lpost_ref, tw_ref, tcnt_ref, tbase_ref,
                  prev_ref, halo_ref, carry_ref, hi_ref, lo_ref, *, steps_per_batch, width, xa_heads, xa_hd):
    i = pl.program_id(0)
    nt = pl.num_programs(0) - 1
    tm = ltri_ref.shape[0]
    n_win = x_ref.shape[0] // tm

    @pl.when(i == 0)
    def _():
        prev_ref[...] = jnp.zeros_like(prev_ref)
        halo_ref[...] = jnp.zeros_like(halo_ref)
        carry_ref[...] = jnp.zeros_like(carry_ref)
        hi_ref[...] = jnp.zeros_like(hi_ref)
        lo_ref[...] = jnp.zeros_like(lo_ref)

    routed = [_route(hi_ref[k * tm:(k + 1) * tm, :], lo_ref[k * tm:(k + 1) * tm, :], wrh_ref, wrl_ref, br_ref)
              for k in range(n_win)]

    halo = jnp.where(i < nt, jnp.where(i % steps_per_batch == 0, 0.0, prev_ref[...]), halo_ref[...])
    halo_ref[...] = halo

    for k in range(n_win):
        rows = slice(k * tm, (k + 1) * tm)
        place = functools.partial(_place, i, *routed[k], ltri_ref, utri_ref, lpos_ref, lpost_ref, tw_ref,
                                  tcnt_ref, tbase_ref, carry_ref, rows, k)
        halo = _mix_window(x_ref, yf_ref, mk_ref, mv_ref, g1_ref, wcat_ref, bg_ref, cw_ref, xqg_ref, wbr_ref,
                           wout_ref, g2_ref, x1_ref, h2_ref, hi_ref, lo_ref, rows, halo, place,
                           width=width, xa_heads=xa_heads, xa_hd=xa_hd)
    prev_ref[...] = halo


def _mix_window(x_ref, yf_ref, mk_ref, mv_ref, g1_ref, wcat_ref, bg_ref, cw_ref, xqg_ref, wbr_ref, wout_ref,
                g2_ref, x1_ref, h2_ref, hi_ref, lo_ref, rows, halo, place, *, width, xa_heads, xa_hd):
    tm = rows.stop - rows.start
    d = x_ref.shape[1]
    x = x_ref[rows, :]
    hb = _rms(x, g1_ref[...], NORM_EPS).astype(BF16)
    pc = _dot(hb, wcat_ref[...])

    uc = pc[:, :width] * pc[:, width:2 * width]
    row = lax.broadcasted_iota(I32, (tm, 1), 0)
    p1 = halo[SUBLANES - 1:SUBLANES, :]
    p2 = halo[SUBLANES - 2:SUBLANES - 1, :]
    m1 = jnp.where(row == 0, p1, pltpu.roll(uc, 1, 0))
    m2 = jnp.where(row == 0, p2, jnp.where(row == 1, p1, pltpu.roll(uc, 2, 0)))
    y_conv = pc[:, 2 * width:3 * width] * (cw_ref[0:1, :] * m2 + cw_ref[1:2, :] * m1 + cw_ref[2:3, :] * uc)

    ys = []
    for h in range(xa_heads):
        sl = slice(3 * width + h * xa_hd, 3 * width + (h + 1) * xa_hd)
        ml = slice(h * xa_hd, (h + 1) * xa_hd)
        qh = _rms(pc[:, sl], xqg_ref[...], QK_EPS) * (xa_hd ** -0.5)
        s = _dot_nt(qh.astype(BF16), mk_ref[:, ml])
        p = jnp.exp(s - jnp.max(s, axis=-1, keepdims=True))
        ys.append(_dot(p.astype(BF16), mv_ref[:, ml]) / jnp.sum(p, axis=-1, keepdims=True))
    y_mem = jnp.concatenate(ys, axis=-1)

    place()

    merged = jnp.zeros((tm, d), F32)
    for n, yb in enumerate((yf_ref[rows, :], y_conv.astype(BF16), y_mem.astype(BF16))):
        gate = jax.nn.sigmoid(pc[:, 4 * width + n * d:4 * width + (n + 1) * d] + bg_ref[n:n + 1, :])
        merged = merged + gate * _dot(yb, wbr_ref[n])
    x1 = x + _dot(merged.astype(BF16), wout_ref[...])
    x1_ref[rows, :] = x1
    hi, lo = _split_bf16(_rms(x1, g2_ref[...], NORM_EPS))
    h2_ref[rows, :] = hi
    hi_ref[rows, :] = hi
    lo_ref[rows, :] = lo
    return uc[tm - SUBLANES:tm, :]


def _route(hi, lo, wrh_ref, wrl_ref, br_ref):
    tm = hi.shape[0]
    logits = _dot(hi, wrh_ref[...]) + _dot(hi, wrl_ref[...]) + _dot(lo, wrh_ref[...]) + br_ref[...]
    lane = lax.broadcasted_iota(I32, (tm, LANES), 1)
    vals, idxs = [], []
    onehot = jnp.zeros((tm, LANES), F32)
    for _ in range(TOP_K):
        m = jnp.max(logits, axis=-1, keepdims=True)
        idx = jnp.min(jnp.where(logits == m, lane, LANES), axis=-1, keepdims=True)
        sel = lane == idx
        logits = jnp.where(sel, -jnp.inf, logits)
        onehot = onehot + sel.astype(F32)
        vals.append(m)
        idxs.append(idx)
    es = [jnp.exp(v - vals[0]) for v in vals]
    den = es[0]
    for e in es[1:]:
        den = den + e
    return onehot, idxs, [e / den for e in es]


def _place(i, onehot, idxs, tws, ltri_ref, utri_ref, lpos_ref, lpost_ref, tw_ref, tcnt_ref, tbase_ref,
           carry_ref, rows, k):
    tm = onehot.shape[0]
    lane = lax.broadcasted_iota(I32, (tm, LANES), 1)
    cnt = jnp.sum(onehot, axis=0, keepdims=True)
    span = jnp.floor((cnt + (SUBLANES - 1)) * (1.0 / SUBLANES)) * SUBLANES
    s_hi, s_lo = _split_bf16(jnp.broadcast_to(span, (SUBLANES, LANES)))
    seg_start = (_dot(s_hi, utri_ref[...]) + _dot(s_lo, utri_ref[...]))[0:1, :]
    where_to = _dot(ltri_ref[...], onehot.astype(BF16)) + seg_start
    tcnt_ref[k] = jnp.broadcast_to(span, (SUBLANES, LANES))
    tbase_ref[k] = carry_ref[...]
    carry_ref[...] = jnp.where(i > 0, carry_ref[...] + span, 0.0)

    lpos_o = jnp.zeros((tm, LANES), F32)
    tw_o = jnp.zeros((tm, LANES), F32)
    for kk in range(TOP_K):
        lpos = jnp.sum(jnp.where(lane == idxs[kk], where_to, 0.0), axis=-1, keepdims=True)
        lpos_o = jnp.where(lane == kk, lpos, lpos_o)
        tw_o = jnp.where(lane == kk, tws[kk], tw_o)
    lpos_ref[rows, :] = lpos_o.astype(I32)
    lpost_ref[k] = lpos_o.T[:SUBLANES, :].astype(I32)
    tw_ref[rows, :] = tw_o


def _merge(x2, y_fox, mk, mv, g1, wcat, bg, cw, xqg, wbr, wout, g2, wrh, wrl, br,
           seq, n_mem, width, xa_heads, xa_hd):
    t, d = x2.shape
    tm = min(TM_WINDOW, seq)
    nwin = min(MERGE_WINDOWS, seq // tm)
    tb = nwin * tm
    nt = t // tb
    tpb = seq // tb
    ltri = (jnp.arange(tm)[:, None] > jnp.arange(tm)[None, :]).astype(BF16)
    utri = (jnp.arange(LANES)[:, None] < jnp.arange(LANES)[None, :]).astype(BF16)
    mix = lambda i: jnp.minimum(i, nt - 1)
    rte = lambda i: jnp.maximum(i - 1, 0)
    row = lambda i: (mix(i), 0)
    rrow = lambda i: (rte(i), 0)
    tile3 = lambda i: (rte(i), 0, 0)
    mem_map = lambda i: (mix(i) // tpb, 0)
    return pl.pallas_call(
        functools.partial(_merge_kernel, steps_per_batch=tpb, width=width, xa_heads=xa_heads, xa_hd=xa_hd),
        grid=(nt + 1,),
        in_specs=[pl.BlockSpec((tb, d), row), pl.BlockSpec((tb, width), row),
                  pl.BlockSpec((n_mem, width), mem_map), pl.BlockSpec((n_mem, width), mem_map),
                  _const_spec((1, d)), _const_spec(wcat.shape), _const_spec(bg.shape), _const_spec(cw.shape),
                  _const_spec((1, xa_hd)), _const_spec(wbr.shape), _const_spec((d, d)), _const_spec((1, d)),
                  _const_spec((d, LANES)), _const_spec((d, LANES)), _const_spec((1, LANES)),
                  _const_spec((tm, tm)), _const_spec((LANES, LANES))],
        out_specs=[pl.BlockSpec((tb, d), row), pl.BlockSpec((tb, d), row),
                   pl.BlockSpec((tb, LANES), rrow), pl.BlockSpec((nwin, SUBLANES, tm), tile3),
                   pl.BlockSpec((tb, LANES), rrow),
                   pl.BlockSpec((nwin, SUBLANES, LANES), tile3), pl.BlockSpec((nwin, SUBLANES, LANES), tile3)],
        out_shape=[jax.ShapeDtypeStruct((t, d), F32), jax.ShapeDtypeStruct((t, d), BF16),
                   jax.ShapeDtypeStruct((t, LANES), I32), jax.ShapeDtypeStruct((t // tm, SUBLANES, tm), I32),
                   jax.ShapeDtypeStruct((t, LANES), F32),
                   jax.ShapeDtypeStruct((t // tm, SUBLANES, LANES), F32),
                   jax.ShapeDtypeStruct((t // tm, SUBLANES, LANES), F32)],
        scratch_shapes=[pltpu.VMEM((SUBLANES, width), F32), pltpu.VMEM((SUBLANES, width), F32),
                        pltpu.VMEM((SUBLANES, LANES), F32), pltpu.VMEM((tb, d), BF16), pltpu.VMEM((tb, d), BF16)],
        compiler_params=_params(1),
        name="merge",
    )(x2, y_fox, mk, mv, g1, wcat, bg, cw, xqg, wbr, wout, g2, wrh, wrl, br, ltri, utri)


def _block_rows(tm, n_exp):
    return -(-(TOP_K * tm + (SUBLANES - 1) * n_exp) // MXU_DIM) * MXU_DIM


def _chunked_copies(src_ref, src0, dst_ref, dst0, n, sem, max_rows, wait, fixed_src=False):
    for b in range(max_rows.bit_length() - 1, SUBLANES.bit_length() - 2, -1):
        size = 1 << b
        off = (n >> (b + 1)) << (b + 1)

        @pl.when((n >> b) & 1 == 1)
        def _():
            src = 0 if fixed_src else pl.multiple_of(src0 + off, SUBLANES)
            cp = pltpu.make_async_copy(src_ref.at[pl.ds(src, size)],
                                       dst_ref.at[pl.ds(pl.multiple_of(dst0 + off, SUBLANES), size)], sem)
            if wait:
                cp.wait()
            else:
                cp.start()


def _chunk_classes(tm):
    return [1 << b for b in range(SUBLANES.bit_length() - 1, tm.bit_length())]


def _chunk_table(seg_len, seg_src, seg_dst, tm, n_exp):
    sizes = jnp.asarray(_chunk_classes(tm), I32)
    bit = (seg_len[:, :, None] // sizes) % 2
    off = seg_len[:, :, None] // (2 * sizes) * (2 * sizes)
    slot = jnp.cumsum(bit, axis=1) - 1
    hit = (slot[:, :, None, :] == jnp.arange(n_exp, dtype=I32)[None, None, :, None]) & (bit[:, :, None, :] == 1)
    pack = lambda start: jnp.sum(jnp.where(hit, (start[:, :, None] + off)[:, :, None, :], 0), axis=1)
    nw = seg_len.shape[0]
    cols = [jnp.swapaxes(pack(seg_src), 1, 2).reshape(nw, -1), jnp.swapaxes(pack(seg_dst), 1, 2).reshape(nw, -1),
            jnp.sum(bit, axis=1)]
    tbl = jnp.concatenate(cols, axis=1).astype(I32)
    assert tbl.shape[1] <= CHUNK_TABLE_LEN
    return jnp.pad(tbl, ((0, 0), (0, CHUNK_TABLE_LEN - tbl.shape[1]))).reshape(-1)


def _table_copies(tbl_ref, k, block_ref, sorted_ref, sem, tm, n_exp, to_sorted, wait):
    sizes = _chunk_classes(tm)
    base = k * CHUNK_TABLE_LEN
    for c, size in enumerate(sizes):
        def body(i, carry, c=c, size=size):
            blk = block_ref.at[pl.ds(pl.multiple_of(tbl_ref[base + c * n_exp + i], SUBLANES), size)]
            srt = sorted_ref.at[
                pl.ds(pl.multiple_of(tbl_ref[base + (len(sizes) + c) * n_exp + i], SUBLANES), size)]
            cp = pltpu.make_async_copy(blk, srt, sem) if to_sorted else pltpu.make_async_copy(srt, blk, sem)
            if wait:
                cp.wait()
            else:
                cp.start()
            return carry
        lax.fori_loop(0, tbl_ref[base + 2 * len(sizes) * n_exp + c], body, 0)


def _dispatch_kernel(pad_start_ref, pad_cnt_ref, n_used_ref, tbl_ref, tbl_prev_ref, lpost_ref, h2_ref, xs_ref,
                     buf_ref, zero_ref, sems, zsem, *, n_exp, n_tiles):
    s = pl.program_id(0)
    ns = pl.num_programs(0)
    n_win, _, tm = lpost_ref.shape
    rows = buf_ref.shape[1]
    mine = (s % 2) * n_win
    other = n_win - mine

    def clear(wait):
        def per_expert(e, c):
            _chunked_copies(zero_ref, 0, xs_ref, pad_start_ref[e], pad_cnt_ref[e], zsem, ZERO_ROWS, wait,
                            fixed_src=True)
            return c

        def per_tile(i, c):
            for part in range(TM_EXPERT // ZERO_ROWS):
                cp = pltpu.make_async_copy(
                    zero_ref, xs_ref.at[pl.ds(i * TM_EXPERT + part * ZERO_ROWS, ZERO_ROWS)], zsem)
                if wait:
                    cp.wait()
                else:
                    cp.start()
            return c

        lax.fori_loop(0, n_exp, per_expert, 0)
        lax.fori_loop(n_used_ref[0], n_tiles, per_tile, 0)

    @pl.when(s == 0)
    def _():
        zero_ref[...] = jnp.zeros_like(zero_ref)
        clear(False)

    r_iota = lax.broadcasted_iota(I32, (rows, tm), 0)
    for k in range(n_win):
        sel = r_iota == lpost_ref[k, 0:1, :]
        for kk in range(1, TOP_K):
            sel = sel | (r_iota == lpost_ref[k, kk:kk + 1, :])
        perm = jnp.where(sel, 1.0, 0.0).astype(BF16)
        buf_ref[mine + k] = _dot(perm, h2_ref[k * tm:(k + 1) * tm, :])

    def window_copies(table_ref, first_slot, wait):
        for k in range(n_win):
            _table_copies(table_ref, k, buf_ref.at[first_slot + k], xs_ref, sems.at[first_slot + k], tm, n_exp,
                          True, wait)

    window_copies(tbl_ref, mine, False)

    @pl.when(s >= 1)
    def _():
        window_copies(tbl_prev_ref, other, True)

    @pl.when(s == ns - 1)
    def _():
        window_copies(tbl_ref, mine, True)
        clear(True)


def _dispatch(pad_start, pad_cnt, n_used, chunk_tbl, lpost, h2, n_tiles, n_exp):
    t, d = h2.shape
    nw, _, tm = lpost.shape
    n_win = min(ROUTE_WINDOWS, nw)
    smem_blk = lambda shift: pl.BlockSpec((n_win * CHUNK_TABLE_LEN,), lambda i, *_: (jnp.maximum(i + shift, 0),),
                                          memory_space=pltpu.SMEM)
    grid_spec = pltpu.PrefetchScalarGridSpec(
        num_scalar_prefetch=3,
        grid=(nw // n_win,),
        in_specs=[smem_blk(0), smem_blk(-1),
                  pl.BlockSpec((n_win, SUBLANES, tm), lambda i, *_: (i, 0, 0)),
                  pl.BlockSpec((n_win * tm, d), lambda i, *_: (i, 0))],
        out_specs=pl.BlockSpec(memory_space=pl.ANY),
        scratch_shapes=[pltpu.VMEM((2 * n_win, _block_rows(tm, n_exp), d), F32), pltpu.VMEM((ZERO_ROWS, d), F32),
                        pltpu.SemaphoreType.DMA((2 * n_win,)), pltpu.SemaphoreType.DMA(())],
    )
    return pl.pallas_call(
        functools.partial(_dispatch_kernel, n_exp=n_exp, n_tiles=n_tiles),
        grid_spec=grid_spec,
        out_shape=jax.ShapeDtypeStruct((n_tiles * TM_EXPERT, d), F32),
        compiler_params=_params(1),
        name="dispatch",
    )(pad_start, pad_cnt, n_used, chunk_tbl, chunk_tbl, lpost, h2)


def _expert_kernel(tile_end_ref, wup_ref, bup_ref, wdn_ref, bdn_ref, xs_ref, ys_ref,
                   wup_bf_ref, wdn_bf_ref, xbuf_ref, ybuf_ref, xsems, ysems, *, d_ff, n_tiles):
    e = pl.program_id(0)
    n_exp = pl.num_programs(0)
    tm = xbuf_ref.shape[1]
    n_used = tile_end_ref[n_exp - 1]
    first = jnp.where(e == 0, 0, tile_end_ref[jnp.maximum(e - 1, 0)])
    last = tile_end_ref[e]

    def rows_of(g):
        return pl.ds(pl.multiple_of(g * tm, tm), tm)

    def x_copy(g):
        return pltpu.make_async_copy(xs_ref.at[rows_of(g)], xbuf_ref.at[g % 2], xsems.at[g % 2])

    def y_copy(g):
        return pltpu.make_async_copy(ybuf_ref.at[g % 2], ys_ref.at[rows_of(g)], ysems.at[g % 2])

    @pl.when((e == 0) & (n_used > 0))
    def _():
        x_copy(0).start()

    wup_bf_ref[...] = wup_ref[0].astype(BF16)
    wdn_bf_ref[...] = wdn_ref[0].astype(BF16)

    def tile(g, carry):
        slot = g % 2
        x_copy(g).wait()

        @pl.when(g + 1 < n_used)
        def _():
            x_copy(g + 1).start()

        @pl.when(g >= 2)
        def _():
            y_copy(g - 2).wait()

        h = _dot(xbuf_ref[slot].astype(BF16), wup_bf_ref[...]) + bup_ref[0]
        glu = jnp.minimum(h[:, :d_ff], SWIGLU_LIMIT)
        lin = jnp.clip(h[:, d_ff:], -SWIGLU_LIMIT, SWIGLU_LIMIT)
        a = glu * jax.nn.sigmoid(SWIGLU_ALPHA * glu) * (lin + 1.0)
        ybuf_ref[slot] = _dot(a.astype(BF16), wdn_bf_ref[...]) + bdn_ref[0]
        y_copy(g).start()
        return carry

    lax.fori_loop(first, last, tile, 0)

    @pl.when(e == n_exp - 1)
    def _():
        @pl.when(n_used >= 2)
        def _():
            y_copy(n_used - 2).wait()

        @pl.when(n_used >= 1)
        def _():
            y_copy(n_used - 1).wait()

        ybuf_ref[0] = jnp.zeros(ybuf_ref.shape[1:], F32)

        def clear(g, wait):
            cp = pltpu.make_async_copy(ybuf_ref.at[0], ys_ref.at[rows_of(g)], ysems.at[0])
            if wait:
                cp.wait()
            else:
                cp.start()

        lax.fori_loop(n_used, n_tiles, lambda g, c: (clear(g, False), c)[1], 0)
        lax.fori_loop(n_used, n_tiles, lambda g, c: (clear(g, True), c)[1], 0)


def _experts(tile_end, xs, wup, bup, wdn, bdn):
    n_rows, d = xs.shape
    n_exp, d_ff, _ = wdn.shape
    tm = TM_EXPERT
    exp3 = lambda e, *_: (e, 0, 0)
    grid_spec = pltpu.PrefetchScalarGridSpec(
        num_scalar_prefetch=1,
        grid=(n_exp,),
        in_specs=[pl.BlockSpec((1, d, 2 * d_ff), exp3), pl.BlockSpec((1, 1, 2 * d_ff), exp3),
                  pl.BlockSpec((1, d_ff, d), exp3), pl.BlockSpec((1, 1, d), exp3),
                  pl.BlockSpec(memory_space=pl.ANY)],
        out_specs=pl.BlockSpec(memory_space=pl.ANY),
        scratch_shapes=[pltpu.VMEM((d, 2 * d_ff), BF16), pltpu.VMEM((d_ff, d), BF16),
                        pltpu.VMEM((2, tm, d), F32), pltpu.VMEM((2, tm, d), F32),
                        pltpu.SemaphoreType.DMA((2,)), pltpu.SemaphoreType.DMA((2,))],
    )
    return pl.pallas_call(
        functools.partial(_expert_kernel, d_ff=d_ff, n_tiles=n_rows // tm),
        grid_spec=grid_spec,
        out_shape=jax.ShapeDtypeStruct((n_rows, d), F32),
        compiler_params=_params(1),
        name="experts",
    )(tile_end, wup, bup, wdn, bdn, xs)


def _combine_kernel(tbl_ref, tbl_next_ref, lpos_ref, tw_ref, x1_ref, ys_ref, o_ref, buf_ref, sems,
                    *, n_exp, n_win):
    s = pl.program_id(0)
    ns = pl.num_programs(0)
    tm = x1_ref.shape[0] // n_win
    rows = buf_ref.shape[1]
    mine = (s % 2) * n_win
    other = n_win - mine

    def window_copies(table_ref, first_slot, wait):
        for k in range(n_win):
            _table_copies(table_ref, k, buf_ref.at[first_slot + k], ys_ref, sems.at[first_slot + k], tm, n_exp,
                          False, wait)

    @pl.when(s == 0)
    def _():
        buf_ref[...] = jnp.zeros_like(buf_ref)
        window_copies(tbl_ref, 0, False)

    @pl.when(s + 1 < ns)
    def _():
        window_copies(tbl_next_ref, other, False)

    window_copies(tbl_ref, mine, True)

    c_iota = lax.broadcasted_iota(I32, (tm, rows), 1)
    for k in range(n_win):
        sl = slice(k * tm, (k + 1) * tm)
        lpos = lpos_ref[sl, :]
        tw = tw_ref[sl, :]
        wperm = jnp.zeros((tm, rows), F32)
        for kk in range(TOP_K):
            wperm = jnp.where(c_iota == lpos[:, kk:kk + 1], tw[:, kk:kk + 1], wperm)
        o_ref[sl, :] = x1_ref[sl, :] + _dot(wperm.astype(BF16), buf_ref[mine + k].astype(BF16))


def _combine(chunk_tbl, lpos, tw, x1, ys, n_exp):
    t, d = x1.shape
    tm = min(TM_WINDOW, t)
    n_win = min(ROUTE_WINDOWS, t // tm)
    ns = t // (n_win * tm)
    row = lambda i: (i, 0)
    smem_blk = lambda shift: pl.BlockSpec((n_win * CHUNK_TABLE_LEN,), lambda i: (jnp.minimum(i + shift, ns - 1),),
                                          memory_space=pltpu.SMEM)
    return pl.pallas_call(
        functools.partial(_combine_kernel, n_exp=n_exp, n_win=n_win),
        grid=(ns,),
        in_specs=[smem_blk(0), smem_blk(1),
                  pl.BlockSpec((n_win * tm, LANES), row), pl.BlockSpec((n_win * tm, LANES), row),
                  pl.BlockSpec((n_win * tm, d), row), pl.BlockSpec(memory_space=pl.ANY)],
        out_specs=pl.BlockSpec((n_win * tm, d), row),
        out_shape=jax.ShapeDtypeStruct((t, d), F32),
        scratch_shapes=[pltpu.VMEM((2 * n_win, _block_rows(tm, n_exp), d), F32),
                        pltpu.SemaphoreType.DMA((2 * n_win,))],
        compiler_params=_params(1),
        name="combine",
    )(chunk_tbl, chunk_tbl, lpos, tw, x1, ys)


def _pad_lanes(a, value=0.0):
    return jnp.pad(a, ((0, 0), (0, LANES - a.shape[-1])), constant_values=value)


def _layer(x, mem, norm1_g, w_in, b_forget, b_gate, fox_q_g, fox_k_g, conv_w, mem_norm_g, w_mem_kv,
           xa_q_g, xa_k_g, w_branch, w_out, norm2_g, w_router, b_router, w_up, b_up, w_down, b_down):
    batch, seq, d = x.shape
    n_mem = mem.shape[1]
    n_branch, width, _ = w_branch.shape
    n_fh = b_forget.shape[0]
    hd = fox_q_g.shape[0]
    xa_hd = xa_q_g.shape[0]
    xa_heads = width // xa_hd
    n_exp = w_router.shape[1]
    assert n_fh * hd == width and n_branch == 3 and conv_w.shape[0] == 3
    assert LANES % hd == 0 and width % MXU_DIM == 0 and n_exp <= LANES and n_fh <= LANES
    fh_pad = -(-n_fh // SUBLANES) * SUBLANES
    t = batch * seq
    x2 = x.reshape(t, d)
    row1 = lambda a: a.reshape(1, -1).astype(F32)

    o_f = 3 * width
    o_conv = o_f + n_fh
    wqkv = w_in[:, :o_f].astype(BF16)
    wf = _pad_lanes(w_in[:, o_f:o_conv]).astype(BF16)
    wcat = w_in[:, o_conv:].astype(BF16)

    mk, mv = _memkv(mem.reshape(batch * n_mem, d), row1(mem_norm_g), w_mem_kv.astype(BF16), row1(xa_k_g),
                    xa_heads, xa_hd)
    q, k, v, cumt = _qkv(x2, row1(norm1_g), wqkv, wf, _pad_lanes(row1(b_forget)),
                         row1(jnp.tile(fox_q_g, n_fh)), row1(jnp.tile(fox_k_g, n_fh)),
                         seq, width, hd, fh_pad)
    y_fox = _fox(q, k, v, cumt, batch, seq, width, hd, fh_pad)

    wr = _pad_lanes(w_router.astype(F32))
    wr_hi = wr.astype(BF16)
    wr_lo = (wr - wr_hi.astype(F32)).astype(BF16)
    x1, h2, lpos, lpost, tw, tcnt, tbase = _merge(
        x2, y_fox, mk, mv, row1(norm1_g), wcat, b_gate.astype(F32), conv_w.astype(F32), row1(xa_q_g),
        w_branch.astype(BF16), w_out.astype(BF16), row1(norm2_g), wr_hi, wr_lo,
        _pad_lanes(row1(b_router), NEG_BIG), seq, n_mem, width, xa_heads, xa_hd)

    seg_len = tcnt[:, 0, :n_exp].astype(I32)
    seg_base = tbase[:, 0, :n_exp].astype(I32)
    counts = seg_base[-1] + seg_len[-1]
    tiles_per = (counts + TM_EXPERT - 1) // TM_EXPERT
    tile_end = jnp.cumsum(tiles_per)
    group_start = (tile_end - tiles_per) * TM_EXPERT
    seg_dst = group_start[None, :] + seg_base
    seg_src = jnp.cumsum(seg_len, axis=1) - seg_len
    n_tiles = -(-(t * TOP_K + (SUBLANES - 1) * seg_len.size) // TM_EXPERT) + n_exp
    n_used = tile_end[-1:].astype(I32)
    pad_start = (group_start + counts).astype(I32)
    pad_cnt = (tiles_per * TM_EXPERT - counts).astype(I32)
    chunk_tbl = _chunk_table(seg_len, seg_src.astype(I32), seg_dst.astype(I32), lpost.shape[2], n_exp)

    xs = _dispatch(pad_start, pad_cnt, n_used, chunk_tbl, lpost, h2, n_tiles, n_exp)
    ys = _experts(tile_end.astype(I32), xs, w_up.astype(F32), b_up[:, None, :].astype(F32),
                  w_down.astype(F32), b_down[:, None, :].astype(F32))
    out = _combine(chunk_tbl, lpos, tw, x1, ys, n_exp)
    return out.reshape(batch, seq, d)


def kernel(x, mem, norm1_g, w_in, b_forget, b_gate, fox_q_g, fox_k_g, conv_w, mem_norm_g, w_mem_kv, xa_q_g,
           xa_k_g, w_branch, w_out, norm2_g, w_router, b_router, w_up, b_up, w_down, b_down):
    for l in range(norm1_g.shape[0]):
        x = _layer(x, mem, norm1_g[l], w_in[l], b_forget[l], b_gate[l], fox_q_g[l], fox_k_g[l], conv_w[l],
                   mem_norm_g[l], w_mem_kv[l], xa_q_g[l], xa_k_g[l], w_branch[l], w_out[l], norm2_g[l],
                   w_router[l], b_router[l], w_up[l], b_up[l], w_down[l], b_down[l])
    return x
```

```python
import functools

import jax
import jax.numpy as jnp
from jax import lax
from jax.experimental import pallas as pl
from jax.experimental.pallas import tpu as pltpu

F32 = jnp.float32
BF16 = jnp.bfloat16
I32 = jnp.int32

NORM_EPS = 1e-5
QK_EPS = 1e-6
TOP_K = 4
SWIGLU_ALPHA = 1.702
SWIGLU_LIMIT = 7.0
NEG_BIG = -1e30
LOG2_E = 1.4426950408889634

LANES = 128
SUBLANES = 8
MXU_DIM = 256
VMEM_LIMIT_BYTES = 56 * 1024 * 1024

TM_QKV = 512
QKV_CHUNKS = 4
TQ_FOX = 512
TM_WINDOW = 256
MERGE_WINDOWS = 2
ROUTE_WINDOWS = 2
TM_EXPERT = 512
ZERO_ROWS = 256
CHUNK_TABLE_LEN = 512


def _params(n_axes):
    return pltpu.CompilerParams(dimension_semantics=("arbitrary",) * n_axes,
                                vmem_limit_bytes=VMEM_LIMIT_BYTES)


def _const_spec(shape):
    zeros = (0,) * len(shape)
    return pl.BlockSpec(shape, lambda *_: zeros, pipeline_mode=pl.Buffered(1))


def _rms(x, g, eps):
    return x * lax.rsqrt(jnp.mean(x * x, axis=-1, keepdims=True) + eps) * g


def _split_bf16(x):
    hi = x.astype(BF16)
    lo = (x - hi.astype(F32)).astype(BF16)
    return hi, lo


def _dot(a, b):
    return jnp.dot(a, b, preferred_element_type=F32)


def _dot_nt(a, b):
    return lax.dot_general(a, b, (((1,), (1,)), ((), ())), preferred_element_type=F32)


def _memkv_kernel(mem_ref, g_ref, w_ref, kg_ref, mk_ref, mv_ref, *, n_heads, hd):
    y = _rms(mem_ref[...], g_ref[...], NORM_EPS)
    kv = _dot(y.astype(BF16), w_ref[...])
    width = n_heads * hd
    for h in range(n_heads):
        sl = slice(h * hd, (h + 1) * hd)
        mk_ref[:, sl] = _rms(kv[:, sl], kg_ref[...], QK_EPS).astype(BF16)
    mv_ref[...] = kv[:, width:].astype(BF16)


def _memkv(mem2, g, w_kv, kg, n_heads, hd):
    rows, d = mem2.shape
    width = n_heads * hd
    tm = min(rows, 512)
    return pl.pallas_call(
        functools.partial(_memkv_kernel, n_heads=n_heads, hd=hd),
        grid=(rows // tm,),
        in_specs=[pl.BlockSpec((tm, d), lambda i: (i, 0)),
                  _const_spec((1, d)), _const_spec((d, 2 * width)), _const_spec((1, hd))],
        out_specs=[pl.BlockSpec((tm, width), lambda i: (i, 0)),
                   pl.BlockSpec((tm, width), lambda i: (i, 0))],
        out_shape=[jax.ShapeDtypeStruct((rows, width), BF16)] * 2,
        compiler_params=_params(1),
        name="memkv",
    )(mem2, g, w_kv, kg)


def _qkv_kernel(x_ref, g1_ref, wqkv_ref, wf_ref, bf_ref, gq_ref, gk_ref, bd_ref, tri_ref,
                q_ref, k_ref, v_ref, cumt_ref, carry_ref,
                *, tiles_per_batch, width, hd, fh_pad):
    i = pl.program_id(0)
    tm = tri_ref.shape[0]

    @pl.when(i % tiles_per_batch == 0)
    def _():
        carry_ref[...] = jnp.zeros_like(carry_ref)

    def head_norm(z, g):
        sq = (z * z).astype(BF16)
        parts = []
        for c in range(width // MXU_DIM):
            sl = slice(c * MXU_DIM, (c + 1) * MXU_DIM)
            parts.append(_dot(sq[:, sl], bd_ref[...]))
        ms = jnp.concatenate(parts, axis=-1) * (1.0 / hd)
        return z * lax.rsqrt(ms + QK_EPS) * g

    prefix = carry_ref[0:1, :]
    for c in range(x_ref.shape[0] // tm):
        rows = slice(c * tm, (c + 1) * tm)
        hb = _rms(x_ref[rows, :], g1_ref[...], NORM_EPS).astype(BF16)
        qkv = _dot(hb, wqkv_ref[...])
        q = head_norm(qkv[:, :width], gq_ref[...]) * (hd ** -0.5 * LOG2_E)
        k = head_norm(qkv[:, width:2 * width], gk_ref[...])
        q_ref[rows, :] = q.astype(BF16)
        k_ref[rows, :] = k.astype(BF16)
        v_ref[rows, :] = qkv[:, 2 * width:].astype(BF16)

        z = _dot(hb, wf_ref[...]) + bf_ref[...]
        log_f = jnp.minimum(z, 0.0) - jnp.log(1.0 + jnp.exp(-jnp.abs(z)))
        hi, lo = _split_bf16(log_f)
        cum = _dot(tri_ref[...], hi) + _dot(tri_ref[...], lo) + prefix
        prefix = cum[tm - 1:tm, :]
        cumt_ref[:, rows] = (cum * LOG2_E).T[:fh_pad, :]
    carry_ref[...] = jnp.broadcast_to(prefix, carry_ref.shape)


def _qkv(x2, g1, wqkv, wf, bf, gq_t, gk_t, seq, width, hd, fh_pad):
    t, d = x2.shape
    tm = min(TM_QKV, seq)
    tb = min(QKV_CHUNKS * tm, seq)
    bd = (jnp.arange(MXU_DIM)[:, None] // hd == jnp.arange(MXU_DIM)[None, :] // hd).astype(BF16)
    tri = (jnp.arange(tm)[:, None] >= jnp.arange(tm)[None, :]).astype(BF16)
    row = lambda i: (i, 0)
    return pl.pallas_call(
        functools.partial(_qkv_kernel, tiles_per_batch=seq // tb, width=width, hd=hd, fh_pad=fh_pad),
        grid=(t // tb,),
        in_specs=[pl.BlockSpec((tb, d), row), _const_spec((1, d)), _const_spec((d, 3 * width)),
                  _const_spec((d, LANES)), _const_spec((1, LANES)), _const_spec((1, width)),
                  _const_spec((1, width)), _const_spec((MXU_DIM, MXU_DIM)), _const_spec((tm, tm))],
        out_specs=[pl.BlockSpec((tb, width), row)] * 3 + [pl.BlockSpec((fh_pad, tb), lambda i: (0, i))],
        out_shape=[jax.ShapeDtypeStruct((t, width), BF16)] * 3 + [jax.ShapeDtypeStruct((fh_pad, t), F32)],
        scratch_shapes=[pltpu.VMEM((SUBLANES, LANES), F32)],
        compiler_params=_params(1),
        name="qkv",
    )(x2, g1, wqkv, wf, bf, gq_t, gk_t, bd, tri)


def _fox_kernel(q_ref, k_ref, v_ref, cumt_ref, o_ref, s_ref, *, tq, hd):
    hp = pl.program_id(1)
    seq = q_ref.shape[0]
    nh = LANES // hd
    lane = lax.broadcasted_iota(I32, (1, LANES), 1)
    on_or_below_diag = lax.broadcasted_iota(I32, (tq, tq), 1) <= lax.broadcasted_iota(I32, (tq, tq), 0)
    heads = [(hp * nh + hh, (lane >= hh * hd) & (lane < (hh + 1) * hd)) for hh in range(nh)]

    def scores(i, j, slot):
        q = q_ref[i * tq:(i + 1) * tq, :]
        kj = k_ref[j * tq:(j + 1) * tq, :]
        for hh, (h, in_head) in enumerate(heads):
            qh = jnp.where(in_head, q, jnp.zeros_like(q))
            s_ref[slot, hh] = _dot_nt(qh, kj) - cumt_ref[pl.ds(h, 1), j * tq:(j + 1) * tq]

    def consume(j, slot, carry, masked):
        vj = v_ref[j * tq:(j + 1) * tq, :]
        out = []
        for hh, ((_, in_head), (m, acc)) in enumerate(zip(heads, carry)):
            vh = jnp.where(in_head, vj, jnp.ones_like(vj))
            s = s_ref[slot, hh]
            if masked:
                s = jnp.where(on_or_below_diag, s, NEG_BIG)
            m_new = jnp.maximum(m, jnp.max(s, axis=-1, keepdims=True))
            p = jnp.exp2(s - m_new)
            out.append((m_new, jnp.exp2(m - m_new) * acc + _dot(p.astype(BF16), vh)))
        return tuple(out)

    tiles = [(i, j) for i in range(seq // tq) for j in range(i + 1)]
    scores(*tiles[0], 0)
    carry = None
    for n, (i, j) in enumerate(tiles):
        if n + 1 < len(tiles):
            scores(*tiles[n + 1], (n + 1) % 2)
        if j == 0:
            carry = tuple((jnp.full((tq, 1), NEG_BIG, F32), jnp.zeros((tq, LANES), F32)) for _ in heads)
        carry = consume(j, n % 2, carry, j == i)
        if j == i:
            o = None
            for hh in range(nh - 1, -1, -1):
                acc = carry[hh][1]
                other = ((hh + 1) % nh) * hd
                val = acc / acc[:, other:other + 1]
                o = val if o is None else jnp.where(lane < (hh + 1) * hd, val, o)
            o_ref[i * tq:(i + 1) * tq, :] = o.astype(BF16)


def _fox(q, k, v, cumt, batch, seq, width, hd, fh_pad):
    t = q.shape[0]
    tq = min(TQ_FOX, seq)
    assert seq % tq == 0
    seqmap = lambda b, hp: (b, hp)
    return pl.pallas_call(
        functools.partial(_fox_kernel, tq=tq, hd=hd),
        grid=(batch, width // LANES),
        in_specs=[pl.BlockSpec((seq, LANES), seqmap), pl.BlockSpec((seq, LANES), seqmap),
                  pl.BlockSpec((seq, LANES), seqmap), pl.BlockSpec((fh_pad, seq), lambda b, hp: (0, b))],
        out_specs=pl.BlockSpec((seq, LANES), seqmap),
        out_shape=jax.ShapeDtypeStruct((t, width), BF16),
        scratch_shapes=[pltpu.VMEM((2, LANES // hd, tq, tq), F32)],
        compiler_params=_params(2),
        name="fox",
    )(q, k, v, cumt)


def _merge_kernel(x_ref, yf_ref, mk_ref, mv_ref, g1_ref, wcat_ref, bg_ref, cw_ref, xqg_ref,
                  wbr_ref, wout_ref, g2_ref, wrh_ref, wrl_ref, br_ref, ltri_ref, utri_ref,
                  x1_ref, h2_ref, lpos_ref, lpost_ref, tw_ref, tcnt_ref, tbase_ref,
                  prev_ref, halo_ref, carry_ref, hi_ref, lo_ref, *, steps_per_batch, width, xa_heads, xa_hd):
    i = pl.program_id(0)
    nt = pl.num_programs(0) - 1
    tm = ltri_ref.shape[0]
    n_win = x_ref.shape[0] // tm

    @pl.when(i == 0)
    def _():
        prev_ref[...] = jnp.zeros_like(prev_ref)
        halo_ref[...] = jnp.zeros_like(halo_ref)
        carry_ref[...] = jnp.zeros_like(carry_ref)
        hi_ref[...] = jnp.zeros_like(hi_ref)
        lo_ref[...] = jnp.zeros_like(lo_ref)

    routed = [_route(hi_ref[k * tm:(k + 1) * tm, :], lo_ref[k * tm:(k + 1) * tm, :], wrh_ref, wrl_ref, br_ref)
              for k in range(n_win)]

    halo = jnp.where(i < nt, jnp.where(i % steps_per_batch == 0, 0.0, prev_ref[...]), halo_ref[...])
    halo_ref[...] = halo

    for k in range(n_win):
        rows = slice(k * tm, (k + 1) * tm)
        place = functools.partial(_place, i, *routed[k], ltri_ref, utri_ref, lpos_ref, lpost_ref, tw_ref,
                                  tcnt_ref, tbase_ref, carry_ref, rows, k)
        halo = _mix_window(x_ref, yf_ref, mk_ref, mv_ref, g1_ref, wcat_ref, bg_ref, cw_ref, xqg_ref, wbr_ref,
                           wout_ref, g2_ref, x1_ref, h2_ref, hi_ref, lo_ref, rows, halo, place,
                           width=width, xa_heads=xa_heads, xa_hd=xa_hd)
    prev_ref[...] = halo


def _mix_window(x_ref, yf_ref, mk_ref, mv_ref, g1_ref, wcat_ref, bg_ref, cw_ref, xqg_ref, wbr_ref, wout_ref,
                g2_ref, x1_ref, h2_ref, hi_ref, lo_ref, rows, halo, place, *, width, xa_heads, xa_hd):
    tm = rows.stop - rows.start
    d = x_ref.shape[1]
    x = x_ref[rows, :]
    hb = _rms(x, g1_ref[...], NORM_EPS).astype(BF16)
    pc = _dot(hb, wcat_ref[...])

    uc = pc[:, :width] * pc[:, width:2 * width]
    row = lax.broadcasted_iota(I32, (tm, 1), 0)
    p1 = halo[SUBLANES - 1:SUBLANES, :]
    p2 = halo[SUBLANES - 2:SUBLANES - 1, :]
    m1 = jnp.where(row == 0, p1, pltpu.roll(uc, 1, 0))
    m2 = jnp.where(row == 0, p2, jnp.where(row == 1, p1, pltpu.roll(uc, 2, 0)))
    y_conv = pc[:, 2 * width:3 * width] * (cw_ref[0:1, :] * m2 + cw_ref[1:2, :] * m1 + cw_ref[2:3, :] * uc)

    ys = []
    for h in range(xa_heads):
        sl = slice(3 * width + h * xa_hd, 3 * width + (h + 1) * xa_hd)
        ml = slice(h * xa_hd, (h + 1) * xa_hd)
        qh = _rms(pc[:, sl], xqg_ref[...], QK_EPS) * (xa_hd ** -0.5)
        s = _dot_nt(qh.astype(BF16), mk_ref[:, ml])
        p = jnp.exp(s - jnp.max(s, axis=-1, keepdims=True))
        ys.append(_dot(p.astype(BF16), mv_ref[:, ml]) / jnp.sum(p, axis=-1, keepdims=True))
    y_mem = jnp.concatenate(ys, axis=-1)

    place()

    merged = jnp.zeros((tm, d), F32)
    for n, yb in enumerate((yf_ref[rows, :], y_conv.astype(BF16), y_mem.astype(BF16))):
        gate = jax.nn.sigmoid(pc[:, 4 * width + n * d:4 * width + (n + 1) * d] + bg_ref[n:n + 1, :])
        merged = merged + gate * _dot(yb, wbr_ref[n])
    x1 = x + _dot(merged.astype(BF16), wout_ref[...])
    x1_ref[rows, :] = x1
    hi, lo = _split_bf16(_rms(x1, g2_ref[...], NORM_EPS))
    h2_ref[rows, :] = hi
    hi_ref[rows, :] = hi
    lo_ref[rows, :] = lo
    return uc[tm - SUBLANES:tm, :]


def _route(hi, lo, wrh_ref, wrl_ref, br_ref):
    tm = hi.shape[0]
    logits = _dot(hi, wrh_ref[...]) + _dot(hi, wrl_ref[...]) + _dot(lo, wrh_ref[...]) + br_ref[...]
    lane = lax.broadcasted_iota(I32, (tm, LANES), 1)
    vals, idxs = [], []
    onehot = jnp.zeros((tm, LANES), F32)
    for _ in range(TOP_K):
        m = jnp.max(logits, axis=-1, keepdims=True)
        idx = jnp.min(jnp.where(logits == m, lane, LANES), axis=-1, keepdims=True)
        sel = lane == idx
        logits = jnp.where(sel, -jnp.inf, logits)
        onehot = onehot + sel.astype(F32)
        vals.append(m)
        idxs.append(idx)
    es = [jnp.exp(v - vals[0]) for v in vals]
    den = es[0]
    for e in es[1:]:
        den = den + e
    return onehot, idxs, [e / den for e in es]


def _place(i, onehot, idxs, tws, ltri_ref, utri_ref, lpos_ref, lpost_ref, tw_ref, tcnt_ref, tbase_ref,
           carry_ref, rows, k):
    tm = onehot.shape[0]
    lane = lax.broadcasted_iota(I32, (tm, LANES), 1)
    cnt = jnp.sum(onehot, axis=0, keepdims=True)
    span = jnp.floor((cnt + (SUBLANES - 1)) * (1.0 / SUBLANES)) * SUBLANES
    s_hi, s_lo = _split_bf16(jnp.broadcast_to(span, (SUBLANES, LANES)))
    seg_start = (_dot(s_hi, utri_ref[...]) + _dot(s_lo, utri_ref[...]))[0:1, :]
    where_to = _dot(ltri_ref[...], onehot.astype(BF16)) + seg_start
    tcnt_ref[k] = jnp.broadcast_to(span, (SUBLANES, LANES))
    tbase_ref[k] = carry_ref[...]
    carry_ref[...] = jnp.where(i > 0, carry_ref[...] + span, 0.0)

    lpos_o = jnp.zeros((tm, LANES), F32)
    tw_o = jnp.zeros((tm, LANES), F32)
    for kk in range(TOP_K):
        lpos = jnp.sum(jnp.where(lane == idxs[kk], where_to, 0.0), axis=-1, keepdims=True)
        lpos_o = jnp.where(lane == kk, lpos, lpos_o)
        tw_o = jnp.where(lane == kk, tws[kk], tw_o)
    lpos_ref[rows, :] = lpos_o.astype(I32)
    lpost_ref[k] = lpos_o.T[:SUBLANES, :].astype(I32)
    tw_ref[rows, :] = tw_o


def _merge(x2, y_fox, mk, mv, g1, wcat, bg, cw, xqg, wbr, wout, g2, wrh, wrl, br,
           seq, n_mem, width, xa_heads, xa_hd):
    t, d = x2.shape
    tm = min(TM_WINDOW, seq)
    nwin = min(MERGE_WINDOWS, seq // tm)
    tb = nwin * tm
    nt = t // tb
    tpb = seq // tb
    ltri = (jnp.arange(tm)[:, None] > jnp.arange(tm)[None, :]).astype(BF16)
    utri = (jnp.arange(LANES)[:, None] < jnp.arange(LANES)[None, :]).astype(BF16)
    mix = lambda i: jnp.minimum(i, nt - 1)
    rte = lambda i: jnp.maximum(i - 1, 0)
    row = lambda i: (mix(i), 0)
    rrow = lambda i: (rte(i), 0)
    tile3 = lambda i: (rte(i), 0, 0)
    mem_map = lambda i: (mix(i) // tpb, 0)
    return pl.pallas_call(
        functools.partial(_merge_kernel, steps_per_batch=tpb, width=width, xa_heads=xa_heads, xa_hd=xa_hd),
        grid=(nt + 1,),
        in_specs=[pl.BlockSpec((tb, d), row), pl.BlockSpec((tb, width), row),
                  pl.BlockSpec((n_mem, width), mem_map), pl.BlockSpec((n_mem, width), mem_map),
                  _const_spec((1, d)), _const_spec(wcat.shape), _const_spec(bg.shape), _const_spec(cw.shape),
                  _const_spec((1, xa_hd)), _const_spec(wbr.shape), _const_spec((d, d)), _const_spec((1, d)),
                  _const_spec((d, LANES)), _const_spec((d, LANES)), _const_spec((1, LANES)),
                  _const_spec((tm, tm)), _const_spec((LANES, LANES))],
        out_specs=[pl.BlockSpec((tb, d), row), pl.BlockSpec((tb, d), row),
                   pl.BlockSpec((tb, LANES), rrow), pl.BlockSpec((nwin, SUBLANES, tm), tile3),
                   pl.BlockSpec((tb, LANES), rrow),
                   pl.BlockSpec((nwin, SUBLANES, LANES), tile3), pl.BlockSpec((nwin, SUBLANES, LANES), tile3)],
        out_shape=[jax.ShapeDtypeStruct((t, d), F32), jax.ShapeDtypeStruct((t, d), BF16),
                   jax.ShapeDtypeStruct((t, LANES), I32), jax.ShapeDtypeStruct((t // tm, SUBLANES, tm), I32),
                   jax.ShapeDtypeStruct((t, LANES), F32),
                   jax.ShapeDtypeStruct((t // tm, SUBLANES, LANES), F32),
                   jax.ShapeDtypeStruct((t // tm, SUBLANES, LANES), F32)],
        scratch_shapes=[pltpu.VMEM((SUBLANES, width), F32), pltpu.VMEM((SUBLANES, width), F32),
                        pltpu.VMEM((SUBLANES, LANES), F32), pltpu.VMEM((tb, d), BF16), pltpu.VMEM((tb, d), BF16)],
        compiler_params=_params(1),
        name="merge",
    )(x2, y_fox, mk, mv, g1, wcat, bg, cw, xqg, wbr, wout, g2, wrh, wrl, br, ltri, utri)


def _block_rows(tm, n_exp):
    return -(-(TOP_K * tm + (SUBLANES - 1) * n_exp) // MXU_DIM) * MXU_DIM


def _chunked_copies(src_ref, src0, dst_ref, dst0, n, sem, max_rows, wait, fixed_src=False):
    for b in range(max_rows.bit_length() - 1, SUBLANES.bit_length() - 2, -1):
        size = 1 << b
        off = (n >> (b + 1)) << (b + 1)

        @pl.when((n >> b) & 1 == 1)
        def _():
            src = 0 if fixed_src else pl.multiple_of(src0 + off, SUBLANES)
            cp = pltpu.make_async_copy(src_ref.at[pl.ds(src, size)],
                                       dst_ref.at[pl.ds(pl.multiple_of(dst0 + off, SUBLANES), size)], sem)
            if wait:
                cp.wait()
            else:
                cp.start()


def _chunk_classes(tm):
    return [1 << b for b in range(SUBLANES.bit_length() - 1, tm.bit_length())]


def _chunk_table(seg_len, seg_src, seg_dst, tm, n_exp):
    sizes = jnp.asarray(_chunk_classes(tm), I32)
    bit = (seg_len[:, :, None] // sizes) % 2
    off = seg_len[:, :, None] // (2 * sizes) * (2 * sizes)
    slot = jnp.cumsum(bit, axis=1) - 1
    hit = (slot[:, :, None, :] == jnp.arange(n_exp, dtype=I32)[None, None, :, None]) & (bit[:, :, None, :] == 1)
    pack = lambda start: jnp.sum(jnp.where(hit, (start[:, :, None] + off)[:, :, None, :], 0), axis=1)
    nw = seg_len.shape[0]
    cols = [jnp.swapaxes(pack(seg_src), 1, 2).reshape(nw, -1), jnp.swapaxes(pack(seg_dst), 1, 2).reshape(nw, -1),
            jnp.sum(bit, axis=1)]
    tbl = jnp.concatenate(cols, axis=1).astype(I32)
    assert tbl.shape[1] <= CHUNK_TABLE_LEN
    return jnp.pad(tbl, ((0, 0), (0, CHUNK_TABLE_LEN - tbl.shape[1]))).reshape(-1)


def _table_copies(tbl_ref, k, block_ref, sorted_ref, sem, tm, n_exp, to_sorted, wait):
    sizes = _chunk_classes(tm)
    base = k * CHUNK_TABLE_LEN
    for c, size in enumerate(sizes):
        def body(i, carry, c=c, size=size):
            blk = block_ref.at[pl.ds(pl.multiple_of(tbl_ref[base + c * n_exp + i], SUBLANES), size)]
            srt = sorted_ref.at[
                pl.ds(pl.multiple_of(tbl_ref[base + (len(sizes) + c) * n_exp + i], SUBLANES), size)]
            cp = pltpu.make_async_copy(blk, srt, sem) if to_sorted else pltpu.make_async_copy(srt, blk, sem)
            if wait:
                cp.wait()
            else:
                cp.start()
            return carry
        lax.fori_loop(0, tbl_ref[base + 2 * len(sizes) * n_exp + c], body, 0)


def _dispatch_kernel(pad_start_ref, pad_cnt_ref, n_used_ref, tbl_ref, tbl_prev_ref, lpost_ref, h2_ref, xs_ref,
                     buf_ref, zero_ref, sems, zsem, *, n_exp, n_tiles):
    s = pl.program_id(0)
    ns = pl.num_programs(0)
    n_win, _, tm = lpost_ref.shape
    rows = buf_ref.shape[1]
    mine = (s % 2) * n_win
    other = n_win - mine

    def clear(wait):
        def per_expert(e, c):
            _chunked_copies(zero_ref, 0, xs_ref, pad_start_ref[e], pad_cnt_ref[e], zsem, ZERO_ROWS, wait,
                            fixed_src=True)
            return c

        def per_tile(i, c):
            for part in range(TM_EXPERT // ZERO_ROWS):
                cp = pltpu.make_async_copy(
                    zero_ref, xs_ref.at[pl.ds(i * TM_EXPERT + part * ZERO_ROWS, ZERO_ROWS)], zsem)
                if wait:
                    cp.wait()
                else:
                    cp.start()
            return c

        lax.fori_loop(0, n_exp, per_expert, 0)
        lax.fori_loop(n_used_ref[0], n_tiles, per_tile, 0)

    @pl.when(s == 0)
    def _():
        zero_ref[...] = jnp.zeros_like(zero_ref)
        clear(False)

    r_iota = lax.broadcasted_iota(I32, (rows, tm), 0)
    for k in range(n_win):
        sel = r_iota == lpost_ref[k, 0:1, :]
        for kk in range(1, TOP_K):
            sel = sel | (r_iota == lpost_ref[k, kk:kk + 1, :])
        perm = jnp.where(sel, 1.0, 0.0).astype(BF16)
        buf_ref[mine + k] = _dot(perm, h2_ref[k * tm:(k + 1) * tm, :])

    def window_copies(table_ref, first_slot, wait):
        for k in range(n_win):
            _table_copies(table_ref, k, buf_ref.at[first_slot + k], xs_ref, sems.at[first_slot + k], tm, n_exp,
                          True, wait)

    window_copies(tbl_ref, mine, False)

    @pl.when(s >= 1)
    def _():
        window_copies(tbl_prev_ref, other, True)

    @pl.when(s == ns - 1)
    def _():
        window_copies(tbl_ref, mine, True)
        clear(True)


def _dispatch(pad_start, pad_cnt, n_used, chunk_tbl, lpost, h2, n_tiles, n_exp):
    t, d = h2.shape
    nw, _, tm = lpost.shape
    n_win = min(ROUTE_WINDOWS, nw)
    smem_blk = lambda shift: pl.BlockSpec((n_win * CHUNK_TABLE_LEN,), lambda i, *_: (jnp.maximum(i + shift, 0),),
                                          memory_space=pltpu.SMEM)
    grid_spec = pltpu.PrefetchScalarGridSpec(
        num_scalar_prefetch=3,
        grid=(nw // n_win,),
        in_specs=[smem_blk(0), smem_blk(-1),
                  pl.BlockSpec((n_win, SUBLANES, tm), lambda i, *_: (i, 0, 0)),
                  pl.BlockSpec((n_win * tm, d), lambda i, *_: (i, 0))],
        out_specs=pl.BlockSpec(memory_space=pl.ANY),
        scratch_shapes=[pltpu.VMEM((2 * n_win, _block_rows(tm, n_exp), d), F32), pltpu.VMEM((ZERO_ROWS, d), F32),
                        pltpu.SemaphoreType.DMA((2 * n_win,)), pltpu.SemaphoreType.DMA(())],
    )
    return pl.pallas_call(
        functools.partial(_dispatch_kernel, n_exp=n_exp, n_tiles=n_tiles),
        grid_spec=grid_spec,
        out_shape=jax.ShapeDtypeStruct((n_tiles * TM_EXPERT, d), F32),
        compiler_params=_params(1),
        name="dispatch",
    )(pad_start, pad_cnt, n_used, chunk_tbl, chunk_tbl, lpost, h2)


def _expert_kernel(tile_end_ref, wup_ref, bup_ref, wdn_ref, bdn_ref, xs_ref, ys_ref,
                   wup_bf_ref, wdn_bf_ref, xbuf_ref, ybuf_ref, zero_ref, xsems, ysems, zsem, *, d_ff, n_tiles):
    e = pl.program_id(0)
    n_exp = pl.num_programs(0)
    tm = xbuf_ref.shape[1]
    n_used = tile_end_ref[n_exp - 1]
    first = jnp.where(e == 0, 0, tile_end_ref[jnp.maximum(e - 1, 0)])
    last = tile_end_ref[e]

    def rows_of(g):
        return pl.ds(pl.multiple_of(g * tm, tm), tm)

    def x_copy(g):
        return pltpu.make_async_copy(xs_ref.at[rows_of(g)], xbuf_ref.at[g % 2], xsems.at[g % 2])

    def y_copy(g):
        return pltpu.make_async_copy(ybuf_ref.at[g % 2], ys_ref.at[rows_of(g)], ysems.at[g % 2])

    def clear_copy(g):
        return pltpu.make_async_copy(zero_ref, ys_ref.at[rows_of(g)], zsem)

    @pl.when(e == 0)
    def _():
        zero_ref[...] = jnp.zeros_like(zero_ref)
        lax.fori_loop(n_used, n_tiles, lambda g, c: (clear_copy(g).start(), c)[1], 0)

        @pl.when(n_used > 0)
        def _():
            x_copy(0).start()

    wup_bf_ref[...] = wup_ref[0].astype(BF16)
    wdn_bf_ref[...] = wdn_ref[0].astype(BF16)

    def tile(g, carry):
        slot = g % 2
        x_copy(g).wait()

        @pl.when(g + 1 < n_used)
        def _():
            x_copy(g + 1).start()

        @pl.when(g >= 2)
        def _():
            y_copy(g - 2).wait()

        h = _dot(xbuf_ref[slot].astype(BF16), wup_bf_ref[...]) + bup_ref[0]
        glu = jnp.minimum(h[:, :d_ff], SWIGLU_LIMIT)
        lin = jnp.clip(h[:, d_ff:], -SWIGLU_LIMIT, SWIGLU_LIMIT)
        a = glu * jax.nn.sigmoid(SWIGLU_ALPHA * glu) * (lin + 1.0)
        ybuf_ref[slot] = _dot(a.astype(BF16), wdn_bf_ref[...]) + bdn_ref[0]
        y_copy(g).start()
        return carry

    lax.fori_loop(first, last, tile, 0)

    @pl.when(e == n_exp - 1)
    def _():
        @pl.when(n_used >= 2)
        def _():
            y_copy(n_used - 2).wait()

        @pl.when(n_used >= 1)
        def _():
            y_copy(n_used - 1).wait()

        lax.fori_loop(n_used, n_tiles, lambda g, c: (clear_copy(g).wait(), c)[1], 0)


def _experts(tile_end, xs, wup, bup, wdn, bdn):
    n_rows, d = xs.shape
    n_exp, d_ff, _ = wdn.shape
    tm = TM_EXPERT
    exp3 = lambda e, *_: (e, 0, 0)
    grid_spec = pltpu.PrefetchScalarGridSpec(
        num_scalar_prefetch=1,
        grid=(n_exp,),
        in_specs=[pl.BlockSpec((1, d, 2 * d_ff), exp3), pl.BlockSpec((1, 1, 2 * d_ff), exp3),
                  pl.BlockSpec((1, d_ff, d), exp3), pl.BlockSpec((1, 1, d), exp3),
                  pl.BlockSpec(memory_space=pl.ANY)],
        out_specs=pl.BlockSpec(memory_space=pl.ANY),
        scratch_shapes=[pltpu.VMEM((d, 2 * d_ff), BF16), pltpu.VMEM((d_ff, d), BF16),
                        pltpu.VMEM((2, tm, d), F32), pltpu.VMEM((2, tm, d), F32), pltpu.VMEM((tm, d), F32),
                        pltpu.SemaphoreType.DMA((2,)), pltpu.SemaphoreType.DMA((2,)),
                        pltpu.SemaphoreType.DMA(())],
    )
    return pl.pallas_call(
        functools.partial(_expert_kernel, d_ff=d_ff, n_tiles=n_rows // tm),
        grid_spec=grid_spec,
        out_shape=jax.ShapeDtypeStruct((n_rows, d), F32),
        compiler_params=_params(1),
        name="experts",
    )(tile_end, wup, bup, wdn, bdn, xs)


def _combine_kernel(tbl_ref, tbl_next_ref, lpos_ref, tw_ref, x1_ref, ys_ref, o_ref, buf_ref, sems,
                    *, n_exp, n_win):
    s = pl.program_id(0)
    ns = pl.num_programs(0)
    tm = x1_ref.shape[0] // n_win
    rows = buf_ref.shape[1]
    mine = (s % 2) * n_win
    other = n_win - mine

    def window_copies(table_ref, first_slot, wait):
        for k in range(n_win):
            _table_copies(table_ref, k, buf_ref.at[first_slot + k], ys_ref, sems.at[first_slot + k], tm, n_exp,
                          False, wait)

    @pl.when(s == 0)
    def _():
        buf_ref[...] = jnp.zeros_like(buf_ref)
        window_copies(tbl_ref, 0, False)

    @pl.when(s + 1 < ns)
    def _():
        window_copies(tbl_next_ref, other, False)

    window_copies(tbl_ref, mine, True)

    c_iota = lax.broadcasted_iota(I32, (tm, rows), 1)
    for k in range(n_win):
        sl = slice(k * tm, (k + 1) * tm)
        lpos = lpos_ref[sl, :]
        tw = tw_ref[sl, :]
        wperm = jnp.zeros((tm, rows), F32)
        for kk in range(TOP_K):
            wperm = jnp.where(c_iota == lpos[:, kk:kk + 1], tw[:, kk:kk + 1], wperm)
        o_ref[sl, :] = x1_ref[sl, :] + _dot(wperm.astype(BF16), buf_ref[mine + k].astype(BF16))


def _combine(chunk_tbl, lpos, tw, x1, ys, n_exp):
    t, d = x1.shape
    tm = min(TM_WINDOW, t)
    n_win = min(ROUTE_WINDOWS, t // tm)
    ns = t // (n_win * tm)
    row = lambda i: (i, 0)
    smem_blk = lambda shift: pl.BlockSpec((n_win * CHUNK_TABLE_LEN,), lambda i: (jnp.minimum(i + shift, ns - 1),),
                                          memory_space=pltpu.SMEM)
    return pl.pallas_call(
        functools.partial(_combine_kernel, n_exp=n_exp, n_win=n_win),
        grid=(ns,),
        in_specs=[smem_blk(0), smem_blk(1),
                  pl.BlockSpec((n_win * tm, LANES), row), pl.BlockSpec((n_win * tm, LANES), row),
                  pl.BlockSpec((n_win * tm, d), row), pl.BlockSpec(memory_space=pl.ANY)],
        out_specs=pl.BlockSpec((n_win * tm, d), row),
        out_shape=jax.ShapeDtypeStruct((t, d), F32),
        scratch_shapes=[pltpu.VMEM((2 * n_win, _block_rows(tm, n_exp), d), F32),
                        pltpu.SemaphoreType.DMA((2 * n_win,))],
        compiler_params=_params(1),
        name="combine",
    )(chunk_tbl, chunk_tbl, lpos, tw, x1, ys)


def _pad_lanes(a, value=0.0):
    return jnp.pad(a, ((0, 0), (0, LANES - a.shape[-1])), constant_values=value)


def _layer(x, mem, norm1_g, w_in, b_forget, b_gate, fox_q_g, fox_k_g, conv_w, mem_norm_g, w_mem_kv,
           xa_q_g, xa_k_g, w_branch, w_out, norm2_g, w_router, b_router, w_up, b_up, w_down, b_down):
    batch, seq, d = x.shape
    n_mem = mem.shape[1]
    n_branch, width, _ = w_branch.shape
    n_fh = b_forget.shape[0]
    hd = fox_q_g.shape[0]
    xa_hd = xa_q_g.shape[0]
    xa_heads = width // xa_hd
    n_exp = w_router.shape[1]
    assert n_fh * hd == width and n_branch == 3 and conv_w.shape[0] == 3
    assert LANES % hd == 0 and width % MXU_DIM == 0 and n_exp <= LANES and n_fh <= LANES
    fh_pad = -(-n_fh // SUBLANES) * SUBLANES
    t = batch * seq
    x2 = x.reshape(t, d)
    row1 = lambda a: a.reshape(1, -1).astype(F32)

    o_f = 3 * width
    o_conv = o_f + n_fh
    wqkv = w_in[:, :o_f].astype(BF16)
    wf = _pad_lanes(w_in[:, o_f:o_conv]).astype(BF16)
    wcat = w_in[:, o_conv:].astype(BF16)

    mk, mv = _memkv(mem.reshape(batch * n_mem, d), row1(mem_norm_g), w_mem_kv.astype(BF16), row1(xa_k_g),
                    xa_heads, xa_hd)
    q, k, v, cumt = _qkv(x2, row1(norm1_g), wqkv, wf, _pad_lanes(row1(b_forget)),
                         row1(jnp.tile(fox_q_g, n_fh)), row1(jnp.tile(fox_k_g, n_fh)),
                         seq, width, hd, fh_pad)
    y_fox = _fox(q, k, v, cumt, batch, seq, width, hd, fh_pad)

    wr = _pad_lanes(w_router.astype(F32))
    wr_hi = wr.astype(BF16)
    wr_lo = (wr - wr_hi.astype(F32)).astype(BF16)
    x1, h2, lpos, lpost, tw, tcnt, tbase = _merge(
        x2, y_fox, mk, mv, row1(norm1_g), wcat, b_gate.astype(F32), conv_w.astype(F32), row1(xa_q_g),
        w_branch.astype(BF16), w_out.astype(BF16), row1(norm2_g), wr_hi, wr_lo,
        _pad_lanes(row1(b_router), NEG_BIG), seq, n_mem, width, xa_heads, xa_hd)

    seg_len = tcnt[:, 0, :n_exp].astype(I32)
    seg_base = tbase[:, 0, :n_exp].astype(I32)
    counts = seg_base[-1] + seg_len[-1]
    tiles_per = (counts + TM_EXPERT - 1) // TM_EXPERT
    tile_end = jnp.cumsum(tiles_per)
    group_start = (tile_end - tiles_per) * TM_EXPERT
    seg_dst = group_start[None, :] + seg_base
    seg_src = jnp.cumsum(seg_len, axis=1) - seg_len
    n_tiles = -(-(t * TOP_K + (SUBLANES - 1) * seg_len.size) // TM_EXPERT) + n_exp
    n_used = tile_end[-1:].astype(I32)
    pad_start = (group_start + counts).astype(I32)
    pad_cnt = (tiles_per * TM_EXPERT - counts).astype(I32)
    chunk_tbl = _chunk_table(seg_len, seg_src.astype(I32), seg_dst.astype(I32), lpost.shape[2], n_exp)

    xs = _dispatch(pad_start, pad_cnt, n_used, chunk_tbl, lpost, h2, n_tiles, n_exp)
    ys = _experts(tile_end.astype(I32), xs, w_up.astype(F32), b_up[:, None, :].astype(F32),
                  w_down.astype(F32), b_down[:, None, :].astype(F32))
    out = _combine(chunk_tbl, lpos, tw, x1, ys, n_exp)
    return out.reshape(batch, seq, d)


def kernel(x, mem, norm1_g, w_in, b_forget, b_gate, fox_q_g, fox_k_g, conv_w, mem_norm_g, w_mem_kv, xa_q_g,
           xa_k_g, w_branch, w_out, norm2_g, w_router, b_router, w_up, b_up, w_down, b_down):
    for l in range(norm1_g.shape[0]):
        x = _layer(x, mem, norm1_g[l], w_in[l], b_forget[l], b_gate[l], fox_q_g[l], fox_k_g[l], conv_w[l],
                   mem_norm_g[l], w_mem_kv[l], xa_q_g[l], xa_k_g[l], w_branch[l], w_out[l], norm2_g[l],
                   w_router[l], b_router[l], w_up[l], b_up[l], w_down[l], b_down[l])
    return x
```

```python
import functools

import jax
import jax.numpy as jnp
from jax import lax
from jax.experimental import pallas as pl
from jax.experimental.pallas import tpu as pltpu

F32 = jnp.float32
BF16 = jnp.bfloat16
I32 = jnp.int32

NORM_EPS = 1e-5
QK_EPS = 1e-6
TOP_K = 4
SWIGLU_ALPHA = 1.702
SWIGLU_LIMIT = 7.0
NEG_BIG = -1e30
LOG2_E = 1.4426950408889634

LANES = 128
SUBLANES = 8
MXU_DIM = 256
VMEM_LIMIT_BYTES = 56 * 1024 * 1024

TM_QKV = 512
QKV_CHUNKS = 4
TQ_FOX = 512
TM_WINDOW = 256
MERGE_WINDOWS = 2
ROUTE_WINDOWS = 2
TM_EXPERT = 512
ZERO_ROWS = 256
CHUNK_TABLE_LEN = 512


def _params(n_axes):
    return pltpu.CompilerParams(dimension_semantics=("arbitrary",) * n_axes,
                                vmem_limit_bytes=VMEM_LIMIT_BYTES)


def _const_spec(shape):
    zeros = (0,) * len(shape)
    return pl.BlockSpec(shape, lambda *_: zeros, pipeline_mode=pl.Buffered(1))


def _rms(x, g, eps):
    return x * lax.rsqrt(jnp.mean(x * x, axis=-1, keepdims=True) + eps) * g


def _split_bf16(x):
    hi = x.astype(BF16)
    lo = (x - hi.astype(F32)).astype(BF16)
    return hi, lo


def _dot(a, b):
    return jnp.dot(a, b, preferred_element_type=F32)


def _dot_nt(a, b):
    return lax.dot_general(a, b, (((1,), (1,)), ((), ())), preferred_element_type=F32)


def _memkv_kernel(mem_ref, g_ref, w_ref, kg_ref, mk_ref, mv_ref, *, n_heads, hd):
    y = _rms(mem_ref[...], g_ref[...], NORM_EPS)
    kv = _dot(y.astype(BF16), w_ref[...])
    width = n_heads * hd
    for h in range(n_heads):
        sl = slice(h * hd, (h + 1) * hd)
        mk_ref[:, sl] = _rms(kv[:, sl], kg_ref[...], QK_EPS).astype(BF16)
    mv_ref[...] = kv[:, width:].astype(BF16)


def _memkv(mem2, g, w_kv, kg, n_heads, hd):
    rows, d = mem2.shape
    width = n_heads * hd
    tm = min(rows, 512)
    return pl.pallas_call(
        functools.partial(_memkv_kernel, n_heads=n_heads, hd=hd),
        grid=(rows // tm,),
        in_specs=[pl.BlockSpec((tm, d), lambda i: (i, 0)),
                  _const_spec((1, d)), _const_spec((d, 2 * width)), _const_spec((1, hd))],
        out_specs=[pl.BlockSpec((tm, width), lambda i: (i, 0)),
                   pl.BlockSpec((tm, width), lambda i: (i, 0))],
        out_shape=[jax.ShapeDtypeStruct((rows, width), BF16)] * 2,
        compiler_params=_params(1),
        name="memkv",
    )(mem2, g, w_kv, kg)


def _qkv_kernel(x_ref, g1_ref, wqkv_ref, wf_ref, bf_ref, gq_ref, gk_ref, bd_ref, tri_ref,
                q_ref, k_ref, v_ref, cumt_ref, carry_ref,
                *, tiles_per_batch, width, hd, fh_pad):
    i = pl.program_id(0)
    tm = tri_ref.shape[0]

    @pl.when(i % tiles_per_batch == 0)
    def _():
        carry_ref[...] = jnp.zeros_like(carry_ref)

    def head_norm(z, g):
        sq = (z * z).astype(BF16)
        parts = []
        for c in range(width // MXU_DIM):
            sl = slice(c * MXU_DIM, (c + 1) * MXU_DIM)
            parts.append(_dot(sq[:, sl], bd_ref[...]))
        ms = jnp.concatenate(parts, axis=-1) * (1.0 / hd)
        return z * lax.rsqrt(ms + QK_EPS) * g

    prefix = carry_ref[0:1, :]
    for c in range(x_ref.shape[0] // tm):
        rows = slice(c * tm, (c + 1) * tm)
        hb = _rms(x_ref[rows, :], g1_ref[...], NORM_EPS).astype(BF16)
        qkv = _dot(hb, wqkv_ref[...])
        q = head_norm(qkv[:, :width], gq_ref[...]) * (hd ** -0.5 * LOG2_E)
        k = head_norm(qkv[:, width:2 * width], gk_ref[...])
        q_ref[rows, :] = q.astype(BF16)
        k_ref[rows, :] = k.astype(BF16)
        v_ref[rows, :] = qkv[:, 2 * width:].astype(BF16)

        z = _dot(hb, wf_ref[...]) + bf_ref[...]
        log_f = jnp.minimum(z, 0.0) - jnp.log(1.0 + jnp.exp(-jnp.abs(z)))
        hi, lo = _split_bf16(log_f)
        cum = _dot(tri_ref[...], hi) + _dot(tri_ref[...], lo) + prefix
        prefix = cum[tm - 1:tm, :]
        cumt_ref[:, rows] = (cum * LOG2_E).T[:fh_pad, :]
    carry_ref[...] = jnp.broadcast_to(prefix, carry_ref.shape)


def _qkv(x2, g1, wqkv, wf, bf, gq_t, gk_t, seq, width, hd, fh_pad):
    t, d = x2.shape
    tm = min(TM_QKV, seq)
    tb = min(QKV_CHUNKS * tm, seq)
    bd = (jnp.arange(MXU_DIM)[:, None] // hd == jnp.arange(MXU_DIM)[None, :] // hd).astype(BF16)
    tri = (jnp.arange(tm)[:, None] >= jnp.arange(tm)[None, :]).astype(BF16)
    row = lambda i: (i, 0)
    return pl.pallas_call(
        functools.partial(_qkv_kernel, tiles_per_batch=seq // tb, width=width, hd=hd, fh_pad=fh_pad),
        grid=(t // tb,),
        in_specs=[pl.BlockSpec((tb, d), row), _const_spec((1, d)), _const_spec((d, 3 * width)),
                  _const_spec((d, LANES)), _const_spec((1, LANES)), _const_spec((1, width)),
                  _const_spec((1, width)), _const_spec((MXU_DIM, MXU_DIM)), _const_spec((tm, tm))],
        out_specs=[pl.BlockSpec((tb, width), row)] * 3 + [pl.BlockSpec((fh_pad, tb), lambda i: (0, i))],
        out_shape=[jax.ShapeDtypeStruct((t, width), BF16)] * 3 + [jax.ShapeDtypeStruct((fh_pad, t), F32)],
        scratch_shapes=[pltpu.VMEM((SUBLANES, LANES), F32)],
        compiler_params=_params(1),
        name="qkv",
    )(x2, g1, wqkv, wf, bf, gq_t, gk_t, bd, tri)


def _fox_kernel(q_ref, k_ref, v_ref, cumt_ref, o_ref, s_ref, *, tq, hd):
    hp = pl.program_id(1)
    seq = q_ref.shape[0]
    nh = LANES // hd
    lane = lax.broadcasted_iota(I32, (1, LANES), 1)
    on_or_below_diag = lax.broadcasted_iota(I32, (tq, tq), 1) <= lax.broadcasted_iota(I32, (tq, tq), 0)
    heads = [(hp * nh + hh, (lane >= hh * hd) & (lane < (hh + 1) * hd)) for hh in range(nh)]

    def scores(i, j, slot):
        q = q_ref[i * tq:(i + 1) * tq, :]
        kj = k_ref[j * tq:(j + 1) * tq, :]
        for hh, (h, in_head) in enumerate(heads):
            qh = jnp.where(in_head, q, jnp.zeros_like(q))
            s_ref[slot, hh] = _dot_nt(qh, kj) - cumt_ref[pl.ds(h, 1), j * tq:(j + 1) * tq]

    def consume(j, slot, carry, masked):
        vj = v_ref[j * tq:(j + 1) * tq, :]
        out = []
        for hh, ((_, in_head), (m, acc)) in enumerate(zip(heads, carry)):
            vh = jnp.where(in_head, vj, jnp.ones_like(vj))
            s = s_ref[slot, hh]
            if masked:
                s = jnp.where(on_or_below_diag, s, NEG_BIG)
            m_new = jnp.maximum(m, jnp.max(s, axis=-1, keepdims=True))
            p = jnp.exp2(s - m_new)
            out.append((m_new, jnp.exp2(m - m_new) * acc + _dot(p.astype(BF16), vh)))
        return tuple(out)

    tiles = [(i, j) for i in range(seq // tq) for j in range(i + 1)]
    scores(*tiles[0], 0)
    carry = None
    for n, (i, j) in enumerate(tiles):
        if n + 1 < len(tiles):
            scores(*tiles[n + 1], (n + 1) % 2)
        if j == 0:
            carry = tuple((jnp.full((tq, 1), NEG_BIG, F32), jnp.zeros((tq, LANES), F32)) for _ in heads)
        carry = consume(j, n % 2, carry, j == i)
        if j == i:
            o = None
            for hh in range(nh - 1, -1, -1):
                acc = carry[hh][1]
                other = ((hh + 1) % nh) * hd
                val = acc / acc[:, other:other + 1]
                o = val if o is None else jnp.where(lane < (hh + 1) * hd, val, o)
            o_ref[i * tq:(i + 1) * tq, :] = o.astype(BF16)


def _fox(q, k, v, cumt, batch, seq, width, hd, fh_pad):
    t = q.shape[0]
    tq = min(TQ_FOX, seq)
    assert seq % tq == 0
    seqmap = lambda b, hp: (b, hp)
    return pl.pallas_call(
        functools.partial(_fox_kernel, tq=tq, hd=hd),
        grid=(batch, width // LANES),
        in_specs=[pl.BlockSpec((seq, LANES), seqmap), pl.BlockSpec((seq, LANES), seqmap),
                  pl.BlockSpec((seq, LANES), seqmap), pl.BlockSpec((fh_pad, seq), lambda b, hp: (0, b))],
        out_specs=pl.BlockSpec((seq, LANES), seqmap),
        out_shape=jax.ShapeDtypeStruct((t, width), BF16),
        scratch_shapes=[pltpu.VMEM((2, LANES // hd, tq, tq), F32)],
        compiler_params=_params(2),
        name="fox",
    )(q, k, v, cumt)


def _merge_kernel(x_ref, yf_ref, mk_ref, mv_ref, g1_ref, wcat_ref, bg_ref, cw_ref, xqg_ref,
                  wbr_ref, wout_ref, g2_ref, wrh_ref, wrl_ref, br_ref, ltri_ref, utri_ref,
                  x1_ref, h2_ref, lpos_ref, lpost_ref, tw_ref, tcnt_ref, tbase_ref,
                  prev_ref, halo_ref, carry_ref, hi_ref, lo_ref, *, steps_per_batch, width, xa_heads, xa_hd):
    i = pl.program_id(0)
    nt = pl.num_programs(0) - 1
    tm = ltri_ref.shape[0]
    n_win = x_ref.shape[0] // tm

    @pl.when(i == 0)
    def _():
        prev_ref[...] = jnp.zeros_like(prev_ref)
        halo_ref[...] = jnp.zeros_like(halo_ref)
        carry_ref[...] = jnp.zeros_like(carry_ref)
        hi_ref[...] = jnp.zeros_like(hi_ref)
        lo_ref[...] = jnp.zeros_like(lo_ref)

    routed = [_route(hi_ref[k * tm:(k + 1) * tm, :], lo_ref[k * tm:(k + 1) * tm, :], wrh_ref, wrl_ref, br_ref)
              for k in range(n_win)]

    halo = jnp.where(i < nt, jnp.where(i % steps_per_batch == 0, 0.0, prev_ref[...]), halo_ref[...])
    halo_ref[...] = halo

    for k in range(n_win):
        rows = slice(k * tm, (k + 1) * tm)
        place = functools.partial(_place, i, *routed[k], ltri_ref, utri_ref, lpos_ref, lpost_ref, tw_ref,
                                  tcnt_ref, tbase_ref, carry_ref, rows, k)
        halo = _mix_window(x_ref, yf_ref, mk_ref, mv_ref, g1_ref, wcat_ref, bg_ref, cw_ref, xqg_ref, wbr_ref,
                           wout_ref, g2_ref, x1_ref, h2_ref, hi_ref, lo_ref, rows, halo, place,
                           width=width, xa_heads=xa_heads, xa_hd=xa_hd)
    prev_ref[...] = halo


def _mix_window(x_ref, yf_ref, mk_ref, mv_ref, g1_ref, wcat_ref, bg_ref, cw_ref, xqg_ref, wbr_ref, wout_ref,
                g2_ref, x1_ref, h2_ref, hi_ref, lo_ref, rows, halo, place, *, width, xa_heads, xa_hd):
    tm = rows.stop - rows.start
    d = x_ref.shape[1]
    x = x_ref[rows, :]
    hb = _rms(x, g1_ref[...], NORM_EPS).astype(BF16)
    pc = _dot(hb, wcat_ref[...])

    uc = pc[:, :width] * pc[:, width:2 * width]
    row = lax.broadcasted_iota(I32, (tm, 1), 0)
    p1 = halo[SUBLANES - 1:SUBLANES, :]
    p2 = halo[SUBLANES - 2:SUBLANES - 1, :]
    m1 = jnp.where(row == 0, p1, pltpu.roll(uc, 1, 0))
    m2 = jnp.where(row == 0, p2, jnp.where(row == 1, p1, pltpu.roll(uc, 2, 0)))
    y_conv = pc[:, 2 * width:3 * width] * (cw_ref[0:1, :] * m2 + cw_ref[1:2, :] * m1 + cw_ref[2:3, :] * uc)

    ys = []
    for h in range(xa_heads):
        sl = slice(3 * width + h * xa_hd, 3 * width + (h + 1) * xa_hd)
        ml = slice(h * xa_hd, (h + 1) * xa_hd)
        qh = _rms(pc[:, sl], xqg_ref[...], QK_EPS) * (xa_hd ** -0.5)
        s = _dot_nt(qh.astype(BF16), mk_ref[:, ml])
        p = jnp.exp(s - jnp.max(s, axis=-1, keepdims=True))
        ys.append(_dot(p.astype(BF16), mv_ref[:, ml]) / jnp.sum(p, axis=-1, keepdims=True))
    y_mem = jnp.concatenate(ys, axis=-1)

    place()

    merged = jnp.zeros((tm, d), F32)
    for n, yb in enumerate((yf_ref[rows, :], y_conv.astype(BF16), y_mem.astype(BF16))):
        gate = jax.nn.sigmoid(pc[:, 4 * width + n * d:4 * width + (n + 1) * d] + bg_ref[n:n + 1, :])
        merged = merged + gate * _dot(yb, wbr_ref[n])
    x1 = x + _dot(merged.astype(BF16), wout_ref[...])
    x1_ref[rows, :] = x1
    hi, lo = _split_bf16(_rms(x1, g2_ref[...], NORM_EPS))
    h2_ref[rows, :] = hi
    hi_ref[rows, :] = hi
    lo_ref[rows, :] = lo
    return uc[tm - SUBLANES:tm, :]


def _route(hi, lo, wrh_ref, wrl_ref, br_ref):
    tm = hi.shape[0]
    logits = _dot(hi, wrh_ref[...]) + _dot(hi, wrl_ref[...]) + _dot(lo, wrh_ref[...]) + br_ref[...]
    lane = lax.broadcasted_iota(I32, (tm, LANES), 1)
    vals, idxs = [], []
    onehot = jnp.zeros((tm, LANES), F32)
    for _ in range(TOP_K):
        m = jnp.max(logits, axis=-1, keepdims=True)
        idx = jnp.min(jnp.where(logits == m, lane, LANES), axis=-1, keepdims=True)
        sel = lane == idx
        logits = jnp.where(sel, -jnp.inf, logits)
        onehot = onehot + sel.astype(F32)
        vals.append(m)
        idxs.append(idx)
    es = [jnp.exp(v - vals[0]) for v in vals]
    den = es[0]
    for e in es[1:]:
        den = den + e
    return onehot, idxs, [e / den for e in es]


def _place(i, onehot, idxs, tws, ltri_ref, utri_ref, lpos_ref, lpost_ref, tw_ref, tcnt_ref, tbase_ref,
           carry_ref, rows, k):
    tm = onehot.shape[0]
    lane = lax.broadcasted_iota(I32, (tm, LANES), 1)
    cnt = jnp.sum(onehot, axis=0, keepdims=True)
    span = jnp.floor((cnt + (SUBLANES - 1)) * (1.0 / SUBLANES)) * SUBLANES
    s_hi, s_lo = _split_bf16(jnp.broadcast_to(span, (SUBLANES, LANES)))
    seg_start = (_dot(s_hi, utri_ref[...]) + _dot(s_lo, utri_ref[...]))[0:1, :]
    where_to = _dot(ltri_ref[...], onehot.astype(BF16)) + seg_start
    tcnt_ref[k] = jnp.broadcast_to(span, (SUBLANES, LANES))
    tbase_ref[k] = carry_ref[...]
    carry_ref[...] = jnp.where(i > 0, carry_ref[...] + span, 0.0)

    lpos_o = jnp.zeros((tm, LANES), F32)
    tw_o = jnp.zeros((tm, LANES), F32)
    for kk in range(TOP_K):
        lpos = jnp.sum(jnp.where(lane == idxs[kk], where_to, 0.0), axis=-1, keepdims=True)
        lpos_o = jnp.where(lane == kk, lpos, lpos_o)
        tw_o = jnp.where(lane == kk, tws[kk], tw_o)
    lpos_ref[rows, :] = lpos_o.astype(I32)
    lpost_ref[k] = lpos_o.T[:SUBLANES, :].astype(I32)
    tw_ref[rows, :] = tw_o


def _merge(x2, y_fox, mk, mv, g1, wcat, bg, cw, xqg, wbr, wout, g2, wrh, wrl, br,
           seq, n_mem, width, xa_heads, xa_hd):
    t, d = x2.shape
    tm = min(TM_WINDOW, seq)
    nwin = min(MERGE_WINDOWS, seq // tm)
    tb = nwin * tm
    nt = t // tb
    tpb = seq // tb
    ltri = (jnp.arange(tm)[:, None] > jnp.arange(tm)[None, :]).astype(BF16)
    utri = (jnp.arange(LANES)[:, None] < jnp.arange(LANES)[None, :]).astype(BF16)
    mix = lambda i: jnp.minimum(i, nt - 1)
    rte = lambda i: jnp.maximum(i - 1, 0)
    row = lambda i: (mix(i), 0)
    rrow = lambda i: (rte(i), 0)
    tile3 = lambda i: (rte(i), 0, 0)
    mem_map = lambda i: (mix(i) // tpb, 0)
    return pl.pallas_call(
        functools.partial(_merge_kernel, steps_per_batch=tpb, width=width, xa_heads=xa_heads, xa_hd=xa_hd),
        grid=(nt + 1,),
        in_specs=[pl.BlockSpec((tb, d), row), pl.BlockSpec((tb, width), row),
                  pl.BlockSpec((n_mem, width), mem_map), pl.BlockSpec((n_mem, width), mem_map),
                  _const_spec((1, d)), _const_spec(wcat.shape), _const_spec(bg.shape), _const_spec(cw.shape),
                  _const_spec((1, xa_hd)), _const_spec(wbr.shape), _const_spec((d, d)), _const_spec((1, d)),
                  _const_spec((d, LANES)), _const_spec((d, LANES)), _const_spec((1, LANES)),
                  _const_spec((tm, tm)), _const_spec((LANES, LANES))],
        out_specs=[pl.BlockSpec((tb, d), row), pl.BlockSpec((tb, d), row),
                   pl.BlockSpec((tb, LANES), rrow), pl.BlockSpec((nwin, SUBLANES, tm), tile3),
                   pl.BlockSpec((tb, LANES), rrow),
                   pl.BlockSpec((nwin, SUBLANES, LANES), tile3), pl.BlockSpec((nwin, SUBLANES, LANES), tile3)],
        out_shape=[jax.ShapeDtypeStruct((t, d), F32), jax.ShapeDtypeStruct((t, d), BF16),
                   jax.ShapeDtypeStruct((t, LANES), I32), jax.ShapeDtypeStruct((t // tm, SUBLANES, tm), I32),
                   jax.ShapeDtypeStruct((t, LANES), F32),
                   jax.ShapeDtypeStruct((t // tm, SUBLANES, LANES), F32),
                   jax.ShapeDtypeStruct((t // tm, SUBLANES, LANES), F32)],
        scratch_shapes=[pltpu.VMEM((SUBLANES, width), F32), pltpu.VMEM((SUBLANES, width), F32),
                        pltpu.VMEM((SUBLANES, LANES), F32), pltpu.VMEM((tb, d), BF16), pltpu.VMEM((tb, d), BF16)],
        compiler_params=_params(1),
        name="merge",
    )(x2, y_fox, mk, mv, g1, wcat, bg, cw, xqg, wbr, wout, g2, wrh, wrl, br, ltri, utri)


def _block_rows(tm, n_exp):
    return -(-(TOP_K * tm + (SUBLANES - 1) * n_exp) // MXU_DIM) * MXU_DIM


def _chunked_copies(src_ref, src0, dst_ref, dst0, n, sem, max_rows, wait, fixed_src=False):
    for b in range(max_rows.bit_length() - 1, SUBLANES.bit_length() - 2, -1):
        size = 1 << b
        off = (n >> (b + 1)) << (b + 1)

        @pl.when((n >> b) & 1 == 1)
        def _():
            src = 0 if fixed_src else pl.multiple_of(src0 + off, SUBLANES)
            cp = pltpu.make_async_copy(src_ref.at[pl.ds(src, size)],
                                       dst_ref.at[pl.ds(pl.multiple_of(dst0 + off, SUBLANES), size)], sem)
            if wait:
                cp.wait()
            else:
                cp.start()


def _chunk_classes(tm):
    return [1 << b for b in range(SUBLANES.bit_length() - 1, tm.bit_length())]


def _chunk_table(seg_len, seg_src, seg_dst, tm, n_exp):
    sizes = jnp.asarray(_chunk_classes(tm), I32)
    bit = (seg_len[:, :, None] // sizes) % 2
    off = seg_len[:, :, None] // (2 * sizes) * (2 * sizes)
    slot = jnp.cumsum(bit, axis=1) - 1
    hit = (slot[:, :, None, :] == jnp.arange(n_exp, dtype=I32)[None, None, :, None]) & (bit[:, :, None, :] == 1)
    pack = lambda start: jnp.sum(jnp.where(hit, (start[:, :, None] + off)[:, :, None, :], 0), axis=1)
    nw = seg_len.shape[0]
    cols = [jnp.swapaxes(pack(seg_src), 1, 2).reshape(nw, -1), jnp.swapaxes(pack(seg_dst), 1, 2).reshape(nw, -1),
            jnp.sum(bit, axis=1)]
    tbl = jnp.concatenate(cols, axis=1).astype(I32)
    assert tbl.shape[1] <= CHUNK_TABLE_LEN
    return jnp.pad(tbl, ((0, 0), (0, CHUNK_TABLE_LEN - tbl.shape[1]))).reshape(-1)


def _table_copies(tbl_ref, k, block_ref, sorted_ref, sem, tm, n_exp, to_sorted, wait):
    sizes = _chunk_classes(tm)
    base = k * CHUNK_TABLE_LEN
    for c, size in enumerate(sizes):
        def body(i, carry, c=c, size=size):
            blk = block_ref.at[pl.ds(pl.multiple_of(tbl_ref[base + c * n_exp + i], SUBLANES), size)]
            srt = sorted_ref.at[
                pl.ds(pl.multiple_of(tbl_ref[base + (len(sizes) + c) * n_exp + i], SUBLANES), size)]
            cp = pltpu.make_async_copy(blk, srt, sem) if to_sorted else pltpu.make_async_copy(srt, blk, sem)
            if wait:
                cp.wait()
            else:
                cp.start(priority=k % 2)
            return carry
        lax.fori_loop(0, tbl_ref[base + 2 * len(sizes) * n_exp + c], body, 0)


def _dispatch_kernel(pad_start_ref, pad_cnt_ref, n_used_ref, tbl_ref, tbl_prev_ref, lpost_ref, h2_ref, xs_ref,
                     buf_ref, zero_ref, sems, zsem, *, n_exp, n_tiles):
    s = pl.program_id(0)
    ns = pl.num_programs(0)
    n_win, _, tm = lpost_ref.shape
    rows = buf_ref.shape[1]
    mine = (s % 2) * n_win
    other = n_win - mine

    def clear(wait):
        def per_expert(e, c):
            _chunked_copies(zero_ref, 0, xs_ref, pad_start_ref[e], pad_cnt_ref[e], zsem, ZERO_ROWS, wait,
                            fixed_src=True)
            return c

        def per_tile(i, c):
            for part in range(TM_EXPERT // ZERO_ROWS):
                cp = pltpu.make_async_copy(
                    zero_ref, xs_ref.at[pl.ds(i * TM_EXPERT + part * ZERO_ROWS, ZERO_ROWS)], zsem)
                if wait:
                    cp.wait()
                else:
                    cp.start()
            return c

        lax.fori_loop(0, n_exp, per_expert, 0)
        lax.fori_loop(n_used_ref[0], n_tiles, per_tile, 0)

    @pl.when(s == 0)
    def _():
        zero_ref[...] = jnp.zeros_like(zero_ref)
        clear(False)

    r_iota = lax.broadcasted_iota(I32, (rows, tm), 0)
    for k in range(n_win):
        sel = r_iota == lpost_ref[k, 0:1, :]
        for kk in range(1, TOP_K):
            sel = sel | (r_iota == lpost_ref[k, kk:kk + 1, :])
        perm = jnp.where(sel, 1.0, 0.0).astype(BF16)
        buf_ref[mine + k] = _dot(perm, h2_ref[k * tm:(k + 1) * tm, :])

    def window_copies(table_ref, first_slot, wait):
        for k in range(n_win):
            _table_copies(table_ref, k, buf_ref.at[first_slot + k], xs_ref, sems.at[first_slot + k], tm, n_exp,
                          True, wait)

    window_copies(tbl_ref, mine, False)

    @pl.when(s >= 1)
    def _():
        window_copies(tbl_prev_ref, other, True)

    @pl.when(s == ns - 1)
    def _():
        window_copies(tbl_ref, mine, True)
        clear(True)


def _dispatch(pad_start, pad_cnt, n_used, chunk_tbl, lpost, h2, n_tiles, n_exp):
    t, d = h2.shape
    nw, _, tm = lpost.shape
    n_win = min(ROUTE_WINDOWS, nw)
    smem_blk = lambda shift: pl.BlockSpec((n_win * CHUNK_TABLE_LEN,), lambda i, *_: (jnp.maximum(i + shift, 0),),
                                          memory_space=pltpu.SMEM)
    grid_spec = pltpu.PrefetchScalarGridSpec(
        num_scalar_prefetch=3,
        grid=(nw // n_win,),
        in_specs=[smem_blk(0), smem_blk(-1),
                  pl.BlockSpec((n_win, SUBLANES, tm), lambda i, *_: (i, 0, 0)),
                  pl.BlockSpec((n_win * tm, d), lambda i, *_: (i, 0))],
        out_specs=pl.BlockSpec(memory_space=pl.ANY),
        scratch_shapes=[pltpu.VMEM((2 * n_win, _block_rows(tm, n_exp), d), F32), pltpu.VMEM((ZERO_ROWS, d), F32),
                        pltpu.SemaphoreType.DMA((2 * n_win,)), pltpu.SemaphoreType.DMA(())],
    )
    return pl.pallas_call(
        functools.partial(_dispatch_kernel, n_exp=n_exp, n_tiles=n_tiles),
        grid_spec=grid_spec,
        out_shape=jax.ShapeDtypeStruct((n_tiles * TM_EXPERT, d), F32),
        compiler_params=_params(1),
        name="dispatch",
    )(pad_start, pad_cnt, n_used, chunk_tbl, chunk_tbl, lpost, h2)


def _expert_kernel(tile_end_ref, wup_ref, bup_ref, wdn_ref, bdn_ref, xs_ref, ys_ref,
                   wup_bf_ref, wdn_bf_ref, xbuf_ref, ybuf_ref, zero_ref, xsems, ysems, zsem, *, d_ff, n_tiles):
    e = pl.program_id(0)
    n_exp = pl.num_programs(0)
    tm = xbuf_ref.shape[1]
    n_used = tile_end_ref[n_exp - 1]
    first = jnp.where(e == 0, 0, tile_end_ref[jnp.maximum(e - 1, 0)])
    last = tile_end_ref[e]

    def rows_of(g):
        return pl.ds(pl.multiple_of(g * tm, tm), tm)

    def x_copy(g):
        return pltpu.make_async_copy(xs_ref.at[rows_of(g)], xbuf_ref.at[g % 2], xsems.at[g % 2])

    def y_copy(g):
        return pltpu.make_async_copy(ybuf_ref.at[g % 2], ys_ref.at[rows_of(g)], ysems.at[g % 2])

    def clear_copy(g):
        return pltpu.make_async_copy(zero_ref, ys_ref.at[rows_of(g)], zsem)

    @pl.when(e == 0)
    def _():
        zero_ref[...] = jnp.zeros_like(zero_ref)
        lax.fori_loop(n_used, n_tiles, lambda g, c: (clear_copy(g).start(), c)[1], 0)

        @pl.when(n_used > 0)
        def _():
            x_copy(0).start()

    wup_bf_ref[...] = wup_ref[0].astype(BF16)
    wdn_bf_ref[...] = wdn_ref[0].astype(BF16)

    def tile(g, carry):
        slot = g % 2
        x_copy(g).wait()

        @pl.when(g + 1 < n_used)
        def _():
            x_copy(g + 1).start()

        @pl.when(g >= 2)
        def _():
            y_copy(g - 2).wait()

        h = _dot(xbuf_ref[slot].astype(BF16), wup_bf_ref[...]) + bup_ref[0]
        glu = jnp.minimum(h[:, :d_ff], SWIGLU_LIMIT)
        lin = jnp.clip(h[:, d_ff:], -SWIGLU_LIMIT, SWIGLU_LIMIT)
        a = glu * jax.nn.sigmoid(SWIGLU_ALPHA * glu) * (lin + 1.0)
        ybuf_ref[slot] = _dot(a.astype(BF16), wdn_bf_ref[...]) + bdn_ref[0]
        y_copy(g).start()
        return carry

    lax.fori_loop(first, last, tile, 0)

    @pl.when(e == n_exp - 1)
    def _():
        @pl.when(n_used >= 2)
        def _():
            y_copy(n_used - 2).wait()

        @pl.when(n_used >= 1)
        def _():
            y_copy(n_used - 1).wait()

        lax.fori_loop(n_used, n_tiles, lambda g, c: (clear_copy(g).wait(), c)[1], 0)


def _experts(tile_end, xs, wup, bup, wdn, bdn):
    n_rows, d = xs.shape
    n_exp, d_ff, _ = wdn.shape
    tm = TM_EXPERT
    exp3 = lambda e, *_: (e, 0, 0)
    grid_spec = pltpu.PrefetchScalarGridSpec(
        num_scalar_prefetch=1,
        grid=(n_exp,),
        in_specs=[pl.BlockSpec((1, d, 2 * d_ff), exp3), pl.BlockSpec((1, 1, 2 * d_ff), exp3),
                  pl.BlockSpec((1, d_ff, d), exp3), pl.BlockSpec((1, 1, d), exp3),
                  pl.BlockSpec(memory_space=pl.ANY)],
        out_specs=pl.BlockSpec(memory_space=pl.ANY),
        scratch_shapes=[pltpu.VMEM((d, 2 * d_ff), BF16), pltpu.VMEM((d_ff, d), BF16),
                        pltpu.VMEM((2, tm, d), F32), pltpu.VMEM((2, tm, d), F32), pltpu.VMEM((tm, d), F32),
                        pltpu.SemaphoreType.DMA((2,)), pltpu.SemaphoreType.DMA((2,)),
                        pltpu.SemaphoreType.DMA(())],
    )
    return pl.pallas_call(
        functools.partial(_expert_kernel, d_ff=d_ff, n_tiles=n_rows // tm),
        grid_spec=grid_spec,
        out_shape=jax.ShapeDtypeStruct((n_rows, d), F32),
        compiler_params=_params(1),
        name="experts",
    )(tile_end, wup, bup, wdn, bdn, xs)


def _combine_kernel(tbl_ref, tbl_next_ref, lpos_ref, tw_ref, x1_ref, ys_ref, o_ref, buf_ref, sems,
                    *, n_exp, n_win):
    s = pl.program_id(0)
    ns = pl.num_programs(0)
    tm = x1_ref.shape[0] // n_win
    rows = buf_ref.shape[1]
    mine = (s % 2) * n_win
    other = n_win - mine

    def window_copies(table_ref, first_slot, wait):
        for k in range(n_win):
            _table_copies(table_ref, k, buf_ref.at[first_slot + k], ys_ref, sems.at[first_slot + k], tm, n_exp,
                          False, wait)

    @pl.when(s == 0)
    def _():
        buf_ref[...] = jnp.zeros_like(buf_ref)
        window_copies(tbl_ref, 0, False)

    @pl.when(s + 1 < ns)
    def _():
        window_copies(tbl_next_ref, other, False)

    window_copies(tbl_ref, mine, True)

    c_iota = lax.broadcasted_iota(I32, (tm, rows), 1)
    for k in range(n_win):
        sl = slice(k * tm, (k + 1) * tm)
        lpos = lpos_ref[sl, :]
        tw = tw_ref[sl, :]
        wperm = jnp.zeros((tm, rows), F32)
        for kk in range(TOP_K):
            wperm = jnp.where(c_iota == lpos[:, kk:kk + 1], tw[:, kk:kk + 1], wperm)
        o_ref[sl, :] = x1_ref[sl, :] + _dot(wperm.astype(BF16), buf_ref[mine + k].astype(BF16))


def _combine(chunk_tbl, lpos, tw, x1, ys, n_exp):
    t, d = x1.shape
    tm = min(TM_WINDOW, t)
    n_win = min(ROUTE_WINDOWS, t // tm)
    ns = t // (n_win * tm)
    row = lambda i: (i, 0)
    smem_blk = lambda shift: pl.BlockSpec((n_win * CHUNK_TABLE_LEN,), lambda i: (jnp.minimum(i + shift, ns - 1),),
                                          memory_space=pltpu.SMEM)
    return pl.pallas_call(
        functools.partial(_combine_kernel, n_exp=n_exp, n_win=n_win),
        grid=(ns,),
        in_specs=[smem_blk(0), smem_blk(1),
                  pl.BlockSpec((n_win * tm, LANES), row), pl.BlockSpec((n_win * tm, LANES), row),
                  pl.BlockSpec((n_win * tm, d), row), pl.BlockSpec(memory_space=pl.ANY)],
        out_specs=pl.BlockSpec((n_win * tm, d), row),
        out_shape=jax.ShapeDtypeStruct((t, d), F32),
        scratch_shapes=[pltpu.VMEM((2 * n_win, _block_rows(tm, n_exp), d), F32),
                        pltpu.SemaphoreType.DMA((2 * n_win,))],
        compiler_params=_params(1),
        name="combine",
    )(chunk_tbl, chunk_tbl, lpos, tw, x1, ys)


def _pad_lanes(a, value=0.0):
    return jnp.pad(a, ((0, 0), (0, LANES - a.shape[-1])), constant_values=value)


def _layer(x, mem, norm1_g, w_in, b_forget, b_gate, fox_q_g, fox_k_g, conv_w, mem_norm_g, w_mem_kv,
           xa_q_g, xa_k_g, w_branch, w_out, norm2_g, w_router, b_router, w_up, b_up, w_down, b_down):
    batch, seq, d = x.shape
    n_mem = mem.shape[1]
    n_branch, width, _ = w_branch.shape
    n_fh = b_forget.shape[0]
    hd = fox_q_g.shape[0]
    xa_hd = xa_q_g.shape[0]
    xa_heads = width // xa_hd
    n_exp = w_router.shape[1]
    assert n_fh * hd == width and n_branch == 3 and conv_w.shape[0] == 3
    assert LANES % hd == 0 and width % MXU_DIM == 0 and n_exp <= LANES and n_fh <= LANES
    fh_pad = -(-n_fh // SUBLANES) * SUBLANES
    t = batch * seq
    x2 = x.reshape(t, d)
    row1 = lambda a: a.reshape(1, -1).astype(F32)

    o_f = 3 * width
    o_conv = o_f + n_fh
    wqkv = w_in[:, :o_f].astype(BF16)
    wf = _pad_lanes(w_in[:, o_f:o_conv]).astype(BF16)
    wcat = w_in[:, o_conv:].astype(BF16)

    mk, mv = _memkv(mem.reshape(batch * n_mem, d), row1(mem_norm_g), w_mem_kv.astype(BF16), row1(xa_k_g),
                    xa_heads, xa_hd)
    q, k, v, cumt = _qkv(x2, row1(norm1_g), wqkv, wf, _pad_lanes(row1(b_forget)),
                         row1(jnp.tile(fox_q_g, n_fh)), row1(jnp.tile(fox_k_g, n_fh)),
                         seq, width, hd, fh_pad)
    y_fox = _fox(q, k, v, cumt, batch, seq, width, hd, fh_pad)

    wr = _pad_lanes(w_router.astype(F32))
    wr_hi = wr.astype(BF16)
    wr_lo = (wr - wr_hi.astype(F32)).astype(BF16)
    x1, h2, lpos, lpost, tw, tcnt, tbase = _merge(
        x2, y_fox, mk, mv, row1(norm1_g), wcat, b_gate.astype(F32), conv_w.astype(F32), row1(xa_q_g),
        w_branch.astype(BF16), w_out.astype(BF16), row1(norm2_g), wr_hi, wr_lo,
        _pad_lanes(row1(b_router), NEG_BIG), seq, n_mem, width, xa_heads, xa_hd)

    seg_len = tcnt[:, 0, :n_exp].astype(I32)
    seg_base = tbase[:, 0, :n_exp].astype(I32)
    counts = seg_base[-1] + seg_len[-1]
    tiles_per = (counts + TM_EXPERT - 1) // TM_EXPERT
    tile_end = jnp.cumsum(tiles_per)
    group_start = (tile_end - tiles_per) * TM_EXPERT
    seg_dst = group_start[None, :] + seg_base
    seg_src = jnp.cumsum(seg_len, axis=1) - seg_len
    n_tiles = -(-(t * TOP_K + (SUBLANES - 1) * seg_len.size) // TM_EXPERT) + n_exp
    n_used = tile_end[-1:].astype(I32)
    pad_start = (group_start + counts).astype(I32)
    pad_cnt = (tiles_per * TM_EXPERT - counts).astype(I32)
    chunk_tbl = _chunk_table(seg_len, seg_src.astype(I32), seg_dst.astype(I32), lpost.shape[2], n_exp)

    xs = _dispatch(pad_start, pad_cnt, n_used, chunk_tbl, lpost, h2, n_tiles, n_exp)
    ys = _experts(tile_end.astype(I32), xs, w_up.astype(F32), b_up[:, None, :].astype(F32),
                  w_down.astype(F32), b_down[:, None, :].astype(F32))
    out = _combine(chunk_tbl, lpos, tw, x1, ys, n_exp)
    return out.reshape(batch, seq, d)


def kernel(x, mem, norm1_g, w_in, b_forget, b_gate, fox_q_g, fox_k_g, conv_w, mem_norm_g, w_mem_kv, xa_q_g,
           xa_k_g, w_branch, w_out, norm2_g, w_router, b_router, w_up, b_up, w_down, b_down):
    for l in range(norm1_g.shape[0]):
        x = _layer(x, mem, norm1_g[l], w_in[l], b_forget[l], b_gate[l], fox_q_g[l], fox_k_g[l], conv_w[l],
                   mem_norm_g[l], w_mem_kv[l], xa_q_g[l], xa_k_g[l], w_branch[l], w_out[l], norm2_g[l],
                   w_router[l], b_router[l], w_up[l], b_up[l], w_down[l], b_down[l])
    return x
```
